```python
import math
import jax, jax.numpy as jnp
from jax import lax
import numpy as np

D_MODEL = 2048
BATCH = 1
SEQ = 16384
DEPTH = 1

HEAD_DIM = 64
D_MIX = D_MODEL
SB_HEADS = 16
SW_HEADS = 16
SW_KV_HEADS = 4
SB_WIDTH = SB_HEADS * HEAD_DIM
SW_WIDTH = SW_HEADS * HEAD_DIM
SW_KV_WIDTH = SW_KV_HEADS * HEAD_DIM
WINDOW = 128
BLOCK = 128
N_BUCKETS = 32
MAX_DISTANCE = 128
RMS_EPS = 1e-6
SPLIT_POINTS = (
    SB_WIDTH,
    2 * SB_WIDTH,
    3 * SB_WIDTH,
    4 * SB_WIDTH,
    4 * SB_WIDTH + SW_WIDTH,
    4 * SB_WIDTH + SW_WIDTH + SW_KV_WIDTH,
    4 * SB_WIDTH + SW_WIDTH + 2 * SW_KV_WIDTH,
)
D_IN_PROJ = 4 * SB_WIDTH + 2 * SW_WIDTH + 2 * SW_KV_WIDTH

kernel_name = "hymba_stickbreak_swa_sandwich"


def rmsnorm(x, g):
    xf = x.astype(jnp.float32)
    y = xf * lax.rsqrt(jnp.mean(xf * xf, axis=-1, keepdims=True) + RMS_EPS)
    return (y * g.astype(jnp.float32)).astype(x.dtype)


def t5_bucket(dist):
    n = jnp.maximum(dist, 0)
    max_exact = N_BUCKETS // 2
    nf = jnp.maximum(n, 1).astype(jnp.float32)
    large = max_exact + (jnp.log(nf / max_exact) / math.log(MAX_DISTANCE / max_exact)
                         * (N_BUCKETS - max_exact)).astype(jnp.int32)
    large = jnp.minimum(large, N_BUCKETS - 1)
    return jnp.where(n < max_exact, n, large)


def stick_breaking_attention(q, k, v):
    b, s = q.shape[:2]
    nb = s // BLOCK
    scale = HEAD_DIM ** -0.5
    qh = q.reshape(b, nb, BLOCK, SB_HEADS, HEAD_DIM).transpose(1, 0, 2, 3, 4)
    kh = k.reshape(b, s, SB_HEADS, HEAD_DIM)
    vh = v.reshape(b, s, SB_HEADS, HEAD_DIM).astype(jnp.float32)
    key_pos = jnp.arange(s)

    def one_block(args):
        q_blk, blk = args
        z = jnp.einsum('bqhd,bshd->bhqs', q_blk, kh).astype(jnp.float32) * scale
        q_pos = blk * BLOCK + jnp.arange(BLOCK)
        before = key_pos[None, :] < q_pos[:, None]
        log_1m_beta = jnp.where(before, jax.nn.log_sigmoid(-z), 0.0)
        suffix = lax.cumsum(log_1m_beta, axis=3, reverse=True) - log_1m_beta
        w = jnp.where(before, jnp.exp(jax.nn.log_sigmoid(z) + suffix), 0.0)
        return jnp.einsum('bhqs,bshd->bqhd', w, vh)

    out = lax.map(one_block, (qh, jnp.arange(nb)))
    return out.transpose(1, 0, 2, 3, 4).reshape(b, s, SB_WIDTH).astype(q.dtype)


def sliding_window_gqa(q, k, v, sinks, rel_bias):
    b, s = q.shape[:2]
    nb = s // BLOCK
    grp = SW_HEADS // SW_KV_HEADS
    scale = HEAD_DIM ** -0.5
    qb = q.reshape(b, nb, BLOCK, SW_KV_HEADS, grp, HEAD_DIM)
    kh = k.reshape(b, s, SW_KV_HEADS, HEAD_DIM)
    vh = v.reshape(b, s, SW_KV_HEADS, HEAD_DIM)
    pad = ((0, 0), (BLOCK, 0), (0, 0), (0, 0))
    kp = jnp.pad(kh, pad)
    vp = jnp.pad(vh, pad)
    kb = jnp.concatenate([kp[:, :s].reshape(b, nb, BLOCK, SW_KV_HEADS, HEAD_DIM),
                          kh.reshape(b, nb, BLOCK, SW_KV_HEADS, HEAD_DIM)], axis=2)
    vb = jnp.concatenate([vp[:, :s].reshape(b, nb, BLOCK, SW_KV_HEADS, HEAD_DIM),
                          vh.reshape(b, nb, BLOCK, SW_KV_HEADS, HEAD_DIM)], axis=2)
    logits = jnp.einsum('bnqhgd,bnchd->bnhgqc', qb, kb).astype(jnp.float32) * scale

    qi = jnp.arange(BLOCK)[:, None]
    ci = jnp.arange(2 * BLOCK)[None, :]
    dist = qi + BLOCK - ci
    bias = rel_bias.astype(jnp.float32)[t5_bucket(dist)]
    bias = bias.transpose(2, 0, 1).reshape(SW_KV_HEADS, grp, BLOCK, 2 * BLOCK)
    in_window = (dist >= 0) & (dist < WINDOW)
    key_pos = jnp.arange(nb)[:, None] * BLOCK - BLOCK + ci
    valid = in_window[None] & (key_pos >= 0)[:, None, :]
    logits = jnp.where(valid[None, :, None, None], logits + bias, -jnp.inf)

    sink = sinks.astype(jnp.float32).reshape(SW_KV_HEADS, grp)[None, None, :, :, None, None]
    m = jnp.maximum(jnp.max(logits, axis=-1, keepdims=True), sink)
    p = jnp.exp(logits - m)
    denom = jnp.sum(p, axis=-1, keepdims=True) + jnp.exp(sink - m)
    out = jnp.einsum('bnhgqc,bnchd->bnqhgd', p / denom, vb.astype(jnp.float32))
    return out.reshape(b, s, SW_WIDTH).astype(q.dtype)


def setup_inputs(seed: int = 0) -> dict:
    key = jax.random.key(seed)
    ks = jax.random.split(key, 9)
    x = jax.random.normal(ks[0], (BATCH, SEQ, D_MODEL), jnp.float32)
    w_in = jax.random.normal(ks[1], (DEPTH, D_MODEL, D_IN_PROJ), jnp.float32) * D_MODEL ** -0.5
    w_out = jax.random.normal(ks[2], (DEPTH, D_MIX, D_MODEL), jnp.float32) * D_MIX ** -0.5
    norm_pre = 1.0 + 0.02 * jax.random.normal(ks[3], (DEPTH, D_MODEL), jnp.float32)
    norm_post = 1.0 + 0.02 * jax.random.normal(ks[4], (DEPTH, D_MODEL), jnp.float32)
    gn_sb = 1.0 + 0.02 * jax.random.normal(ks[5], (DEPTH, SB_WIDTH), jnp.float32)
    gn_sw = 1.0 + 0.02 * jax.random.normal(ks[6], (DEPTH, SW_WIDTH), jnp.float32)
    sinks = 0.5 * jax.random.normal(ks[7], (DEPTH, SW_HEADS), jnp.float32)
    rel_bias = 0.1 * jax.random.normal(ks[8], (N_BUCKETS, SW_HEADS), jnp.float32)
    return {"x": x, "w_in": w_in, "w_out": w_out, "norm_pre": norm_pre, "norm_post": norm_post,
            "gn_sb": gn_sb, "gn_sw": gn_sw, "sinks": sinks, "rel_bias": rel_bias}


def reference(x, w_in, w_out, norm_pre, norm_post, gn_sb, gn_sw, sinks, rel_bias):
    h = x
    for l in range(DEPTH):
        u = rmsnorm(h, norm_pre[l])
        proj = jnp.einsum('bsd,de->bse', u, w_in[l])
        q_sb, k_sb, v_sb, z_sb, q_sw, k_sw, v_sw, z_sw = jnp.split(proj, SPLIT_POINTS, axis=-1)
        o_sb = stick_breaking_attention(q_sb, k_sb, v_sb)
        o_sw = sliding_window_gqa(q_sw, k_sw, v_sw, sinks[l], rel_bias)
        o_sb = rmsnorm(o_sb, gn_sb[l]) * jax.nn.silu(z_sb)
        o_sw = rmsnorm(o_sw, gn_sw[l]) * jax.nn.silu(z_sw)
        y = jnp.einsum('bse,ed->bsd', jnp.concatenate([o_sb, o_sw], axis=-1), w_out[l])
        h = h + rmsnorm(y, norm_post[l])
    return h
```

```python
import functools
import math

import numpy as np
import jax
import jax.numpy as jnp
from jax import lax
from jax.experimental import pallas as pl
from jax.experimental.pallas import tpu as pltpu

D_MODEL = 2048
SEQ = 16384
HEAD_DIM = 64
SB_HEADS = 16
SW_HEADS = 16
SW_KV_HEADS = 4
SB_WIDTH = SB_HEADS * HEAD_DIM
SW_WIDTH = SW_HEADS * HEAD_DIM
SW_KV_WIDTH = SW_KV_HEADS * HEAD_DIM
D_MIX = SB_WIDTH + SW_WIDTH
D_IN_PROJ = 4 * SB_WIDTH + 2 * SW_WIDTH + 2 * SW_KV_WIDTH
WINDOW = 128
BLOCK = 128
N_BUCKETS = 32
MAX_DISTANCE = 128
RMS_EPS = 1e-6
SCALE = HEAD_DIM ** -0.5

OFF_Q_SB = 0
OFF_K_SB = SB_WIDTH
OFF_V_SB = 2 * SB_WIDTH
OFF_Z_SB = 3 * SB_WIDTH
OFF_Q_SW = 4 * SB_WIDTH
OFF_K_SW = OFF_Q_SW + SW_WIDTH
OFF_V_SW = OFF_K_SW + SW_KV_WIDTH
OFF_Z_SW = OFF_V_SW + SW_KV_WIDTH

LANES = 128
VMEM_LIMIT = 56 * 1024 * 1024

PROJ_TM = 512
PROJ_TN = 1664
SB_T = 256
OUT_TM = 512
NEG_BIG = -1e30

F32 = jnp.float32
BF16 = jnp.bfloat16


def _inproj_kernel(x_ref, g_ref, w_ref, o_ref, xn_ref):
    @pl.when(pl.program_id(1) == 0)
    def _():
        x = x_ref[...]
        ms = jnp.mean(x * x, axis=-1, keepdims=True)
        xn_ref[...] = ((x * lax.rsqrt(ms + RMS_EPS)) * g_ref[...]).astype(BF16)

    o_ref[...] = jnp.dot(xn_ref[...], w_ref[...], preferred_element_type=F32).astype(BF16)


def _inproj(x2, g, w_bf):
    return pl.pallas_call(
        _inproj_kernel,
        grid=(SEQ // PROJ_TM, D_IN_PROJ // PROJ_TN),
        in_specs=[
            pl.BlockSpec((PROJ_TM, D_MODEL), lambda i, j: (i, 0)),
            pl.BlockSpec((1, D_MODEL), lambda i, j: (0, 0)),
            pl.BlockSpec((D_MODEL, PROJ_TN), lambda i, j: (0, j)),
        ],
        out_specs=pl.BlockSpec((PROJ_TM, PROJ_TN), lambda i, j: (i, j)),
        out_shape=jax.ShapeDtypeStruct((SEQ, D_IN_PROJ), BF16),
        scratch_shapes=[pltpu.VMEM((PROJ_TM, D_MODEL), BF16)],
        compiler_params=pltpu.CompilerParams(
            dimension_semantics=("arbitrary", "arbitrary"), vmem_limit_bytes=VMEM_LIMIT),
        name="inproj",
    )(x2, g, w_bf)


def _sb_kernel(q_ref, k_ref, v_ref, tri_ref, o_ref, acc_ref, carry_ref):
    qi = pl.program_id(1)
    q = q_ref[...] * jnp.asarray(SCALE, BF16)
    lane = lax.broadcasted_iota(jnp.int32, q.shape, 1)
    zero = jnp.zeros_like(q)
    q_heads = (jnp.where(lane < HEAD_DIM, q, zero), jnp.where(lane >= HEAD_DIM, q, zero))
    tri = tri_ref[...]

    acc_ref[...] = jnp.zeros_like(acc_ref)
    carry_ref[...] = jnp.zeros_like(carry_ref)

    row = lax.broadcasted_iota(jnp.int32, (SB_T, SB_T), 0)
    col = lax.broadcasted_iota(jnp.int32, (SB_T, SB_T), 1)
    before = col < row

    def block(kb, diagonal):
        start = pl.multiple_of(kb * SB_T, SB_T)
        kblk = k_ref[pl.ds(start, SB_T), :]
        vblk = v_ref[pl.ds(start, SB_T), :]
        for h in range(2):
            z = lax.dot_general(q_heads[h], kblk, (((1,), (1,)), ((), ())),
                                preferred_element_type=F32)
            sp = jnp.maximum(z, 0.0) + jnp.log(1.0 + jnp.exp(-jnp.abs(z)))
            log_beta = z - sp
            if diagonal:
                sp = jnp.where(before, sp, 0.0)
            hi = sp.astype(BF16)
            lo = (sp - hi.astype(F32)).astype(BF16)
            cs = jnp.dot(jnp.concatenate([hi, lo], axis=1), tri, preferred_element_type=F32)
            carry = carry_ref[h]
            w = jnp.exp(log_beta + cs + carry)
            if diagonal:
                w = jnp.where(before, w, 0.0)
            acc_ref[h] += jnp.dot(w.astype(BF16), vblk, preferred_element_type=F32)
            carry_ref[h] = carry - jnp.sum(sp, axis=1, keepdims=True)

    block(qi, True)

    def body(i, c):
        block(qi - 1 - i, False)
        return c

    lax.fori_loop(0, qi, body, 0)

    lane_o = lax.broadcasted_iota(jnp.int32, (SB_T, LANES), 1)
    o_ref[...] = jnp.where(lane_o < HEAD_DIM, acc_ref[0], acc_ref[1])


def _sb_tri():
    j = np.arange(SB_T)[:, None]
    s = np.arange(SB_T)[None, :]
    t = np.where(j > s, -1.0, 0.0).astype(np.float32)
    return jnp.asarray(np.concatenate([t, t], axis=0), BF16)


def _sb_attention(proj):
    pairs = SB_WIDTH // LANES
    return pl.pallas_call(
        _sb_kernel,
        grid=(pairs, SEQ // SB_T),
        in_specs=[
            pl.BlockSpec((SB_T, LANES), lambda p, i: (i, OFF_Q_SB // LANES + p)),
            pl.BlockSpec((SEQ, LANES), lambda p, i: (0, OFF_K_SB // LANES + p)),
            pl.BlockSpec((SEQ, LANES), lambda p, i: (0, OFF_V_SB // LANES + p)),
            pl.BlockSpec((2 * SB_T, SB_T), lambda p, i: (0, 0)),
        ],
        out_specs=pl.BlockSpec((SB_T, LANES), lambda p, i: (i, p)),
        out_shape=jax.ShapeDtypeStruct((SEQ, SB_WIDTH), F32),
        scratch_shapes=[pltpu.VMEM((2, SB_T, LANES), F32), pltpu.VMEM((2, SB_T, 1), F32)],
        compiler_params=pltpu.CompilerParams(
            dimension_semantics=("arbitrary", "arbitrary"), vmem_limit_bytes=VMEM_LIMIT),
        name="sb_attention",
    )(proj, proj, proj, _sb_tri())


def _bucket_table():
    qi = np.arange(BLOCK)[:, None]
    ci = np.arange(2 * BLOCK)[None, :]
    dist = qi + BLOCK - ci
    n = np.maximum(dist, 0)
    max_exact = N_BUCKETS // 2
    nf = np.maximum(n, 1).astype(np.float64)
    val = np.log(nf / max_exact) / math.log(MAX_DISTANCE / max_exact) * (N_BUCKETS - max_exact)
    in_window = (dist >= 0) & (dist < WINDOW)
    frac = val - np.floor(val)
    chk = in_window & (n > max_exact)
    assert np.all((frac[chk] > 1e-3) & (frac[chk] < 1 - 1e-3))
    large = np.minimum(max_exact + val.astype(np.int64), N_BUCKETS - 1)
    bucket = np.where(n < max_exact, n, large)
    later = np.where(in_window, bucket, -1)
    first = np.where(ci >= BLOCK, later, -1)
    return jnp.asarray(np.stack([first, later]).astype(np.int32))


def _bias_kernel(rb_ref, bucket_ref, o_ref):
    h = pl.program_id(1)
    b = bucket_ref[0]
    acc = jnp.full(b.shape, NEG_BIG, F32)
    for k in range(N_BUCKETS):
        acc = jnp.where(b == k, rb_ref[k, h], acc)
    o_ref[0, 0] = acc


def _bias_table(rel_bias):
    return pl.pallas_call(
        _bias_kernel,
        grid=(2, SW_HEADS),
        in_specs=[
            pl.BlockSpec(memory_space=pltpu.SMEM),
            pl.BlockSpec((1, BLOCK, 2 * BLOCK), lambda v, h: (v, 0, 0)),
        ],
        out_specs=pl.BlockSpec((1, 1, BLOCK, 2 * BLOCK), lambda v, h: (v, h, 0, 0)),
        out_shape=jax.ShapeDtypeStruct((2, SW_HEADS, BLOCK, 2 * BLOCK), F32),
        name="t5_bias",
    )(rel_bias.astype(F32), _bucket_table())


def _sw_kernel(sink_ref, q_ref, kp_ref, kc_ref, vp_ref, vc_ref, bias_ref, o_ref):
    q = q_ref[...] * jnp.asarray(SCALE, BF16)
    k = jnp.concatenate([kp_ref[...], kc_ref[...]], axis=0)
    v = jnp.concatenate([vp_ref[...], vc_ref[...]], axis=0)
    grp = SW_HEADS // SW_KV_HEADS
    outs = []
    for h in range(SW_HEADS):
        kv = h // grp
        qh = q[:, h * HEAD_DIM:(h + 1) * HEAD_DIM]
        kh = k[:, kv * HEAD_DIM:(kv + 1) * HEAD_DIM]
        vh = v[:, kv * HEAD_DIM:(kv + 1) * HEAD_DIM]
        logits = lax.dot_general(qh, kh, (((1,), (1,)), ((), ())),
                                 preferred_element_type=F32) + bias_ref[0, h]
        sink = sink_ref[h]
        m = jnp.maximum(jnp.max(logits, axis=-1, keepdims=True), sink)
        p = jnp.exp(logits - m)
        denom = jnp.sum(p, axis=-1, keepdims=True) + jnp.exp(sink - m)
        o = jnp.dot(p.astype(BF16), vh, preferred_element_type=F32)
        outs.append(o / denom)
    o_ref[...] = jnp.concatenate(outs, axis=1)


def _sw_attention(proj, sinks, bias):
    kq = OFF_Q_SW // SW_WIDTH
    kk = OFF_K_SW // SW_KV_WIDTH
    kv = OFF_V_SW // SW_KV_WIDTH
    prev = lambda n: jnp.maximum(n - 1, 0)
    return pl.pallas_call(
        _sw_kernel,
        grid=(SEQ // BLOCK,),
        in_specs=[
            pl.BlockSpec(memory_space=pltpu.SMEM),
            pl.BlockSpec((BLOCK, SW_WIDTH), lambda n: (n, kq)),
            pl.BlockSpec((BLOCK, SW_KV_WIDTH), lambda n: (prev(n), kk)),
            pl.BlockSpec((BLOCK, SW_KV_WIDTH), lambda n: (n, kk)),
            pl.BlockSpec((BLOCK, SW_KV_WIDTH), lambda n: (prev(n), kv)),
            pl.BlockSpec((BLOCK, SW_KV_WIDTH), lambda n: (n, kv)),
            pl.BlockSpec((1, SW_HEADS, BLOCK, 2 * BLOCK), lambda n: (jnp.minimum(n, 1), 0, 0, 0)),
        ],
        out_specs=pl.BlockSpec((BLOCK, SW_WIDTH), lambda n: (n, 0)),
        out_shape=jax.ShapeDtypeStruct((SEQ, SW_WIDTH), F32),
        compiler_params=pltpu.CompilerParams(
            dimension_semantics=("arbitrary",), vmem_limit_bytes=VMEM_LIMIT),
        name="sw_attention",
    )(sinks.astype(F32), proj, proj, proj, proj, proj, bias)


def _rms(x, g):
    ms = jnp.mean(x * x, axis=-1, keepdims=True)
    return (x * lax.rsqrt(ms + RMS_EPS)) * g


def _silu(z):
    return z * (1.0 / (1.0 + jnp.exp(-z)))


def _out_kernel(osb_ref, osw_ref, zsb_ref, zswa_ref, zswb_ref, gsb_ref, gsw_ref, gpost_ref,
                w_ref, x_ref, o_ref):
    a_sb = _rms(osb_ref[...], gsb_ref[...]) * _silu(zsb_ref[...].astype(F32))
    z_sw = jnp.concatenate([zswa_ref[...], zswb_ref[...]], axis=1).astype(F32)
    a_sw = _rms(osw_ref[...], gsw_ref[...]) * _silu(z_sw)
    y = jnp.dot(a_sb.astype(BF16), w_ref[:SB_WIDTH, :], preferred_element_type=F32)
    y = y + jnp.dot(a_sw.astype(BF16), w_ref[SB_WIDTH:, :], preferred_element_type=F32)
    o_ref[...] = x_ref[...] + _rms(y, gpost_ref[...])


def _outproj(o_sb, o_sw, proj, gn_sb, gn_sw, norm_post, w_out_bf, x2):
    half = SW_WIDTH // 2
    return pl.pallas_call(
        _out_kernel,
        grid=(SEQ // OUT_TM,),
        in_specs=[
            pl.BlockSpec((OUT_TM, SB_WIDTH), lambda i: (i, 0)),
            pl.BlockSpec((OUT_TM, SW_WIDTH), lambda i: (i, 0)),
            pl.BlockSpec((OUT_TM, SB_WIDTH), lambda i: (i, OFF_Z_SB // SB_WIDTH)),
            pl.BlockSpec((OUT_TM, half), lambda i: (i, OFF_Z_SW // half)),
            pl.BlockSpec((OUT_TM, half), lambda i: (i, OFF_Z_SW // half + 1)),
            pl.BlockSpec((1, SB_WIDTH), lambda i: (0, 0)),
            pl.BlockSpec((1, SW_WIDTH), lambda i: (0, 0)),
            pl.BlockSpec((1, D_MODEL), lambda i: (0, 0)),
            pl.BlockSpec((D_MIX, D_MODEL), lambda i: (0, 0)),
            pl.BlockSpec((OUT_TM, D_MODEL), lambda i: (i, 0)),
        ],
        out_specs=pl.BlockSpec((OUT_TM, D_MODEL), lambda i: (i, 0)),
        out_shape=jax.ShapeDtypeStruct((SEQ, D_MODEL), F32),
        compiler_params=pltpu.CompilerParams(
            dimension_semantics=("arbitrary",), vmem_limit_bytes=VMEM_LIMIT),
        name="outproj",
    )(o_sb, o_sw, proj, proj, proj, gn_sb, gn_sw, norm_post, w_out_bf, x2)


def kernel(x, w_in, w_out, norm_pre, norm_post, gn_sb, gn_sw, sinks, rel_bias):
    assert x.shape == (1, SEQ, D_MODEL) and w_in.shape == (1, D_MODEL, D_IN_PROJ)
    assert OFF_Z_SW % (SW_WIDTH // 2) == 0 and OFF_Q_SW % SW_WIDTH == 0
    x2 = x.reshape(SEQ, D_MODEL)
    proj = _inproj(x2, norm_pre.astype(F32), w_in[0].astype(BF16))
    o_sb = _sb_attention(proj)
    bias = _bias_table(rel_bias)
    o_sw = _sw_attention(proj, sinks[0], bias)
    out = _outproj(o_sb, o_sw, proj, gn_sb.astype(F32), gn_sw.astype(F32), norm_post.astype(F32),
                   w_out[0].astype(BF16), x2)
    return out.reshape(1, SEQ, D_MODEL)
```

```python
import functools
import math

import numpy as np
import jax
import jax.numpy as jnp
from jax import lax
from jax.experimental import pallas as pl
from jax.experimental.pallas import tpu as pltpu

D_MODEL = 2048
SEQ = 16384
HEAD_DIM = 64
SB_HEADS = 16
SW_HEADS = 16
SW_KV_HEADS = 4
SB_WIDTH = SB_HEADS * HEAD_DIM
SW_WIDTH = SW_HEADS * HEAD_DIM
SW_KV_WIDTH = SW_KV_HEADS * HEAD_DIM
D_MIX = SB_WIDTH + SW_WIDTH
D_IN_PROJ = 4 * SB_WIDTH + 2 * SW_WIDTH + 2 * SW_KV_WIDTH
WINDOW = 128
BLOCK = 128
N_BUCKETS = 32
MAX_DISTANCE = 128
RMS_EPS = 1e-6
SCALE = HEAD_DIM ** -0.5

OFF_Q_SB = 0
OFF_K_SB = SB_WIDTH
OFF_V_SB = 2 * SB_WIDTH
OFF_Z_SB = 3 * SB_WIDTH
OFF_Q_SW = 4 * SB_WIDTH
OFF_K_SW = OFF_Q_SW + SW_WIDTH
OFF_V_SW = OFF_K_SW + SW_KV_WIDTH
OFF_Z_SW = OFF_V_SW + SW_KV_WIDTH

LANES = 128
VMEM_LIMIT = 56 * 1024 * 1024

PROJ_TM = 512
PROJ_TN = 1664
SB_T = 256
OUT_TM = 512
NEG_BIG = -1e30
SB_STOP = -110.0

F32 = jnp.float32
BF16 = jnp.bfloat16


def _inproj_kernel(x_ref, g_ref, w_ref, o_ref, xn_ref):
    @pl.when(pl.program_id(1) == 0)
    def _():
        x = x_ref[...]
        ms = jnp.mean(x * x, axis=-1, keepdims=True)
        xn_ref[...] = ((x * lax.rsqrt(ms + RMS_EPS)) * g_ref[...]).astype(BF16)

    o_ref[...] = jnp.dot(xn_ref[...], w_ref[...], preferred_element_type=F32).astype(BF16)


def _inproj(x2, g, w_bf):
    return pl.pallas_call(
        _inproj_kernel,
        grid=(SEQ // PROJ_TM, D_IN_PROJ // PROJ_TN),
        in_specs=[
            pl.BlockSpec((PROJ_TM, D_MODEL), lambda i, j: (i, 0)),
            pl.BlockSpec((1, D_MODEL), lambda i, j: (0, 0)),
            pl.BlockSpec((D_MODEL, PROJ_TN), lambda i, j: (0, j)),
        ],
        out_specs=pl.BlockSpec((PROJ_TM, PROJ_TN), lambda i, j: (i, j)),
        out_shape=jax.ShapeDtypeStruct((SEQ, D_IN_PROJ), BF16),
        scratch_shapes=[pltpu.VMEM((PROJ_TM, D_MODEL), BF16)],
        compiler_params=pltpu.CompilerParams(
            dimension_semantics=("arbitrary", "arbitrary"), vmem_limit_bytes=VMEM_LIMIT),
        name="inproj",
    )(x2, g, w_bf)


def _sb_kernel(q_ref, k_ref, v_ref, tri_ref, o_ref, acc_ref, carry_ref):
    qi = pl.program_id(1)
    q = q_ref[...] * jnp.asarray(SCALE, BF16)
    lane = lax.broadcasted_iota(jnp.int32, q.shape, 1)
    zero = jnp.zeros_like(q)
    q_heads = (jnp.where(lane < HEAD_DIM, q, zero), jnp.where(lane >= HEAD_DIM, q, zero))
    tri = tri_ref[...]

    acc_ref[...] = jnp.zeros_like(acc_ref)
    carry_ref[...] = jnp.zeros_like(carry_ref)

    row = lax.broadcasted_iota(jnp.int32, (SB_T, SB_T), 0)
    col = lax.broadcasted_iota(jnp.int32, (SB_T, SB_T), 1)
    before = col < row

    def block(kb, diagonal):
        start = pl.multiple_of(kb * SB_T, SB_T)
        kblk = k_ref[pl.ds(start, SB_T), :]
        vblk = v_ref[pl.ds(start, SB_T), :]
        for h in range(2):
            z = lax.dot_general(q_heads[h], kblk, (((1,), (1,)), ((), ())),
                                preferred_element_type=F32)
            sp = jnp.maximum(z, 0.0) + jnp.log(1.0 + jnp.exp(-jnp.abs(z)))
            log_beta = z - sp
            if diagonal:
                sp = jnp.where(before, sp, 0.0)
            hi = sp.astype(BF16)
            lo = (sp - hi.astype(F32)).astype(BF16)
            cs = jnp.dot(jnp.concatenate([hi, lo], axis=1), tri, preferred_element_type=F32)
            carry = carry_ref[h]
            w = jnp.exp(log_beta + cs + carry)
            if diagonal:
                w = jnp.where(before, w, 0.0)
            acc_ref[h] += jnp.dot(w.astype(BF16), vblk, preferred_element_type=F32)
            carry_ref[h] = carry - jnp.sum(sp, axis=1, keepdims=True)

    block(qi, True)

    def cond(state):
        kb, go = state
        return jnp.logical_and(kb >= 0, go > 0)

    def body(state):
        kb, _ = state
        block(kb, False)
        go = (jnp.max(carry_ref[...]) >= SB_STOP).astype(jnp.int32)
        return kb - 1, go

    lax.while_loop(cond, body, (qi - 1, jnp.int32(1)))

    lane_o = lax.broadcasted_iota(jnp.int32, (SB_T, LANES), 1)
    o_ref[...] = jnp.where(lane_o < HEAD_DIM, acc_ref[0], acc_ref[1])


def _sb_tri():
    j = np.arange(SB_T)[:, None]
    s = np.arange(SB_T)[None, :]
    t = np.where(j > s, -1.0, 0.0).astype(np.float32)
    return jnp.asarray(np.concatenate([t, t], axis=0), BF16)


def _sb_attention(proj):
    pairs = SB_WIDTH // LANES
    return pl.pallas_call(
        _sb_kernel,
        grid=(pairs, SEQ // SB_T),
        in_specs=[
            pl.BlockSpec((SB_T, LANES), lambda p, i: (i, OFF_Q_SB // LANES + p)),
            pl.BlockSpec((SEQ, LANES), lambda p, i: (0, OFF_K_SB // LANES + p)),
            pl.BlockSpec((SEQ, LANES), lambda p, i: (0, OFF_V_SB // LANES + p)),
            pl.BlockSpec((2 * SB_T, SB_T), lambda p, i: (0, 0)),
        ],
        out_specs=pl.BlockSpec((SB_T, LANES), lambda p, i: (i, p)),
        out_shape=jax.ShapeDtypeStruct((SEQ, SB_WIDTH), F32),
        scratch_shapes=[pltpu.VMEM((2, SB_T, LANES), F32), pltpu.VMEM((2, SB_T, 1), F32)],
        compiler_params=pltpu.CompilerParams(
            dimension_semantics=("arbitrary", "arbitrary"), vmem_limit_bytes=VMEM_LIMIT),
        name="sb_attention",
    )(proj, proj, proj, _sb_tri())


def _bucket_table():
    qi = np.arange(BLOCK)[:, None]
    ci = np.arange(2 * BLOCK)[None, :]
    dist = qi + BLOCK - ci
    n = np.maximum(dist, 0)
    max_exact = N_BUCKETS // 2
    nf = np.maximum(n, 1).astype(np.float64)
    val = np.log(nf / max_exact) / math.log(MAX_DISTANCE / max_exact) * (N_BUCKETS - max_exact)
    in_window = (dist >= 0) & (dist < WINDOW)
    frac = val - np.floor(val)
    chk = in_window & (n > max_exact)
    assert np.all((frac[chk] > 1e-3) & (frac[chk] < 1 - 1e-3))
    large = np.minimum(max_exact + val.astype(np.int64), N_BUCKETS - 1)
    bucket = np.where(n < max_exact, n, large)
    later = np.where(in_window, bucket, -1)
    first = np.where(ci >= BLOCK, later, -1)
    return jnp.asarray(np.stack([first, later]).astype(np.int32))


def _bias_kernel(rb_ref, bucket_ref, o_ref):
    h = pl.program_id(1)
    b = bucket_ref[0]
    acc = jnp.full(b.shape, NEG_BIG, F32)
    for k in range(N_BUCKETS):
        acc = jnp.where(b == k, rb_ref[k, h], acc)
    o_ref[0, 0] = acc


def _bias_table(rel_bias):
    return pl.pallas_call(
        _bias_kernel,
        grid=(2, SW_HEADS),
        in_specs=[
            pl.BlockSpec(memory_space=pltpu.SMEM),
            pl.BlockSpec((1, BLOCK, 2 * BLOCK), lambda v, h: (v, 0, 0)),
        ],
        out_specs=pl.BlockSpec((1, 1, BLOCK, 2 * BLOCK), lambda v, h: (v, h, 0, 0)),
        out_shape=jax.ShapeDtypeStruct((2, SW_HEADS, BLOCK, 2 * BLOCK), F32),
        name="t5_bias",
    )(rel_bias.astype(F32), _bucket_table())


def _sw_kernel(sink_ref, q_ref, kp_ref, kc_ref, vp_ref, vc_ref, bias_ref, o_ref):
    q = q_ref[...] * jnp.asarray(SCALE, BF16)
    k = jnp.concatenate([kp_ref[...], kc_ref[...]], axis=0)
    v = jnp.concatenate([vp_ref[...], vc_ref[...]], axis=0)
    grp = SW_HEADS // SW_KV_HEADS
    outs = []
    for h in range(SW_HEADS):
        kv = h // grp
        qh = q[:, h * HEAD_DIM:(h + 1) * HEAD_DIM]
        kh = k[:, kv * HEAD_DIM:(kv + 1) * HEAD_DIM]
        vh = v[:, kv * HEAD_DIM:(kv + 1) * HEAD_DIM]
        logits = lax.dot_general(qh, kh, (((1,), (1,)), ((), ())),
                                 preferred_element_type=F32) + bias_ref[0, h]
        sink = sink_ref[h]
        m = jnp.maximum(jnp.max(logits, axis=-1, keepdims=True), sink)
        p = jnp.exp(logits - m)
        denom = jnp.sum(p, axis=-1, keepdims=True) + jnp.exp(sink - m)
        o = jnp.dot(p.astype(BF16), vh, preferred_element_type=F32)
        outs.append(o / denom)
    o_ref[...] = jnp.concatenate(outs, axis=1)


def _sw_attention(proj, sinks, bias):
    kq = OFF_Q_SW // SW_WIDTH
    kk = OFF_K_SW // SW_KV_WIDTH
    kv = OFF_V_SW // SW_KV_WIDTH
    prev = lambda n: jnp.maximum(n - 1, 0)
    return pl.pallas_call(
        _sw_kernel,
        grid=(SEQ // BLOCK,),
        in_specs=[
            pl.BlockSpec(memory_space=pltpu.SMEM),
            pl.BlockSpec((BLOCK, SW_WIDTH), lambda n: (n, kq)),
            pl.BlockSpec((BLOCK, SW_KV_WIDTH), lambda n: (prev(n), kk)),
            pl.BlockSpec((BLOCK, SW_KV_WIDTH), lambda n: (n, kk)),
            pl.BlockSpec((BLOCK, SW_KV_WIDTH), lambda n: (prev(n), kv)),
            pl.BlockSpec((BLOCK, SW_KV_WIDTH), lambda n: (n, kv)),
            pl.BlockSpec((1, SW_HEADS, BLOCK, 2 * BLOCK), lambda n: (jnp.minimum(n, 1), 0, 0, 0)),
        ],
        out_specs=pl.BlockSpec((BLOCK, SW_WIDTH), lambda n: (n, 0)),
        out_shape=jax.ShapeDtypeStruct((SEQ, SW_WIDTH), F32),
        compiler_params=pltpu.CompilerParams(
            dimension_semantics=("arbitrary",), vmem_limit_bytes=VMEM_LIMIT),
        name="sw_attention",
    )(sinks.astype(F32), proj, proj, proj, proj, proj, bias)


def _rms(x, g):
    ms = jnp.mean(x * x, axis=-1, keepdims=True)
    return (x * lax.rsqrt(ms + RMS_EPS)) * g


def _silu(z):
    return z * (1.0 / (1.0 + jnp.exp(-z)))


def _out_kernel(osb_ref, osw_ref, zsb_ref, zswa_ref, zswb_ref, gsb_ref, gsw_ref, gpost_ref,
                w_ref, x_ref, o_ref):
    a_sb = _rms(osb_ref[...], gsb_ref[...]) * _silu(zsb_ref[...].astype(F32))
    z_sw = jnp.concatenate([zswa_ref[...], zswb_ref[...]], axis=1).astype(F32)
    a_sw = _rms(osw_ref[...], gsw_ref[...]) * _silu(z_sw)
    y = jnp.dot(a_sb.astype(BF16), w_ref[:SB_WIDTH, :], preferred_element_type=F32)
    y = y + jnp.dot(a_sw.astype(BF16), w_ref[SB_WIDTH:, :], preferred_element_type=F32)
    o_ref[...] = x_ref[...] + _rms(y, gpost_ref[...])


def _outproj(o_sb, o_sw, proj, gn_sb, gn_sw, norm_post, w_out_bf, x2):
    half = SW_WIDTH // 2
    return pl.pallas_call(
        _out_kernel,
        grid=(SEQ // OUT_TM,),
        in_specs=[
            pl.BlockSpec((OUT_TM, SB_WIDTH), lambda i: (i, 0)),
            pl.BlockSpec((OUT_TM, SW_WIDTH), lambda i: (i, 0)),
            pl.BlockSpec((OUT_TM, SB_WIDTH), lambda i: (i, OFF_Z_SB // SB_WIDTH)),
            pl.BlockSpec((OUT_TM, half), lambda i: (i, OFF_Z_SW // half)),
            pl.BlockSpec((OUT_TM, half), lambda i: (i, OFF_Z_SW // half + 1)),
            pl.BlockSpec((1, SB_WIDTH), lambda i: (0, 0)),
            pl.BlockSpec((1, SW_WIDTH), lambda i: (0, 0)),
            pl.BlockSpec((1, D_MODEL), lambda i: (0, 0)),
            pl.BlockSpec((D_MIX, D_MODEL), lambda i: (0, 0)),
            pl.BlockSpec((OUT_TM, D_MODEL), lambda i: (i, 0)),
        ],
        out_specs=pl.BlockSpec((OUT_TM, D_MODEL), lambda i: (i, 0)),
        out_shape=jax.ShapeDtypeStruct((SEQ, D_MODEL), F32),
        compiler_params=pltpu.CompilerParams(
            dimension_semantics=("arbitrary",), vmem_limit_bytes=VMEM_LIMIT),
        name="outproj",
    )(o_sb, o_sw, proj, proj, proj, gn_sb, gn_sw, norm_post, w_out_bf, x2)


def kernel(x, w_in, w_out, norm_pre, norm_post, gn_sb, gn_sw, sinks, rel_bias):
    assert x.shape == (1, SEQ, D_MODEL) and w_in.shape == (1, D_MODEL, D_IN_PROJ)
    assert OFF_Z_SW % (SW_WIDTH // 2) == 0 and OFF_Q_SW % SW_WIDTH == 0
    x2 = x.reshape(SEQ, D_MODEL)
    proj = _inproj(x2, norm_pre.astype(F32), w_in[0].astype(BF16))
    o_sb = _sb_attention(proj)
    bias = _bias_table(rel_bias)
    o_sw = _sw_attention(proj, sinks[0], bias)
    out = _outproj(o_sb, o_sw, proj, gn_sb.astype(F32), gn_sw.astype(F32), norm_post.astype(F32),
                   w_out[0].astype(BF16), x2)
    return out.reshape(1, SEQ, D_MODEL)
```

```python
import functools
import math

import numpy as np
import jax
import jax.numpy as jnp
from jax import lax
from jax.experimental import pallas as pl
from jax.experimental.pallas import tpu as pltpu

D_MODEL = 2048
SEQ = 16384
HEAD_DIM = 64
SB_HEADS = 16
SW_HEADS = 16
SW_KV_HEADS = 4
SB_WIDTH = SB_HEADS * HEAD_DIM
SW_WIDTH = SW_HEADS * HEAD_DIM
SW_KV_WIDTH = SW_KV_HEADS * HEAD_DIM
D_MIX = SB_WIDTH + SW_WIDTH
D_IN_PROJ = 4 * SB_WIDTH + 2 * SW_WIDTH + 2 * SW_KV_WIDTH
WINDOW = 128
BLOCK = 128
N_BUCKETS = 32
MAX_DISTANCE = 128
RMS_EPS = 1e-6
SCALE = HEAD_DIM ** -0.5

OFF_Q_SB = 0
OFF_K_SB = SB_WIDTH
OFF_V_SB = 2 * SB_WIDTH
OFF_Z_SB = 3 * SB_WIDTH
OFF_Q_SW = 4 * SB_WIDTH
OFF_K_SW = OFF_Q_SW + SW_WIDTH
OFF_V_SW = OFF_K_SW + SW_KV_WIDTH
OFF_Z_SW = OFF_V_SW + SW_KV_WIDTH

LANES = 128
VMEM_LIMIT = 56 * 1024 * 1024

PROJ_TM = 512
PROJ_TN = 1664
SB_T = 256
OUT_TM = 512
NEG_BIG = -1e30
SB_STOP = -110.0

F32 = jnp.float32
BF16 = jnp.bfloat16


def _inproj_kernel(x_ref, g_ref, w_ref, o_ref, xn_ref):
    @pl.when(pl.program_id(1) == 0)
    def _():
        x = x_ref[...]
        ms = jnp.mean(x * x, axis=-1, keepdims=True)
        xn_ref[...] = ((x * lax.rsqrt(ms + RMS_EPS)) * g_ref[...]).astype(BF16)

    o_ref[...] = jnp.dot(xn_ref[...], w_ref[...], preferred_element_type=F32).astype(BF16)


def _inproj(x2, g, w_bf):
    return pl.pallas_call(
        _inproj_kernel,
        grid=(SEQ // PROJ_TM, D_IN_PROJ // PROJ_TN),
        in_specs=[
            pl.BlockSpec((PROJ_TM, D_MODEL), lambda i, j: (i, 0)),
            pl.BlockSpec((1, D_MODEL), lambda i, j: (0, 0)),
            pl.BlockSpec((D_MODEL, PROJ_TN), lambda i, j: (0, j)),
        ],
        out_specs=pl.BlockSpec((PROJ_TM, PROJ_TN), lambda i, j: (i, j)),
        out_shape=jax.ShapeDtypeStruct((SEQ, D_IN_PROJ), BF16),
        scratch_shapes=[pltpu.VMEM((PROJ_TM, D_MODEL), BF16)],
        compiler_params=pltpu.CompilerParams(
            dimension_semantics=("arbitrary", "arbitrary"), vmem_limit_bytes=VMEM_LIMIT),
        name="inproj",
    )(x2, g, w_bf)


def _sb_kernel(q_ref, k_ref, v_ref, tri_ref, o_ref, acc_ref, carry_ref):
    qi = pl.program_id(1)
    q = q_ref[...] * jnp.asarray(SCALE, BF16)
    lane = lax.broadcasted_iota(jnp.int32, q.shape, 1)
    zero = jnp.zeros_like(q)
    qs = jnp.concatenate([jnp.where(lane < HEAD_DIM, q, zero),
                          jnp.where(lane >= HEAD_DIM, q, zero)], axis=0)
    tri = tri_ref[...]

    row = lax.broadcasted_iota(jnp.int32, (2 * SB_T, SB_T), 0)
    col = lax.broadcasted_iota(jnp.int32, (2 * SB_T, SB_T), 1)
    before = col < jnp.where(row >= SB_T, row - SB_T, row)

    def scores(kb, diagonal=False):
        start = pl.multiple_of(kb * SB_T, SB_T)
        kblk = k_ref[pl.ds(start, SB_T), :]
        z = lax.dot_general(qs, kblk, (((1,), (1,)), ((), ())), preferred_element_type=F32)
        sp = jnp.maximum(z, 0.0) + jnp.log(1.0 + jnp.exp(-jnp.abs(z)))
        log_beta = z - sp
        if diagonal:
            sp = jnp.where(before, sp, 0.0)
        return sp, log_beta

    def weights(sp, log_beta, carry, diagonal=False):
        hi = sp.astype(BF16)
        lo = (sp - hi.astype(F32)).astype(BF16)
        cs = (jnp.dot(hi, tri, preferred_element_type=F32)
              + jnp.dot(lo, tri, preferred_element_type=F32))
        e = log_beta + cs
        if carry is not None:
            e = e + carry
        w = jnp.exp(e)
        if diagonal:
            w = jnp.where(before, w, 0.0)
        return w.astype(BF16)

    def values(kb, w, v_valid=None):
        start = pl.multiple_of(kb * SB_T, SB_T)
        vblk = v_ref[pl.ds(start, SB_T), :]
        if v_valid is not None:
            vblk = jnp.where(v_valid, vblk, jnp.zeros_like(vblk))
        return jnp.dot(w, vblk, preferred_element_type=F32)

    kb1 = jnp.maximum(qi - 1, 0)
    sp0, lb0 = scores(qi, diagonal=True)
    sp1, lb1 = scores(kb1)
    carry0 = -jnp.sum(sp0, axis=1, keepdims=True)
    w0 = weights(sp0, lb0, None, diagonal=True)
    w1 = weights(sp1, lb1, carry0)
    acc_ref[...] = values(qi, w0) + values(kb1, w1, v_valid=qi > 0)
    carry_ref[...] = carry0 - jnp.sum(sp1, axis=1, keepdims=True)

    def block(kb):
        sp, log_beta = scores(kb)
        carry = carry_ref[...]
        acc_ref[...] += values(kb, weights(sp, log_beta, carry))
        carry_ref[...] = carry - jnp.sum(sp, axis=1, keepdims=True)

    def cond(state):
        kb, go = state
        return jnp.logical_and(kb >= 0, go > 0)

    def body(state):
        kb, _ = state
        block(kb)
        go = (jnp.max(carry_ref[...]) >= SB_STOP).astype(jnp.int32)
        return kb - 1, go

    go0 = (jnp.max(carry_ref[...]) >= SB_STOP).astype(jnp.int32)
    lax.while_loop(cond, body, (qi - 2, go0))

    lane_o = lax.broadcasted_iota(jnp.int32, (SB_T, LANES), 1)
    o_ref[...] = jnp.where(lane_o < HEAD_DIM, acc_ref[:SB_T, :], acc_ref[SB_T:, :])


def _sb_tri():
    j = np.arange(SB_T)[:, None]
    s = np.arange(SB_T)[None, :]
    return jnp.asarray(np.where(j > s, -1.0, 0.0).astype(np.float32), BF16)


def _sb_attention(proj):
    pairs = SB_WIDTH // LANES
    return pl.pallas_call(
        _sb_kernel,
        grid=(pairs, SEQ // SB_T),
        in_specs=[
            pl.BlockSpec((SB_T, LANES), lambda p, i: (i, OFF_Q_SB // LANES + p)),
            pl.BlockSpec((SEQ, LANES), lambda p, i: (0, OFF_K_SB // LANES + p)),
            pl.BlockSpec((SEQ, LANES), lambda p, i: (0, OFF_V_SB // LANES + p)),
            pl.BlockSpec((SB_T, SB_T), lambda p, i: (0, 0)),
        ],
        out_specs=pl.BlockSpec((SB_T, LANES), lambda p, i: (i, p)),
        out_shape=jax.ShapeDtypeStruct((SEQ, SB_WIDTH), F32),
        scratch_shapes=[pltpu.VMEM((2 * SB_T, LANES), F32), pltpu.VMEM((2 * SB_T, 1), F32)],
        compiler_params=pltpu.CompilerParams(
            dimension_semantics=("arbitrary", "arbitrary"), vmem_limit_bytes=VMEM_LIMIT),
        name="sb_attention",
    )(proj, proj, proj, _sb_tri())


def _bucket_table():
    qi = np.arange(BLOCK)[:, None]
    ci = np.arange(2 * BLOCK)[None, :]
    dist = qi + BLOCK - ci
    n = np.maximum(dist, 0)
    max_exact = N_BUCKETS // 2
    nf = np.maximum(n, 1).astype(np.float64)
    val = np.log(nf / max_exact) / math.log(MAX_DISTANCE / max_exact) * (N_BUCKETS - max_exact)
    in_window = (dist >= 0) & (dist < WINDOW)
    frac = val - np.floor(val)
    chk = in_window & (n > max_exact)
    assert np.all((frac[chk] > 1e-3) & (frac[chk] < 1 - 1e-3))
    large = np.minimum(max_exact + val.astype(np.int64), N_BUCKETS - 1)
    bucket = np.where(n < max_exact, n, large)
    later = np.where(in_window, bucket, -1)
    first = np.where(ci >= BLOCK, later, -1)
    return jnp.asarray(np.stack([first, later]).astype(np.int32))


def _bias_kernel(rb_ref, bucket_ref, o_ref):
    h = pl.program_id(1)
    b = bucket_ref[0]
    acc = jnp.full(b.shape, NEG_BIG, F32)
    for k in range(N_BUCKETS):
        acc = jnp.where(b == k, rb_ref[k, h], acc)
    o_ref[0, 0] = acc


def _bias_table(rel_bias):
    return pl.pallas_call(
        _bias_kernel,
        grid=(2, SW_HEADS),
        in_specs=[
            pl.BlockSpec(memory_space=pltpu.SMEM),
            pl.BlockSpec((1, BLOCK, 2 * BLOCK), lambda v, h: (v, 0, 0)),
        ],
        out_specs=pl.BlockSpec((1, 1, BLOCK, 2 * BLOCK), lambda v, h: (v, h, 0, 0)),
        out_shape=jax.ShapeDtypeStruct((2, SW_HEADS, BLOCK, 2 * BLOCK), F32),
        name="t5_bias",
    )(rel_bias.astype(F32), _bucket_table())


def _sw_kernel(sink_ref, q_ref, kp_ref, kc_ref, vp_ref, vc_ref, bias_ref, o_ref):
    q = q_ref[...] * jnp.asarray(SCALE, BF16)
    k = jnp.concatenate([kp_ref[...], kc_ref[...]], axis=0)
    v = jnp.concatenate([vp_ref[...], vc_ref[...]], axis=0)
    grp = SW_HEADS // SW_KV_HEADS
    outs = []
    for h in range(SW_HEADS):
        kv = h // grp
        qh = q[:, h * HEAD_DIM:(h + 1) * HEAD_DIM]
        kh = k[:, kv * HEAD_DIM:(kv + 1) * HEAD_DIM]
        vh = v[:, kv * HEAD_DIM:(kv + 1) * HEAD_DIM]
        logits = lax.dot_general(qh, kh, (((1,), (1,)), ((), ())),
                                 preferred_element_type=F32) + bias_ref[0, h]
        sink = sink_ref[h]
        m = jnp.maximum(jnp.max(logits, axis=-1, keepdims=True), sink)
        p = jnp.exp(logits - m)
        denom = jnp.sum(p, axis=-1, keepdims=True) + jnp.exp(sink - m)
        o = jnp.dot(p.astype(BF16), vh, preferred_element_type=F32)
        outs.append(o / denom)
    o_ref[...] = jnp.concatenate(outs, axis=1)


def _sw_attention(proj, sinks, bias):
    kq = OFF_Q_SW // SW_WIDTH
    kk = OFF_K_SW // SW_KV_WIDTH
    kv = OFF_V_SW // SW_KV_WIDTH
    prev = lambda n: jnp.maximum(n - 1, 0)
    return pl.pallas_call(
        _sw_kernel,
        grid=(SEQ // BLOCK,),
        in_specs=[
            pl.BlockSpec(memory_space=pltpu.SMEM),
            pl.BlockSpec((BLOCK, SW_WIDTH), lambda n: (n, kq)),
            pl.BlockSpec((BLOCK, SW_KV_WIDTH), lambda n: (prev(n), kk)),
            pl.BlockSpec((BLOCK, SW_KV_WIDTH), lambda n: (n, kk)),
            pl.BlockSpec((BLOCK, SW_KV_WIDTH), lambda n: (prev(n), kv)),
            pl.BlockSpec((BLOCK, SW_KV_WIDTH), lambda n: (n, kv)),
            pl.BlockSpec((1, SW_HEADS, BLOCK, 2 * BLOCK), lambda n: (jnp.minimum(n, 1), 0, 0, 0)),
        ],
        out_specs=pl.BlockSpec((BLOCK, SW_WIDTH), lambda n: (n, 0)),
        out_shape=jax.ShapeDtypeStruct((SEQ, SW_WIDTH), F32),
        compiler_params=pltpu.CompilerParams(
            dimension_semantics=("arbitrary",), vmem_limit_bytes=VMEM_LIMIT),
        name="sw_attention",
    )(sinks.astype(F32), proj, proj, proj, proj, proj, bias)


def _rms(x, g):
    ms = jnp.mean(x * x, axis=-1, keepdims=True)
    return (x * lax.rsqrt(ms + RMS_EPS)) * g


def _silu(z):
    return z * (1.0 / (1.0 + jnp.exp(-z)))


def _out_kernel(osb_ref, osw_ref, zsb_ref, zswa_ref, zswb_ref, gsb_ref, gsw_ref, gpost_ref,
                w_ref, x_ref, o_ref):
    a_sb = _rms(osb_ref[...], gsb_ref[...]) * _silu(zsb_ref[...].astype(F32))
    z_sw = jnp.concatenate([zswa_ref[...], zswb_ref[...]], axis=1).astype(F32)
    a_sw = _rms(osw_ref[...], gsw_ref[...]) * _silu(z_sw)
    y = jnp.dot(a_sb.astype(BF16), w_ref[:SB_WIDTH, :], preferred_element_type=F32)
    y = y + jnp.dot(a_sw.astype(BF16), w_ref[SB_WIDTH:, :], preferred_element_type=F32)
    o_ref[...] = x_ref[...] + _rms(y, gpost_ref[...])


def _outproj(o_sb, o_sw, proj, gn_sb, gn_sw, norm_post, w_out_bf, x2):
    half = SW_WIDTH // 2
    return pl.pallas_call(
        _out_kernel,
        grid=(SEQ // OUT_TM,),
        in_specs=[
            pl.BlockSpec((OUT_TM, SB_WIDTH), lambda i: (i, 0)),
            pl.BlockSpec((OUT_TM, SW_WIDTH), lambda i: (i, 0)),
            pl.BlockSpec((OUT_TM, SB_WIDTH), lambda i: (i, OFF_Z_SB // SB_WIDTH)),
            pl.BlockSpec((OUT_TM, half), lambda i: (i, OFF_Z_SW // half)),
            pl.BlockSpec((OUT_TM, half), lambda i: (i, OFF_Z_SW // half + 1)),
            pl.BlockSpec((1, SB_WIDTH), lambda i: (0, 0)),
            pl.BlockSpec((1, SW_WIDTH), lambda i: (0, 0)),
            pl.BlockSpec((1, D_MODEL), lambda i: (0, 0)),
            pl.BlockSpec((D_MIX, D_MODEL), lambda i: (0, 0)),
            pl.BlockSpec((OUT_TM, D_MODEL), lambda i: (i, 0)),
        ],
        out_specs=pl.BlockSpec((OUT_TM, D_MODEL), lambda i: (i, 0)),
        out_shape=jax.ShapeDtypeStruct((SEQ, D_MODEL), F32),
        compiler_params=pltpu.CompilerParams(
            dimension_semantics=("arbitrary",), vmem_limit_bytes=VMEM_LIMIT),
        name="outproj",
    )(o_sb, o_sw, proj, proj, proj, gn_sb, gn_sw, norm_post, w_out_bf, x2)


def kernel(x, w_in, w_out, norm_pre, norm_post, gn_sb, gn_sw, sinks, rel_bias):
    assert x.shape == (1, SEQ, D_MODEL) and w_in.shape == (1, D_MODEL, D_IN_PROJ)
    assert OFF_Z_SW % (SW_WIDTH // 2) == 0 and OFF_Q_SW % SW_WIDTH == 0
    x2 = x.reshape(SEQ, D_MODEL)
    proj = _inproj(x2, norm_pre.astype(F32), w_in[0].astype(BF16))
    o_sb = _sb_attention(proj)
    bias = _bias_table(rel_bias)
    o_sw = _sw_attention(proj, sinks[0], bias)
    out = _outproj(o_sb, o_sw, proj, gn_sb.astype(F32), gn_sw.astype(F32), norm_post.astype(F32),
                   w_out[0].astype(BF16), x2)
    return out.reshape(1, SEQ, D_MODEL)
```

```python
import functools
import math

import numpy as np
import jax
import jax.numpy as jnp
from jax import lax
from jax.experimental import pallas as pl
from jax.experimental.pallas import tpu as pltpu

D_MODEL = 2048
SEQ = 16384
HEAD_DIM = 64
SB_HEADS = 16
SW_HEADS = 16
SW_KV_HEADS = 4
SB_WIDTH = SB_HEADS * HEAD_DIM
SW_WIDTH = SW_HEADS * HEAD_DIM
SW_KV_WIDTH = SW_KV_HEADS * HEAD_DIM
D_MIX = SB_WIDTH + SW_WIDTH
D_IN_PROJ = 4 * SB_WIDTH + 2 * SW_WIDTH + 2 * SW_KV_WIDTH
WINDOW = 128
BLOCK = 128
N_BUCKETS = 32
MAX_DISTANCE = 128
RMS_EPS = 1e-6
SCALE = HEAD_DIM ** -0.5

OFF_Q_SB = 0
OFF_K_SB = SB_WIDTH
OFF_V_SB = 2 * SB_WIDTH
OFF_Z_SB = 3 * SB_WIDTH
OFF_Q_SW = 4 * SB_WIDTH
OFF_K_SW = OFF_Q_SW + SW_WIDTH
OFF_V_SW = OFF_K_SW + SW_KV_WIDTH
OFF_Z_SW = OFF_V_SW + SW_KV_WIDTH

LANES = 128
VMEM_LIMIT = 56 * 1024 * 1024

PROJ_TM = 512
PROJ_TN = 3328
SB_T = 256
OUT_TM = 512
OUT_CHUNKS = 4
NEG_BIG = -1e30
SB_STOP = -110.0

F32 = jnp.float32
BF16 = jnp.bfloat16


def _inproj_kernel(x_ref, g_ref, w_ref, o_ref, xn_ref):
    @pl.when(pl.program_id(1) == 0)
    def _():
        x = x_ref[...]
        ms = jnp.mean(x * x, axis=-1, keepdims=True)
        xn_ref[...] = ((x * lax.rsqrt(ms + RMS_EPS)) * g_ref[...]).astype(BF16)

    o_ref[...] = jnp.dot(xn_ref[...], w_ref[...], preferred_element_type=F32).astype(BF16)


def _inproj(x2, g, w_bf):
    return pl.pallas_call(
        _inproj_kernel,
        grid=(SEQ // PROJ_TM, D_IN_PROJ // PROJ_TN),
        in_specs=[
            pl.BlockSpec((PROJ_TM, D_MODEL), lambda i, j: (i, 0)),
            pl.BlockSpec((1, D_MODEL), lambda i, j: (0, 0)),
            pl.BlockSpec((D_MODEL, PROJ_TN), lambda i, j: (0, j)),
        ],
        out_specs=pl.BlockSpec((PROJ_TM, PROJ_TN), lambda i, j: (i, j)),
        out_shape=jax.ShapeDtypeStruct((SEQ, D_IN_PROJ), BF16),
        scratch_shapes=[pltpu.VMEM((PROJ_TM, D_MODEL), BF16)],
        compiler_params=pltpu.CompilerParams(
            dimension_semantics=("arbitrary", "arbitrary"), vmem_limit_bytes=VMEM_LIMIT),
        name="inproj",
    )(x2, g, w_bf)


def _sb_kernel(q_ref, k_ref, v_ref, tri_ref, o_ref, acc_ref, carry_ref):
    qi = pl.program_id(1)
    q = q_ref[...] * jnp.asarray(SCALE, BF16)
    lane = lax.broadcasted_iota(jnp.int32, q.shape, 1)
    zero = jnp.zeros_like(q)
    qs = jnp.concatenate([jnp.where(lane < HEAD_DIM, q, zero),
                          jnp.where(lane >= HEAD_DIM, q, zero)], axis=0)
    tri = tri_ref[...]

    row = lax.broadcasted_iota(jnp.int32, (2 * SB_T, SB_T), 0)
    col = lax.broadcasted_iota(jnp.int32, (2 * SB_T, SB_T), 1)
    before = col < jnp.where(row >= SB_T, row - SB_T, row)

    def scores(kb, diagonal=False):
        start = pl.multiple_of(kb * SB_T, SB_T)
        kblk = k_ref[pl.ds(start, SB_T), :]
        z = lax.dot_general(qs, kblk, (((1,), (1,)), ((), ())), preferred_element_type=F32)
        sp = jnp.maximum(z, 0.0) + jnp.log(1.0 + jnp.exp(-jnp.abs(z)))
        log_beta = z - sp
        if diagonal:
            sp = jnp.where(before, sp, 0.0)
        return sp, log_beta

    def weights(sp, log_beta, carry, diagonal=False):
        hi = sp.astype(BF16)
        lo = (sp - hi.astype(F32)).astype(BF16)
        cs = (jnp.dot(hi, tri, preferred_element_type=F32)
              + jnp.dot(lo, tri, preferred_element_type=F32))
        e = log_beta + cs
        if carry is not None:
            e = e + carry
        w = jnp.exp(e)
        if diagonal:
            w = jnp.where(before, w, 0.0)
        return w.astype(BF16)

    def values(kb, w, v_valid=None):
        start = pl.multiple_of(kb * SB_T, SB_T)
        vblk = v_ref[pl.ds(start, SB_T), :]
        if v_valid is not None:
            vblk = jnp.where(v_valid, vblk, jnp.zeros_like(vblk))
        return jnp.dot(w, vblk, preferred_element_type=F32)

    kb1 = jnp.maximum(qi - 1, 0)
    sp0, lb0 = scores(qi, diagonal=True)
    sp1, lb1 = scores(kb1)
    carry0 = -jnp.sum(sp0, axis=1, keepdims=True)
    w0 = weights(sp0, lb0, None, diagonal=True)
    w1 = weights(sp1, lb1, carry0)
    acc_ref[...] = values(qi, w0) + values(kb1, w1, v_valid=qi > 0)
    carry_ref[...] = carry0 - jnp.sum(sp1, axis=1, keepdims=True)

    def block(kb):
        sp, log_beta = scores(kb)
        carry = carry_ref[...]
        acc_ref[...] += values(kb, weights(sp, log_beta, carry))
        carry_ref[...] = carry - jnp.sum(sp, axis=1, keepdims=True)

    def cond(state):
        kb, go = state
        return jnp.logical_and(kb >= 0, go > 0)

    def body(state):
        kb, _ = state
        block(kb)
        go = (jnp.max(carry_ref[...]) >= SB_STOP).astype(jnp.int32)
        return kb - 1, go

    go0 = (jnp.max(carry_ref[...]) >= SB_STOP).astype(jnp.int32)
    lax.while_loop(cond, body, (qi - 2, go0))

    lane_o = lax.broadcasted_iota(jnp.int32, (SB_T, LANES), 1)
    o_ref[...] = jnp.where(lane_o < HEAD_DIM, acc_ref[:SB_T, :], acc_ref[SB_T:, :])


def _sb_tri():
    j = np.arange(SB_T)[:, None]
    s = np.arange(SB_T)[None, :]
    return jnp.asarray(np.where(j > s, -1.0, 0.0).astype(np.float32), BF16)


def _sb_attention(proj):
    pairs = SB_WIDTH // LANES
    return pl.pallas_call(
        _sb_kernel,
        grid=(pairs, SEQ // SB_T),
        in_specs=[
            pl.BlockSpec((SB_T, LANES), lambda p, i: (i, OFF_Q_SB // LANES + p)),
            pl.BlockSpec((SEQ, LANES), lambda p, i: (0, OFF_K_SB // LANES + p)),
            pl.BlockSpec((SEQ, LANES), lambda p, i: (0, OFF_V_SB // LANES + p)),
            pl.BlockSpec((SB_T, SB_T), lambda p, i: (0, 0)),
        ],
        out_specs=pl.BlockSpec((SB_T, LANES), lambda p, i: (i, p)),
        out_shape=jax.ShapeDtypeStruct((SEQ, SB_WIDTH), F32),
        scratch_shapes=[pltpu.VMEM((2 * SB_T, LANES), F32), pltpu.VMEM((2 * SB_T, 1), F32)],
        compiler_params=pltpu.CompilerParams(
            dimension_semantics=("arbitrary", "arbitrary"), vmem_limit_bytes=VMEM_LIMIT),
        name="sb_attention",
    )(proj, proj, proj, _sb_tri())


def _bucket_table():
    qi = np.arange(BLOCK)[:, None]
    ci = np.arange(2 * BLOCK)[None, :]
    dist = qi + BLOCK - ci
    n = np.maximum(dist, 0)
    max_exact = N_BUCKETS // 2
    nf = np.maximum(n, 1).astype(np.float64)
    val = np.log(nf / max_exact) / math.log(MAX_DISTANCE / max_exact) * (N_BUCKETS - max_exact)
    in_window = (dist >= 0) & (dist < WINDOW)
    frac = val - np.floor(val)
    chk = in_window & (n > max_exact)
    assert np.all((frac[chk] > 1e-3) & (frac[chk] < 1 - 1e-3))
    large = np.minimum(max_exact + val.astype(np.int64), N_BUCKETS - 1)
    bucket = np.where(n < max_exact, n, large)
    later = np.where(in_window, bucket, -1)
    first = np.where(ci >= BLOCK, later, -1)
    return jnp.asarray(np.stack([first, later]).astype(np.int32))


def _bias_kernel(rb_ref, bucket_ref, o_ref):
    h = pl.program_id(1)
    b = bucket_ref[0]
    acc = jnp.full(b.shape, NEG_BIG, F32)
    for k in range(N_BUCKETS):
        acc = jnp.where(b == k, rb_ref[k, h], acc)
    o_ref[0, 0] = acc


def _bias_table(rel_bias):
    return pl.pallas_call(
        _bias_kernel,
        grid=(2, SW_HEADS),
        in_specs=[
            pl.BlockSpec(memory_space=pltpu.SMEM),
            pl.BlockSpec((1, BLOCK, 2 * BLOCK), lambda v, h: (v, 0, 0)),
        ],
        out_specs=pl.BlockSpec((1, 1, BLOCK, 2 * BLOCK), lambda v, h: (v, h, 0, 0)),
        out_shape=jax.ShapeDtypeStruct((2, SW_HEADS, BLOCK, 2 * BLOCK), F32),
        name="t5_bias",
    )(rel_bias.astype(F32), _bucket_table())


def _sw_kernel(sink_ref, q_ref, kp_ref, kc_ref, vp_ref, vc_ref, bias_ref, o_ref):
    k = jnp.concatenate([kp_ref[...], kc_ref[...]], axis=0)
    v = jnp.concatenate([vp_ref[...], vc_ref[...]], axis=0)
    lane = lax.broadcasted_iota(jnp.int32, (BLOCK, LANES), 1)
    ones = jnp.ones((2 * BLOCK, LANES), BF16)
    scale = jnp.asarray(SCALE, BF16)
    pairs_per_kv = SW_HEADS // SW_KV_HEADS // 2
    dims = (((1,), (1,)), ((), ()))

    def swap_halves(t):
        return jnp.concatenate([t[:, HEAD_DIM:], t[:, :HEAD_DIM]], axis=1)

    logits, heads, sides = [], [], []
    for kv in range(SW_KV_HEADS):
        side = kv % 2
        on_side = (lane >= HEAD_DIM) if side else (lane < HEAD_DIM)
        k_pair = k[:, (kv // 2) * LANES:(kv // 2 + 1) * LANES]
        same, other, hs, ho = [], [], [], []
        for pp in range(pairs_per_kv):
            p = kv * pairs_per_kv + pp
            q_pair = q_ref[:, p * LANES:(p + 1) * LANES] * scale
            zero = jnp.zeros_like(q_pair)
            same.append(jnp.where(on_side, q_pair, zero))
            other.append(jnp.where(on_side, swap_halves(q_pair), zero))
            hs.append(2 * p + side)
            ho.append(2 * p + 1 - side)
        lhs = jnp.concatenate(same + other, axis=0)
        logits.append(lax.dot_general(lhs, k_pair, dims, preferred_element_type=F32))
        heads.append(hs + ho)
        sides.append(on_side)

    for kv in range(SW_KV_HEADS):
        v_pair = v[:, (kv // 2) * LANES:(kv // 2 + 1) * LANES]
        probs, sink_terms = [], []
        for i, h in enumerate(heads[kv]):
            lg = logits[kv][i * BLOCK:(i + 1) * BLOCK, :] + bias_ref[0, h]
            sink = sink_ref[h]
            m = jnp.maximum(jnp.max(lg, axis=-1, keepdims=True), sink)
            probs.append(jnp.exp(lg - m).astype(BF16))
            sink_terms.append(jnp.exp(sink - m))
        r_same = jnp.dot(jnp.concatenate(probs[:pairs_per_kv], axis=0),
                         jnp.concatenate([v_pair, ones], axis=1), preferred_element_type=F32)
        r_other = jnp.dot(jnp.concatenate(probs[pairs_per_kv:], axis=0),
                          jnp.concatenate([swap_halves(v_pair), ones], axis=1),
                          preferred_element_type=F32)
        for pp in range(pairs_per_kv):
            p = kv * pairs_per_kv + pp
            rows = slice(pp * BLOCK, (pp + 1) * BLOCK)
            o_same = r_same[rows, :LANES] / (r_same[rows, LANES:] + sink_terms[pp])
            o_other = r_other[rows, :LANES] / (r_other[rows, LANES:] + sink_terms[pairs_per_kv + pp])
            o_ref[:, p * LANES:(p + 1) * LANES] = jnp.where(sides[kv], o_same, o_other)


def _sw_attention(proj, sinks, bias):
    kq = OFF_Q_SW // SW_WIDTH
    kk = OFF_K_SW // SW_KV_WIDTH
    kv = OFF_V_SW // SW_KV_WIDTH
    prev = lambda n: jnp.maximum(n - 1, 0)
    return pl.pallas_call(
        _sw_kernel,
        grid=(SEQ // BLOCK,),
        in_specs=[
            pl.BlockSpec(memory_space=pltpu.SMEM),
            pl.BlockSpec((BLOCK, SW_WIDTH), lambda n: (n, kq)),
            pl.BlockSpec((BLOCK, SW_KV_WIDTH), lambda n: (prev(n), kk)),
            pl.BlockSpec((BLOCK, SW_KV_WIDTH), lambda n: (n, kk)),
            pl.BlockSpec((BLOCK, SW_KV_WIDTH), lambda n: (prev(n), kv)),
            pl.BlockSpec((BLOCK, SW_KV_WIDTH), lambda n: (n, kv)),
            pl.BlockSpec((1, SW_HEADS, BLOCK, 2 * BLOCK), lambda n: (jnp.minimum(n, 1), 0, 0, 0)),
        ],
        out_specs=pl.BlockSpec((BLOCK, SW_WIDTH), lambda n: (n, 0)),
        out_shape=jax.ShapeDtypeStruct((SEQ, SW_WIDTH), F32),
        compiler_params=pltpu.CompilerParams(
            dimension_semantics=("arbitrary",), vmem_limit_bytes=VMEM_LIMIT),
        name="sw_attention",
    )(sinks.astype(F32), proj, proj, proj, proj, proj, bias)


def _rms(x, g):
    ms = jnp.mean(x * x, axis=-1, keepdims=True)
    return (x * lax.rsqrt(ms + RMS_EPS)) * g


def _silu(z):
    return z * (1.0 / (1.0 + jnp.exp(-z)))


def _out_kernel(osb_ref, osw_ref, zsb_ref, zswa_ref, zswb_ref, gsb_ref, gsw_ref, gpost_ref,
                w_ref, x_ref, o_ref):
    chunk = OUT_TM // OUT_CHUNKS
    ys = []
    for c in range(OUT_CHUNKS):
        rows = pl.ds(c * chunk, chunk)
        a_sb = _rms(osb_ref[rows, :], gsb_ref[...]) * _silu(zsb_ref[rows, :].astype(F32))
        z_sw = jnp.concatenate([zswa_ref[rows, :], zswb_ref[rows, :]], axis=1).astype(F32)
        a_sw = _rms(osw_ref[rows, :], gsw_ref[...]) * _silu(z_sw)
        y = jnp.dot(a_sb.astype(BF16), w_ref[:SB_WIDTH, :], preferred_element_type=F32)
        ys.append(y + jnp.dot(a_sw.astype(BF16), w_ref[SB_WIDTH:, :],
                              preferred_element_type=F32))
    for c in range(OUT_CHUNKS):
        rows = pl.ds(c * chunk, chunk)
        o_ref[rows, :] = x_ref[rows, :] + _rms(ys[c], gpost_ref[...])


def _outproj(o_sb, o_sw, proj, gn_sb, gn_sw, norm_post, w_out_bf, x2):
    half = SW_WIDTH // 2
    return pl.pallas_call(
        _out_kernel,
        grid=(SEQ // OUT_TM,),
        in_specs=[
            pl.BlockSpec((OUT_TM, SB_WIDTH), lambda i: (i, 0)),
            pl.BlockSpec((OUT_TM, SW_WIDTH), lambda i: (i, 0)),
            pl.BlockSpec((OUT_TM, SB_WIDTH), lambda i: (i, OFF_Z_SB // SB_WIDTH)),
            pl.BlockSpec((OUT_TM, half), lambda i: (i, OFF_Z_SW // half)),
            pl.BlockSpec((OUT_TM, half), lambda i: (i, OFF_Z_SW // half + 1)),
            pl.BlockSpec((1, SB_WIDTH), lambda i: (0, 0)),
            pl.BlockSpec((1, SW_WIDTH), lambda i: (0, 0)),
            pl.BlockSpec((1, D_MODEL), lambda i: (0, 0)),
            pl.BlockSpec((D_MIX, D_MODEL), lambda i: (0, 0)),
            pl.BlockSpec((OUT_TM, D_MODEL), lambda i: (i, 0)),
        ],
        out_specs=pl.BlockSpec((OUT_TM, D_MODEL), lambda i: (i, 0)),
        out_shape=jax.ShapeDtypeStruct((SEQ, D_MODEL), F32),
        compiler_params=pltpu.CompilerParams(
            dimension_semantics=("arbitrary",), vmem_limit_bytes=VMEM_LIMIT),
        name="outproj",
    )(o_sb, o_sw, proj, proj, proj, gn_sb, gn_sw, norm_post, w_out_bf, x2)


def kernel(x, w_in, w_out, norm_pre, norm_post, gn_sb, gn_sw, sinks, rel_bias):
    assert x.shape == (1, SEQ, D_MODEL) and w_in.shape == (1, D_MODEL, D_IN_PROJ)
    assert OFF_Z_SW % (SW_WIDTH // 2) == 0 and OFF_Q_SW % SW_WIDTH == 0
    x2 = x.reshape(SEQ, D_MODEL)
    proj = _inproj(x2, norm_pre.astype(F32), w_in[0].astype(BF16))
    o_sb = _sb_attention(proj)
    bias = _bias_table(rel_bias)
    o_sw = _sw_attention(proj, sinks[0], bias)
    out = _outproj(o_sb, o_sw, proj, gn_sb.astype(F32), gn_sw.astype(F32), norm_post.astype(F32),
                   w_out[0].astype(BF16), x2)
    return out.reshape(1, SEQ, D_MODEL)
```

```python
import math

import numpy as np
import jax
import jax.numpy as jnp
from jax import lax
from jax.experimental import pallas as pl
from jax.experimental.pallas import tpu as pltpu

D_MODEL = 2048
SEQ = 16384
HEAD_DIM = 64
SB_HEADS = 16
SW_HEADS = 16
SW_KV_HEADS = 4
SB_WIDTH = SB_HEADS * HEAD_DIM
SW_WIDTH = SW_HEADS * HEAD_DIM
SW_KV_WIDTH = SW_KV_HEADS * HEAD_DIM
D_MIX = SB_WIDTH + SW_WIDTH
D_IN_PROJ = 4 * SB_WIDTH + 2 * SW_WIDTH + 2 * SW_KV_WIDTH
WINDOW = 128
BLOCK = 128
N_BUCKETS = 32
MAX_DISTANCE = 128
RMS_EPS = 1e-6
SCALE = HEAD_DIM ** -0.5

OFF_Q_SB = 0
OFF_K_SB = SB_WIDTH
OFF_V_SB = 2 * SB_WIDTH
OFF_Z_SB = 3 * SB_WIDTH
OFF_Q_SW = 4 * SB_WIDTH
OFF_K_SW = OFF_Q_SW + SW_WIDTH
OFF_V_SW = OFF_K_SW + SW_KV_WIDTH
OFF_Z_SW = OFF_V_SW + SW_KV_WIDTH

LANES = 128
VMEM_LIMIT = 56 * 1024 * 1024

PROJ_TM = 512
PROJ_TN = 3328
SB_T = 256
SB_H = SB_T // 2
OUT_TM = 512
OUT_CHUNKS = 4
NEG_BIG = -1e30
SB_STOP = -110.0

F32 = jnp.float32
BF16 = jnp.bfloat16


def _inproj_kernel(x_ref, g_ref, w_ref, o_ref, xn_ref):
    @pl.when(pl.program_id(1) == 0)
    def _():
        x = x_ref[...]
        ms = jnp.mean(x * x, axis=-1, keepdims=True)
        xn_ref[...] = ((x * lax.rsqrt(ms + RMS_EPS)) * g_ref[...]).astype(BF16)

    o_ref[...] = jnp.dot(xn_ref[...], w_ref[...], preferred_element_type=F32).astype(BF16)


def _inproj(x2, g, w_bf):
    return pl.pallas_call(
        _inproj_kernel,
        grid=(SEQ // PROJ_TM, D_IN_PROJ // PROJ_TN),
        in_specs=[
            pl.BlockSpec((PROJ_TM, D_MODEL), lambda i, j: (i, 0)),
            pl.BlockSpec((1, D_MODEL), lambda i, j: (0, 0)),
            pl.BlockSpec((D_MODEL, PROJ_TN), lambda i, j: (0, j)),
        ],
        out_specs=pl.BlockSpec((PROJ_TM, PROJ_TN), lambda i, j: (i, j)),
        out_shape=jax.ShapeDtypeStruct((SEQ, D_IN_PROJ), BF16),
        scratch_shapes=[pltpu.VMEM((PROJ_TM, D_MODEL), BF16)],
        compiler_params=pltpu.CompilerParams(
            dimension_semantics=("arbitrary", "arbitrary"), vmem_limit_bytes=VMEM_LIMIT),
        name="inproj",
    )(x2, g, w_bf)


def _sb_kernel(qn_ref, qt_ref, k_ref, v_ref, tri_ref, o_ref,
               hi_ref, lbl_ref, lbr_ref, lbp_ref, sp0_ref, w_ref, carry_w_ref, go_ref,
               acc_ref, carry_ref):
    s = pl.program_id(1)
    last = SEQ // SB_T - 1
    n = jnp.minimum(s, last)
    t = jnp.maximum(s - 2, 0)

    @pl.when(s == 0)
    def _():
        for ref in (hi_ref, lbl_ref, lbr_ref, lbp_ref, sp0_ref, w_ref, carry_w_ref):
            ref[...] = jnp.zeros_like(ref)

    lane = lax.broadcasted_iota(jnp.int32, (SB_T, LANES), 1)

    def stack_heads(q_ref):
        q = q_ref[...] * jnp.asarray(SCALE, BF16)
        zero = jnp.zeros_like(q)
        return jnp.concatenate([jnp.where(lane < HEAD_DIM, q, zero),
                                jnp.where(lane >= HEAD_DIM, q, zero)], axis=0)

    tri = tri_ref[...]
    r_h = lax.broadcasted_iota(jnp.int32, (SB_H, SB_H), 0)
    c_h = lax.broadcasted_iota(jnp.int32, (SB_H, SB_H), 1)
    before_h = c_h < r_h
    zeros_h = jnp.zeros((SB_H, SB_H), BF16)

    def qk(qs, kb):
        start = pl.multiple_of(kb * SB_T, SB_T)
        kblk = k_ref[pl.ds(start, SB_T), :]
        return lax.dot_general(qs, kblk, (((1,), (1,)), ((), ())), preferred_element_type=F32)

    def softplus_parts(z):
        neg_abs = lax.bitcast_convert_type(
            lax.bitcast_convert_type(z, jnp.int32) | jnp.int32(-2 ** 31), F32)
        sp = jnp.maximum(z, 0.0) + jnp.log(1.0 + jnp.exp(neg_abs))
        return sp, z - sp

    def cut(z):
        return jnp.where(before_h, z, NEG_BIG)

    def quarters(a):
        left = a[:, :SB_H]
        right = jnp.concatenate([a[SB_H:SB_T, SB_H:], a[SB_T + SB_H:, SB_H:]], axis=0)
        return left, right

    def unquarter(left, right):
        right_full = jnp.concatenate([zeros_h, right[:SB_H], zeros_h, right[SB_H:]], axis=0)
        return jnp.concatenate([left, right_full], axis=1)

    def cumsum(hi):
        return jnp.dot(hi, tri, preferred_element_type=F32)

    def values(kb, w, v_valid=None):
        start = pl.multiple_of(kb * SB_T, SB_T)
        vblk = v_ref[pl.ds(start, SB_T), :]
        if v_valid is not None:
            vblk = jnp.where(v_valid, vblk, jnp.zeros_like(vblk))
        return jnp.dot(w, vblk, preferred_element_type=F32)

    def stages(cur):
        old = 1 - cur
        qs_n = stack_heads(qn_ref)
        z_d = qk(qs_n, n)
        z_p = qk(qs_n, jnp.maximum(n - 1, 0))
        cs_d = cumsum(hi_ref[old, 0])
        cs_p = cumsum(hi_ref[old, 1])
        acc_ref[...] = (values(t, w_ref[old, 0])
                        + values(jnp.maximum(t - 1, 0), w_ref[old, 1], v_valid=t > 0))
        carry_t = carry_w_ref[old]
        carry_ref[...] = carry_t
        go_ref[0] = (jnp.max(carry_t) >= SB_STOP).astype(jnp.int32)

        zl, zr = quarters(z_d)
        zl = jnp.concatenate([cut(zl[:SB_H]), zl[SB_H:SB_T],
                              cut(zl[SB_T:SB_T + SB_H]), zl[SB_T + SB_H:]], axis=0)
        zr = jnp.concatenate([cut(zr[:SB_H]), cut(zr[SB_H:])], axis=0)
        sp_l, lb_l = softplus_parts(zl)
        sp_r, lb_r = softplus_parts(zr)
        sp_p, lb_p = softplus_parts(z_p)
        hi_ref[cur, 0] = unquarter(sp_l.astype(BF16), sp_r.astype(BF16))
        hi_ref[cur, 1] = sp_p.astype(BF16)
        lbl_ref[cur] = lb_l
        lbr_ref[cur] = lb_r
        lbp_ref[cur] = lb_p
        sp0_ref[cur, 0] = sp_l[:, 0:1]
        sp0_ref[cur, 1] = sp_p[:, 0:1]

        cs_l, cs_r = quarters(cs_d)
        w_l = jnp.exp(lbl_ref[old] + cs_l).astype(BF16)
        w_r = jnp.exp(lbr_ref[old] + cs_r).astype(BF16)
        carry_d = cs_d[:, 0:1] - sp0_ref[old, 0]
        w_p = jnp.exp(lbp_ref[old] + cs_p + carry_d).astype(BF16)
        w_ref[cur, 0] = unquarter(w_l, w_r)
        w_ref[cur, 1] = w_p
        carry_w_ref[cur] = carry_d + (cs_p[:, 0:1] - sp0_ref[old, 1])

    @pl.when(s % 2 == 0)
    def _():
        stages(0)

    @pl.when(s % 2 == 1)
    def _():
        stages(1)

    def block(qs, kb):
        sp, log_beta = softplus_parts(qk(qs, kb))
        cs = cumsum(sp.astype(BF16))
        carry = carry_ref[...]
        w = jnp.exp(log_beta + cs + carry).astype(BF16)
        acc_ref[...] += values(kb, w)
        carry_ref[...] = carry + (cs[:, 0:1] - sp[:, 0:1])

    def cond(state):
        kb, go = state
        return jnp.logical_and(kb >= 0, go > 0)

    def body(state):
        kb, _ = state
        block(stack_heads(qt_ref), kb)
        go = (jnp.max(carry_ref[...]) >= SB_STOP).astype(jnp.int32)
        return kb - 1, go

    lax.while_loop(cond, body, (t - 2, go_ref[0]))

    o_ref[...] = jnp.where(lane < HEAD_DIM, acc_ref[:SB_T, :], acc_ref[SB_T:, :])


def _sb_tri():
    j = np.arange(SB_T)[:, None]
    s = np.arange(SB_T)[None, :]
    return jnp.asarray(np.where(j > s, -1.0, 0.0).astype(np.float32), BF16)


def _sb_attention(proj):
    pairs = SB_WIDTH // LANES
    tiles = SEQ // SB_T
    return pl.pallas_call(
        _sb_kernel,
        grid=(pairs, tiles + 2),
        in_specs=[
            pl.BlockSpec((SB_T, LANES),
                         lambda p, s: (jnp.minimum(s, tiles - 1), OFF_Q_SB // LANES + p)),
            pl.BlockSpec((SB_T, LANES),
                         lambda p, s: (jnp.maximum(s - 2, 0), OFF_Q_SB // LANES + p)),
            pl.BlockSpec((SEQ, LANES), lambda p, s: (0, OFF_K_SB // LANES + p)),
            pl.BlockSpec((SEQ, LANES), lambda p, s: (0, OFF_V_SB // LANES + p)),
            pl.BlockSpec((SB_T, SB_T), lambda p, s: (0, 0)),
        ],
        out_specs=pl.BlockSpec((SB_T, LANES), lambda p, s: (jnp.maximum(s - 2, 0), p)),
        out_shape=jax.ShapeDtypeStruct((SEQ, SB_WIDTH), F32),
        scratch_shapes=[
            pltpu.VMEM((2, 2, 2 * SB_T, SB_T), BF16),
            pltpu.VMEM((2, 2 * SB_T, SB_H), F32),
            pltpu.VMEM((2, SB_T, SB_H), F32),
            pltpu.VMEM((2, 2 * SB_T, SB_T), F32),
            pltpu.VMEM((2, 2, 2 * SB_T, 1), F32),
            pltpu.VMEM((2, 2, 2 * SB_T, SB_T), BF16),
            pltpu.VMEM((2, 2 * SB_T, 1), F32),
            pltpu.SMEM((1,), jnp.int32),
            pltpu.VMEM((2 * SB_T, LANES), F32),
            pltpu.VMEM((2 * SB_T, 1), F32),
        ],
        compiler_params=pltpu.CompilerParams(
            dimension_semantics=("arbitrary", "arbitrary"), vmem_limit_bytes=VMEM_LIMIT),
        name="sb_attention",
    )(proj, proj, proj, proj, _sb_tri())


def _bucket_table():
    qi = np.arange(BLOCK)[:, None]
    ci = np.arange(2 * BLOCK)[None, :]
    dist = qi + BLOCK - ci
    n = np.maximum(dist, 0)
    max_exact = N_BUCKETS // 2
    nf = np.maximum(n, 1).astype(np.float64)
    val = np.log(nf / max_exact) / math.log(MAX_DISTANCE / max_exact) * (N_BUCKETS - max_exact)
    in_window = (dist >= 0) & (dist < WINDOW)
    frac = val - np.floor(val)
    chk = in_window & (n > max_exact)
    assert np.all((frac[chk] > 1e-3) & (frac[chk] < 1 - 1e-3))
    large = np.minimum(max_exact + val.astype(np.int64), N_BUCKETS - 1)
    bucket = np.where(n < max_exact, n, large)
    later = np.where(in_window, bucket, -1)
    first = np.where(ci >= BLOCK, later, -1)
    return jnp.asarray(np.stack([first, later]).astype(np.int32))


def _bias_kernel(rb_ref, bucket_ref, o_ref):
    h = pl.program_id(1)
    b = bucket_ref[0]
    acc = jnp.full(b.shape, NEG_BIG, F32)
    for k in range(N_BUCKETS):
        acc = jnp.where(b == k, rb_ref[k, h], acc)
    o_ref[0, 0] = acc


def _bias_table(rel_bias):
    return pl.pallas_call(
        _bias_kernel,
        grid=(2, SW_HEADS),
        in_specs=[
            pl.BlockSpec(memory_space=pltpu.SMEM),
            pl.BlockSpec((1, BLOCK, 2 * BLOCK), lambda v, h: (v, 0, 0)),
        ],
        out_specs=pl.BlockSpec((1, 1, BLOCK, 2 * BLOCK), lambda v, h: (v, h, 0, 0)),
        out_shape=jax.ShapeDtypeStruct((2, SW_HEADS, BLOCK, 2 * BLOCK), F32),
        name="t5_bias",
    )(rel_bias.astype(F32), _bucket_table())


def _sw_kernel(sink_ref, q_ref, kp_ref, kc_ref, vp_ref, vc_ref, bias_ref, o_ref):
    k = jnp.concatenate([kp_ref[...], kc_ref[...]], axis=0)
    v = jnp.concatenate([vp_ref[...], vc_ref[...]], axis=0)
    lane = lax.broadcasted_iota(jnp.int32, (BLOCK, LANES), 1)
    ones = jnp.ones((2 * BLOCK, LANES), BF16)
    scale = jnp.asarray(SCALE, BF16)
    pairs_per_kv = SW_HEADS // SW_KV_HEADS // 2
    dims = (((1,), (1,)), ((), ()))

    def swap_halves(t):
        return jnp.concatenate([t[:, HEAD_DIM:], t[:, :HEAD_DIM]], axis=1)

    logits, heads, sides = [], [], []
    for kv in range(SW_KV_HEADS):
        side = kv % 2
        on_side = (lane >= HEAD_DIM) if side else (lane < HEAD_DIM)
        k_pair = k[:, (kv // 2) * LANES:(kv // 2 + 1) * LANES]
        same, other, hs, ho = [], [], [], []
        for pp in range(pairs_per_kv):
            p = kv * pairs_per_kv + pp
            q_pair = q_ref[:, p * LANES:(p + 1) * LANES] * scale
            zero = jnp.zeros_like(q_pair)
            same.append(jnp.where(on_side, q_pair, zero))
            other.append(jnp.where(on_side, swap_halves(q_pair), zero))
            hs.append(2 * p + side)
            ho.append(2 * p + 1 - side)
        lhs = jnp.concatenate(same + other, axis=0)
        logits.append(lax.dot_general(lhs, k_pair, dims, preferred_element_type=F32))
        heads.append(hs + ho)
        sides.append(on_side)

    for kv in range(SW_KV_HEADS):
        v_pair = v[:, (kv // 2) * LANES:(kv // 2 + 1) * LANES]
        probs, sink_terms = [], []
        for i, h in enumerate(heads[kv]):
            lg = logits[kv][i * BLOCK:(i + 1) * BLOCK, :] + bias_ref[0, h]
            sink = sink_ref[h]
            m = jnp.maximum(jnp.max(lg, axis=-1, keepdims=True), sink)
            probs.append(jnp.exp(lg - m).astype(BF16))
            sink_terms.append(jnp.exp(sink - m))
        r_same = jnp.dot(jnp.concatenate(probs[:pairs_per_kv], axis=0),
                         jnp.concatenate([v_pair, ones], axis=1), preferred_element_type=F32)
        r_other = jnp.dot(jnp.concatenate(probs[pairs_per_kv:], axis=0),
                          jnp.concatenate([swap_halves(v_pair), ones], axis=1),
                          preferred_element_type=F32)
        for pp in range(pairs_per_kv):
            p = kv * pairs_per_kv + pp
            rows = slice(pp * BLOCK, (pp + 1) * BLOCK)
            o_same = r_same[rows, :LANES] / (r_same[rows, LANES:] + sink_terms[pp])
            o_other = r_other[rows, :LANES] / (r_other[rows, LANES:] + sink_terms[pairs_per_kv + pp])
            o_ref[:, p * LANES:(p + 1) * LANES] = jnp.where(sides[kv], o_same, o_other)


def _sw_attention(proj, sinks, bias):
    kq = OFF_Q_SW // SW_WIDTH
    kk = OFF_K_SW // SW_KV_WIDTH
    kv = OFF_V_SW // SW_KV_WIDTH
    prev = lambda n: jnp.maximum(n - 1, 0)
    return pl.pallas_call(
        _sw_kernel,
        grid=(SEQ // BLOCK,),
        in_specs=[
            pl.BlockSpec(memory_space=pltpu.SMEM),
            pl.BlockSpec((BLOCK, SW_WIDTH), lambda n: (n, kq)),
            pl.BlockSpec((BLOCK, SW_KV_WIDTH), lambda n: (prev(n), kk)),
            pl.BlockSpec((BLOCK, SW_KV_WIDTH), lambda n: (n, kk)),
            pl.BlockSpec((BLOCK, SW_KV_WIDTH), lambda n: (prev(n), kv)),
            pl.BlockSpec((BLOCK, SW_KV_WIDTH), lambda n: (n, kv)),
            pl.BlockSpec((1, SW_HEADS, BLOCK, 2 * BLOCK), lambda n: (jnp.minimum(n, 1), 0, 0, 0)),
        ],
        out_specs=pl.BlockSpec((BLOCK, SW_WIDTH), lambda n: (n, 0)),
        out_shape=jax.ShapeDtypeStruct((SEQ, SW_WIDTH), F32),
        compiler_params=pltpu.CompilerParams(
            dimension_semantics=("arbitrary",), vmem_limit_bytes=VMEM_LIMIT),
        name="sw_attention",
    )(sinks.astype(F32), proj, proj, proj, proj, proj, bias)


def _rms(x, g):
    ms = jnp.mean(x * x, axis=-1, keepdims=True)
    return (x * lax.rsqrt(ms + RMS_EPS)) * g


def _silu(z):
    return z * (1.0 / (1.0 + jnp.exp(-z)))


def _out_kernel(osb_ref, osw_ref, zsb_ref, zswa_ref, zswb_ref, gsb_ref, gsw_ref, gpost_ref,
                w_ref, x_ref, o_ref):
    chunk = OUT_TM // OUT_CHUNKS
    ys = []
    for c in range(OUT_CHUNKS):
        rows = pl.ds(c * chunk, chunk)
        a_sb = _rms(osb_ref[rows, :], gsb_ref[...]) * _silu(zsb_ref[rows, :].astype(F32))
        z_sw = jnp.concatenate([zswa_ref[rows, :], zswb_ref[rows, :]], axis=1).astype(F32)
        a_sw = _rms(osw_ref[rows, :], gsw_ref[...]) * _silu(z_sw)
        y = jnp.dot(a_sb.astype(BF16), w_ref[:SB_WIDTH, :], preferred_element_type=F32)
        ys.append(y + jnp.dot(a_sw.astype(BF16), w_ref[SB_WIDTH:, :],
                              preferred_element_type=F32))
    for c in range(OUT_CHUNKS):
        rows = pl.ds(c * chunk, chunk)
        o_ref[rows, :] = x_ref[rows, :] + _rms(ys[c], gpost_ref[...])


def _outproj(o_sb, o_sw, proj, gn_sb, gn_sw, norm_post, w_out_bf, x2):
    half = SW_WIDTH // 2
    return pl.pallas_call(
        _out_kernel,
        grid=(SEQ // OUT_TM,),
        in_specs=[
            pl.BlockSpec((OUT_TM, SB_WIDTH), lambda i: (i, 0)),
            pl.BlockSpec((OUT_TM, SW_WIDTH), lambda i: (i, 0)),
            pl.BlockSpec((OUT_TM, SB_WIDTH), lambda i: (i, OFF_Z_SB // SB_WIDTH)),
            pl.BlockSpec((OUT_TM, half), lambda i: (i, OFF_Z_SW // half)),
            pl.BlockSpec((OUT_TM, half), lambda i: (i, OFF_Z_SW // half + 1)),
            pl.BlockSpec((1, SB_WIDTH), lambda i: (0, 0)),
            pl.BlockSpec((1, SW_WIDTH), lambda i: (0, 0)),
            pl.BlockSpec((1, D_MODEL), lambda i: (0, 0)),
            pl.BlockSpec((D_MIX, D_MODEL), lambda i: (0, 0)),
            pl.BlockSpec((OUT_TM, D_MODEL), lambda i: (i, 0)),
        ],
        out_specs=pl.BlockSpec((OUT_TM, D_MODEL), lambda i: (i, 0)),
        out_shape=jax.ShapeDtypeStruct((SEQ, D_MODEL), F32),
        compiler_params=pltpu.CompilerParams(
            dimension_semantics=("arbitrary",), vmem_limit_bytes=VMEM_LIMIT),
        name="outproj",
    )(o_sb, o_sw, proj, proj, proj, gn_sb, gn_sw, norm_post, w_out_bf, x2)


def kernel(x, w_in, w_out, norm_pre, norm_post, gn_sb, gn_sw, sinks, rel_bias):
    assert x.shape == (1, SEQ, D_MODEL) and w_in.shape == (1, D_MODEL, D_IN_PROJ)
    assert OFF_Z_SW % (SW_WIDTH // 2) == 0 and OFF_Q_SW % SW_WIDTH == 0
    x2 = x.reshape(SEQ, D_MODEL)
    proj = _inproj(x2, norm_pre.astype(F32), w_in[0].astype(BF16))
    o_sb = _sb_attention(proj)
    bias = _bias_table(rel_bias)
    o_sw = _sw_attention(proj, sinks[0], bias)
    out = _outproj(o_sb, o_sw, proj, gn_sb.astype(F32), gn_sw.astype(F32), norm_post.astype(F32),
                   w_out[0].astype(BF16), x2)
    return out.reshape(1, SEQ, D_MODEL)
```

```python
import math

import numpy as np
import jax
import jax.numpy as jnp
from jax import lax
from jax.experimental import pallas as pl
from jax.experimental.pallas import tpu as pltpu

D_MODEL = 2048
SEQ = 16384
HEAD_DIM = 64
SB_HEADS = 16
SW_HEADS = 16
SW_KV_HEADS = 4
SB_WIDTH = SB_HEADS * HEAD_DIM
SW_WIDTH = SW_HEADS * HEAD_DIM
SW_KV_WIDTH = SW_KV_HEADS * HEAD_DIM
D_MIX = SB_WIDTH + SW_WIDTH
D_IN_PROJ = 4 * SB_WIDTH + 2 * SW_WIDTH + 2 * SW_KV_WIDTH
WINDOW = 128
BLOCK = 128
N_BUCKETS = 32
MAX_DISTANCE = 128
RMS_EPS = 1e-6
SCALE = HEAD_DIM ** -0.5

OFF_Q_SB = 0
OFF_K_SB = SB_WIDTH
OFF_V_SB = 2 * SB_WIDTH
OFF_Z_SB = 3 * SB_WIDTH
OFF_Q_SW = 4 * SB_WIDTH
OFF_K_SW = OFF_Q_SW + SW_WIDTH
OFF_V_SW = OFF_K_SW + SW_KV_WIDTH
OFF_Z_SW = OFF_V_SW + SW_KV_WIDTH

LANES = 128
VMEM_LIMIT = 56 * 1024 * 1024

PROJ_TM = 512
PROJ_TN = 3328
SB_T = 256
SB_H = SB_T // 2
SB_UNITS = 1
OUT_TM = 512
OUT_CHUNKS = 4
NEG_BIG = -1e30
SB_STOP = -110.0

F32 = jnp.float32
BF16 = jnp.bfloat16


def _inproj_kernel(x_ref, g_ref, w_ref, o_ref, xn_ref):
    @pl.when(pl.program_id(1) == 0)
    def _():
        x = x_ref[...]
        ms = jnp.mean(x * x, axis=-1, keepdims=True)
        xn_ref[...] = ((x * lax.rsqrt(ms + RMS_EPS)) * g_ref[...]).astype(BF16)

    o_ref[...] = jnp.dot(xn_ref[...], w_ref[...], preferred_element_type=F32).astype(BF16)


def _inproj(x2, g, w_bf):
    return pl.pallas_call(
        _inproj_kernel,
        grid=(SEQ // PROJ_TM, D_IN_PROJ // PROJ_TN),
        in_specs=[
            pl.BlockSpec((PROJ_TM, D_MODEL), lambda i, j: (i, 0)),
            pl.BlockSpec((1, D_MODEL), lambda i, j: (0, 0)),
            pl.BlockSpec((D_MODEL, PROJ_TN), lambda i, j: (0, j)),
        ],
        out_specs=pl.BlockSpec((PROJ_TM, PROJ_TN), lambda i, j: (i, j)),
        out_shape=jax.ShapeDtypeStruct((SEQ, D_IN_PROJ), BF16),
        scratch_shapes=[pltpu.VMEM((PROJ_TM, D_MODEL), BF16)],
        compiler_params=pltpu.CompilerParams(
            dimension_semantics=("arbitrary", "arbitrary"), vmem_limit_bytes=VMEM_LIMIT),
        name="inproj",
    )(x2, g, w_bf)


def _sb_kernel(q_ref, k_ref, v_ref, tri_ref, o_ref,
               hi_ref, zl_ref, zr_ref, zp_ref, w_ref, carry_w_ref, go_ref,
               acc_ref, carry_ref):
    s = pl.program_id(1)
    last = SEQ // SB_T - 1
    n = jnp.minimum(s, last)
    t = jnp.maximum(s - 2, 0)

    @pl.when(s == 0)
    def _():
        for ref in (hi_ref, zl_ref, zr_ref, zp_ref, w_ref, carry_w_ref):
            ref[...] = jnp.zeros_like(ref)

    lane = lax.broadcasted_iota(jnp.int32, (SB_T, LANES), 1)

    def cols(u):
        return slice(u * LANES, (u + 1) * LANES)

    def stack_heads(tile, u):
        start = pl.multiple_of(tile * SB_T, SB_T)
        q = q_ref[pl.ds(start, SB_T), cols(u)] * jnp.asarray(SCALE, BF16)
        zero = jnp.zeros_like(q)
        return jnp.concatenate([jnp.where(lane < HEAD_DIM, q, zero),
                                jnp.where(lane >= HEAD_DIM, q, zero)], axis=0)

    tri = tri_ref[...]
    r_h = lax.broadcasted_iota(jnp.int32, (SB_H, SB_H), 0)
    c_h = lax.broadcasted_iota(jnp.int32, (SB_H, SB_H), 1)
    before_h = c_h < r_h
    zeros_h = jnp.zeros((SB_H, SB_H), BF16)

    def qk(qs, kb, u):
        start = pl.multiple_of(kb * SB_T, SB_T)
        kblk = k_ref[pl.ds(start, SB_T), cols(u)]
        return lax.dot_general(qs, kblk, (((1,), (1,)), ((), ())), preferred_element_type=F32)

    def softplus(z):
        neg_abs = lax.bitcast_convert_type(
            lax.bitcast_convert_type(z, jnp.int32) | jnp.int32(-2 ** 31), F32)
        return (jnp.maximum(z, 0.0) + jnp.log(1.0 + jnp.exp(neg_abs))).astype(BF16)

    def cut(z):
        return jnp.where(before_h, z, NEG_BIG)

    def quarters(a):
        left = a[:, :SB_H]
        right = jnp.concatenate([a[SB_H:SB_T, SB_H:], a[SB_T + SB_H:, SB_H:]], axis=0)
        return left, right

    def unquarter(left, right):
        right_full = jnp.concatenate([zeros_h, right[:SB_H], zeros_h, right[SB_H:]], axis=0)
        return jnp.concatenate([left, right_full], axis=1)

    def cumsum(hi):
        return jnp.dot(hi, tri, preferred_element_type=F32)

    def values(kb, w, u, v_valid=None):
        start = pl.multiple_of(kb * SB_T, SB_T)
        vblk = v_ref[pl.ds(start, SB_T), cols(u)]
        if v_valid is not None:
            vblk = jnp.where(v_valid, vblk, jnp.zeros_like(vblk))
        return jnp.dot(w, vblk, preferred_element_type=F32)

    def stages(cur):
        old = 1 - cur
        units = range(SB_UNITS)
        qs_n = [stack_heads(n, u) for u in units]
        z_d = [qk(qs_n[u], n, u) for u in units]
        z_p = [qk(qs_n[u], jnp.maximum(n - 1, 0), u) for u in units]
        cs_d = [cumsum(hi_ref[u, old, 0]) for u in units]
        cs_p = [cumsum(hi_ref[u, old, 1]) for u in units]
        for u in units:
            acc_ref[u] = (values(t, w_ref[u, old, 0], u)
                          + values(jnp.maximum(t - 1, 0), w_ref[u, old, 1], u, v_valid=t > 0))
            carry_t = carry_w_ref[u, old]
            carry_ref[u] = carry_t
            go_ref[u] = (jnp.max(carry_t) >= SB_STOP).astype(jnp.int32)

        for u in units:
            zl, zr = quarters(z_d[u])
            zl = jnp.concatenate([cut(zl[:SB_H]), zl[SB_H:SB_T],
                                  cut(zl[SB_T:SB_T + SB_H]), zl[SB_T + SB_H:]], axis=0)
            zr = jnp.concatenate([cut(zr[:SB_H]), cut(zr[SB_H:])], axis=0)
            hi_ref[u, cur, 0] = unquarter(softplus(zl), softplus(zr))
            hi_ref[u, cur, 1] = softplus(z_p[u])
            zl_ref[u, cur] = zl
            zr_ref[u, cur] = zr
            zp_ref[u, cur] = z_p[u]

            cs_l, cs_r = quarters(cs_d[u])
            w_l = jnp.exp(zl_ref[u, old] + cs_l).astype(BF16)
            w_r = jnp.exp(zr_ref[u, old] + cs_r).astype(BF16)
            carry_d = cs_d[u][:, 0:1]
            w_p = jnp.exp(zp_ref[u, old] + cs_p[u] + carry_d).astype(BF16)
            w_ref[u, cur, 0] = unquarter(w_l, w_r)
            w_ref[u, cur, 1] = w_p
            carry_w_ref[u, cur] = carry_d + cs_p[u][:, 0:1]

    @pl.when(s % 2 == 0)
    def _():
        stages(0)

    @pl.when(s % 2 == 1)
    def _():
        stages(1)

    def block(qs, kb, u):
        z = qk(qs, kb, u)
        cs = cumsum(softplus(z))
        carry = carry_ref[u]
        w = jnp.exp(z + cs + carry).astype(BF16)
        acc_ref[u] += values(kb, w, u)
        carry_ref[u] = carry + cs[:, 0:1]

    def cond(state):
        kb, go = state
        return jnp.logical_and(kb >= 0, go > 0)

    for u in range(SB_UNITS):
        def body(state, u=u):
            kb, _ = state
            block(stack_heads(t, u), kb, u)
            go = (jnp.max(carry_ref[u]) >= SB_STOP).astype(jnp.int32)
            return kb - 1, go

        lax.while_loop(cond, body, (t - 2, go_ref[u]))
        o_ref[:, cols(u)] = jnp.where(lane < HEAD_DIM, acc_ref[u, :SB_T, :], acc_ref[u, SB_T:, :])


def _sb_tri():
    j = np.arange(SB_T)[:, None]
    s = np.arange(SB_T)[None, :]
    return jnp.asarray(np.where(j >= s, -1.0, 0.0).astype(np.float32), BF16)


def _sb_attention(proj):
    wide = SB_UNITS * LANES
    tiles = SEQ // SB_T
    U = SB_UNITS
    return pl.pallas_call(
        _sb_kernel,
        grid=(SB_WIDTH // wide, tiles + 2),
        in_specs=[
            pl.BlockSpec((SEQ, wide), lambda p, s: (0, OFF_Q_SB // wide + p)),
            pl.BlockSpec((SEQ, wide), lambda p, s: (0, OFF_K_SB // wide + p)),
            pl.BlockSpec((SEQ, wide), lambda p, s: (0, OFF_V_SB // wide + p)),
            pl.BlockSpec((SB_T, SB_T), lambda p, s: (0, 0)),
        ],
        out_specs=pl.BlockSpec((SB_T, wide), lambda p, s: (jnp.maximum(s - 2, 0), p)),
        out_shape=jax.ShapeDtypeStruct((SEQ, SB_WIDTH), F32),
        scratch_shapes=[
            pltpu.VMEM((U, 2, 2, 2 * SB_T, SB_T), BF16),
            pltpu.VMEM((U, 2, 2 * SB_T, SB_H), F32),
            pltpu.VMEM((U, 2, SB_T, SB_H), F32),
            pltpu.VMEM((U, 2, 2 * SB_T, SB_T), F32),
            pltpu.VMEM((U, 2, 2, 2 * SB_T, SB_T), BF16),
            pltpu.VMEM((U, 2, 2 * SB_T, 1), F32),
            pltpu.SMEM((U,), jnp.int32),
            pltpu.VMEM((U, 2 * SB_T, LANES), F32),
            pltpu.VMEM((U, 2 * SB_T, 1), F32),
        ],
        compiler_params=pltpu.CompilerParams(
            dimension_semantics=("arbitrary", "arbitrary"), vmem_limit_bytes=VMEM_LIMIT),
        name="sb_attention",
    )(proj, proj, proj, _sb_tri())


def _bucket_table():
    qi = np.arange(BLOCK)[:, None]
    ci = np.arange(2 * BLOCK)[None, :]
    dist = qi + BLOCK - ci
    n = np.maximum(dist, 0)
    max_exact = N_BUCKETS // 2
    nf = np.maximum(n, 1).astype(np.float64)
    val = np.log(nf / max_exact) / math.log(MAX_DISTANCE / max_exact) * (N_BUCKETS - max_exact)
    in_window = (dist >= 0) & (dist < WINDOW)
    frac = val - np.floor(val)
    chk = in_window & (n > max_exact)
    assert np.all((frac[chk] > 1e-3) & (frac[chk] < 1 - 1e-3))
    large = np.minimum(max_exact + val.astype(np.int64), N_BUCKETS - 1)
    bucket = np.where(n < max_exact, n, large)
    later = np.where(in_window, bucket, -1)
    first = np.where(ci >= BLOCK, later, -1)
    return jnp.asarray(np.stack([first, later]).astype(np.int32))


def _bias_kernel(rb_ref, bucket_ref, o_ref):
    b = bucket_ref[0]
    hits = [b == k for k in range(N_BUCKETS)]
    for h in range(SW_HEADS):
        acc = jnp.full(b.shape, NEG_BIG, F32)
        for k in range(N_BUCKETS):
            acc = jnp.where(hits[k], rb_ref[k, h], acc)
        o_ref[0, h] = acc


def _bias_table(rel_bias):
    return pl.pallas_call(
        _bias_kernel,
        grid=(2,),
        in_specs=[
            pl.BlockSpec(memory_space=pltpu.SMEM),
            pl.BlockSpec((1, BLOCK, 2 * BLOCK), lambda v: (v, 0, 0)),
        ],
        out_specs=pl.BlockSpec((1, SW_HEADS, BLOCK, 2 * BLOCK), lambda v: (v, 0, 0, 0)),
        out_shape=jax.ShapeDtypeStruct((2, SW_HEADS, BLOCK, 2 * BLOCK), F32),
        name="t5_bias",
    )(rel_bias.astype(F32), _bucket_table())


def _sw_kernel(sink_ref, q_ref, kp_ref, kc_ref, vp_ref, vc_ref, bias_ref, o_ref):
    k = jnp.concatenate([kp_ref[...], kc_ref[...]], axis=0)
    v = jnp.concatenate([vp_ref[...], vc_ref[...]], axis=0)
    lane = lax.broadcasted_iota(jnp.int32, (BLOCK, LANES), 1)
    ones = jnp.ones((2 * BLOCK, LANES), BF16)
    scale = jnp.asarray(SCALE, BF16)
    pairs_per_kv = SW_HEADS // SW_KV_HEADS // 2
    dims = (((1,), (1,)), ((), ()))

    def swap_halves(t):
        return jnp.concatenate([t[:, HEAD_DIM:], t[:, :HEAD_DIM]], axis=1)

    logits, heads, sides = [], [], []
    for kv in range(SW_KV_HEADS):
        side = kv % 2
        on_side = (lane >= HEAD_DIM) if side else (lane < HEAD_DIM)
        k_pair = k[:, (kv // 2) * LANES:(kv // 2 + 1) * LANES]
        same, other, hs, ho = [], [], [], []
        for pp in range(pairs_per_kv):
            p = kv * pairs_per_kv + pp
            q_pair = q_ref[:, p * LANES:(p + 1) * LANES] * scale
            zero = jnp.zeros_like(q_pair)
            same.append(jnp.where(on_side, q_pair, zero))
            other.append(jnp.where(on_side, swap_halves(q_pair), zero))
            hs.append(2 * p + side)
            ho.append(2 * p + 1 - side)
        lhs = jnp.concatenate(same + other, axis=0)
        logits.append(lax.dot_general(lhs, k_pair, dims, preferred_element_type=F32))
        heads.append(hs + ho)
        sides.append(on_side)

    for kv in range(SW_KV_HEADS):
        v_pair = v[:, (kv // 2) * LANES:(kv // 2 + 1) * LANES]
        probs, sink_terms = [], []
        for i, h in enumerate(heads[kv]):
            lg = logits[kv][i * BLOCK:(i + 1) * BLOCK, :] + bias_ref[0, h]
            sink = sink_ref[h]
            m = jnp.maximum(jnp.max(lg, axis=-1, keepdims=True), sink)
            probs.append(jnp.exp(lg - m).astype(BF16))
            sink_terms.append(jnp.exp(sink - m))
        r_same = jnp.dot(jnp.concatenate(probs[:pairs_per_kv], axis=0),
                         jnp.concatenate([v_pair, ones], axis=1), preferred_element_type=F32)
        r_other = jnp.dot(jnp.concatenate(probs[pairs_per_kv:], axis=0),
                          jnp.concatenate([swap_halves(v_pair), ones], axis=1),
                          preferred_element_type=F32)
        for pp in range(pairs_per_kv):
            p = kv * pairs_per_kv + pp
            rows = slice(pp * BLOCK, (pp + 1) * BLOCK)
            o_same = r_same[rows, :LANES] / (r_same[rows, LANES:] + sink_terms[pp])
            o_other = r_other[rows, :LANES] / (r_other[rows, LANES:] + sink_terms[pairs_per_kv + pp])
            o_ref[:, p * LANES:(p + 1) * LANES] = jnp.where(sides[kv], o_same, o_other)


def _sw_attention(proj, sinks, bias):
    kq = OFF_Q_SW // SW_WIDTH
    kk = OFF_K_SW // SW_KV_WIDTH
    kv = OFF_V_SW // SW_KV_WIDTH
    prev = lambda n: jnp.maximum(n - 1, 0)
    return pl.pallas_call(
        _sw_kernel,
        grid=(SEQ // BLOCK,),
        in_specs=[
            pl.BlockSpec(memory_space=pltpu.SMEM),
            pl.BlockSpec((BLOCK, SW_WIDTH), lambda n: (n, kq)),
            pl.BlockSpec((BLOCK, SW_KV_WIDTH), lambda n: (prev(n), kk)),
            pl.BlockSpec((BLOCK, SW_KV_WIDTH), lambda n: (n, kk)),
            pl.BlockSpec((BLOCK, SW_KV_WIDTH), lambda n: (prev(n), kv)),
            pl.BlockSpec((BLOCK, SW_KV_WIDTH), lambda n: (n, kv)),
            pl.BlockSpec((1, SW_HEADS, BLOCK, 2 * BLOCK), lambda n: (jnp.minimum(n, 1), 0, 0, 0)),
        ],
        out_specs=pl.BlockSpec((BLOCK, SW_WIDTH), lambda n: (n, 0)),
        out_shape=jax.ShapeDtypeStruct((SEQ, SW_WIDTH), F32),
        compiler_params=pltpu.CompilerParams(
            dimension_semantics=("arbitrary",), vmem_limit_bytes=VMEM_LIMIT),
        name="sw_attention",
    )(sinks.astype(F32), proj, proj, proj, proj, proj, bias)


def _rms(x, g):
    ms = jnp.mean(x * x, axis=-1, keepdims=True)
    return (x * lax.rsqrt(ms + RMS_EPS)) * g


def _silu(z):
    return z * (1.0 / (1.0 + jnp.exp(-z)))


def _out_kernel(osb_ref, osw_ref, zsb_ref, zswa_ref, zswb_ref, gsb_ref, gsw_ref, gpost_ref,
                w_ref, x_ref, o_ref):
    chunk = OUT_TM // OUT_CHUNKS
    ys = []
    for c in range(OUT_CHUNKS):
        rows = pl.ds(c * chunk, chunk)
        a_sb = _rms(osb_ref[rows, :], gsb_ref[...]) * _silu(zsb_ref[rows, :].astype(F32))
        z_sw = jnp.concatenate([zswa_ref[rows, :], zswb_ref[rows, :]], axis=1).astype(F32)
        a_sw = _rms(osw_ref[rows, :], gsw_ref[...]) * _silu(z_sw)
        y = jnp.dot(a_sb.astype(BF16), w_ref[:SB_WIDTH, :], preferred_element_type=F32)
        ys.append(y + jnp.dot(a_sw.astype(BF16), w_ref[SB_WIDTH:, :],
                              preferred_element_type=F32))
    for c in range(OUT_CHUNKS):
        rows = pl.ds(c * chunk, chunk)
        o_ref[rows, :] = x_ref[rows, :] + _rms(ys[c], gpost_ref[...])


def _outproj(o_sb, o_sw, proj, gn_sb, gn_sw, norm_post, w_out_bf, x2):
    half = SW_WIDTH // 2
    return pl.pallas_call(
        _out_kernel,
        grid=(SEQ // OUT_TM,),
        in_specs=[
            pl.BlockSpec((OUT_TM, SB_WIDTH), lambda i: (i, 0)),
            pl.BlockSpec((OUT_TM, SW_WIDTH), lambda i: (i, 0)),
            pl.BlockSpec((OUT_TM, SB_WIDTH), lambda i: (i, OFF_Z_SB // SB_WIDTH)),
            pl.BlockSpec((OUT_TM, half), lambda i: (i, OFF_Z_SW // half)),
            pl.BlockSpec((OUT_TM, half), lambda i: (i, OFF_Z_SW // half + 1)),
            pl.BlockSpec((1, SB_WIDTH), lambda i: (0, 0)),
            pl.BlockSpec((1, SW_WIDTH), lambda i: (0, 0)),
            pl.BlockSpec((1, D_MODEL), lambda i: (0, 0)),
            pl.BlockSpec((D_MIX, D_MODEL), lambda i: (0, 0)),
            pl.BlockSpec((OUT_TM, D_MODEL), lambda i: (i, 0)),
        ],
        out_specs=pl.BlockSpec((OUT_TM, D_MODEL), lambda i: (i, 0)),
        out_shape=jax.ShapeDtypeStruct((SEQ, D_MODEL), F32),
        compiler_params=pltpu.CompilerParams(
            dimension_semantics=("arbitrary",), vmem_limit_bytes=VMEM_LIMIT),
        name="outproj",
    )(o_sb, o_sw, proj, proj, proj, gn_sb, gn_sw, norm_post, w_out_bf, x2)


def kernel(x, w_in, w_out, norm_pre, norm_post, gn_sb, gn_sw, sinks, rel_bias):
    assert x.shape == (1, SEQ, D_MODEL) and w_in.shape == (1, D_MODEL, D_IN_PROJ)
    assert OFF_Z_SW % (SW_WIDTH // 2) == 0 and OFF_Q_SW % SW_WIDTH == 0
    x2 = x.reshape(SEQ, D_MODEL)
    proj = _inproj(x2, norm_pre.astype(F32), w_in[0].astype(BF16))
    o_sb = _sb_attention(proj)
    bias = _bias_table(rel_bias)
    o_sw = _sw_attention(proj, sinks[0], bias)
    out = _outproj(o_sb, o_sw, proj, gn_sb.astype(F32), gn_sw.astype(F32), norm_post.astype(F32),
                   w_out[0].astype(BF16), x2)
    return out.reshape(1, SEQ, D_MODEL)
```

```python
import math

import numpy as np
import jax
import jax.numpy as jnp
from jax import lax
from jax.experimental import pallas as pl
from jax.experimental.pallas import tpu as pltpu

D_MODEL = 2048
SEQ = 16384
HEAD_DIM = 64
SB_HEADS = 16
SW_HEADS = 16
SW_KV_HEADS = 4
SB_WIDTH = SB_HEADS * HEAD_DIM
SW_WIDTH = SW_HEADS * HEAD_DIM
SW_KV_WIDTH = SW_KV_HEADS * HEAD_DIM
D_MIX = SB_WIDTH + SW_WIDTH
D_IN_PROJ = 4 * SB_WIDTH + 2 * SW_WIDTH + 2 * SW_KV_WIDTH
WINDOW = 128
BLOCK = 128
N_BUCKETS = 32
MAX_DISTANCE = 128
RMS_EPS = 1e-6
SCALE = HEAD_DIM ** -0.5

OFF_Q_SB = 0
OFF_K_SB = SB_WIDTH
OFF_V_SB = 2 * SB_WIDTH
OFF_Z_SB = 3 * SB_WIDTH
OFF_Q_SW = 4 * SB_WIDTH
OFF_K_SW = OFF_Q_SW + SW_WIDTH
OFF_V_SW = OFF_K_SW + SW_KV_WIDTH
OFF_Z_SW = OFF_V_SW + SW_KV_WIDTH

LANES = 128
VMEM_LIMIT = 56 * 1024 * 1024

PROJ_TM = 512
PROJ_TN = 3328
SB_T = 256
SB_H = SB_T // 2
SB_UNITS = 1
OUT_TM = 512
OUT_CHUNKS = 4
NEG_BIG = -1e30
SB_STOP = -110.0

F32 = jnp.float32
BF16 = jnp.bfloat16


def _inproj_kernel(x_ref, g_ref, w_ref, o_ref):
    x = x_ref[...]
    ms = jnp.mean(x * x, axis=-1, keepdims=True)
    xn = ((x * lax.rsqrt(ms + RMS_EPS)) * g_ref[...]).astype(BF16)
    for j in range(D_IN_PROJ // PROJ_TN):
        cols = slice(j * PROJ_TN, (j + 1) * PROJ_TN)
        o_ref[:, cols] = jnp.dot(xn, w_ref[:, cols], preferred_element_type=F32).astype(BF16)


def _inproj(x2, g, w_bf):
    return pl.pallas_call(
        _inproj_kernel,
        grid=(SEQ // PROJ_TM,),
        in_specs=[
            pl.BlockSpec((PROJ_TM, D_MODEL), lambda i: (i, 0)),
            pl.BlockSpec((1, D_MODEL), lambda i: (0, 0)),
            pl.BlockSpec((D_MODEL, D_IN_PROJ), lambda i: (0, 0), pipeline_mode=pl.Buffered(1)),
        ],
        out_specs=pl.BlockSpec((PROJ_TM, D_IN_PROJ), lambda i: (i, 0)),
        out_shape=jax.ShapeDtypeStruct((SEQ, D_IN_PROJ), BF16),
        compiler_params=pltpu.CompilerParams(
            dimension_semantics=("arbitrary",), vmem_limit_bytes=VMEM_LIMIT),
        name="inproj",
    )(x2, g, w_bf)


def _sb_kernel(q_ref, k_ref, v_ref, tri_ref, o_ref,
               hi_ref, zl_ref, zr_ref, zp_ref, w_ref, carry_w_ref, go_ref,
               acc_ref, carry_ref):
    s = pl.program_id(1)
    last = SEQ // SB_T - 1
    n = jnp.minimum(s, last)
    t = jnp.maximum(s - 2, 0)

    @pl.when(s == 0)
    def _():
        for ref in (hi_ref, zl_ref, zr_ref, zp_ref, w_ref, carry_w_ref):
            ref[...] = jnp.zeros_like(ref)

    lane = lax.broadcasted_iota(jnp.int32, (SB_T, LANES), 1)

    def cols(u):
        return slice(u * LANES, (u + 1) * LANES)

    def stack_heads(tile, u):
        start = pl.multiple_of(tile * SB_T, SB_T)
        q = q_ref[pl.ds(start, SB_T), cols(u)] * jnp.asarray(SCALE, BF16)
        zero = jnp.zeros_like(q)
        return jnp.concatenate([jnp.where(lane < HEAD_DIM, q, zero),
                                jnp.where(lane >= HEAD_DIM, q, zero)], axis=0)

    tri = tri_ref[...]
    r_h = lax.broadcasted_iota(jnp.int32, (SB_H, SB_H), 0)
    c_h = lax.broadcasted_iota(jnp.int32, (SB_H, SB_H), 1)
    before_h = c_h < r_h
    zeros_h = jnp.zeros((SB_H, SB_H), BF16)

    def qk(qs, kb, u):
        start = pl.multiple_of(kb * SB_T, SB_T)
        kblk = k_ref[pl.ds(start, SB_T), cols(u)]
        return lax.dot_general(qs, kblk, (((1,), (1,)), ((), ())), preferred_element_type=F32)

    def softplus(z):
        neg_abs = lax.bitcast_convert_type(
            lax.bitcast_convert_type(z, jnp.int32) | jnp.int32(-2 ** 31), F32)
        return (jnp.maximum(z, 0.0) + jnp.log(1.0 + jnp.exp(neg_abs))).astype(BF16)

    def cut(z):
        return jnp.where(before_h, z, NEG_BIG)

    def quarters(a):
        left = a[:, :SB_H]
        right = jnp.concatenate([a[SB_H:SB_T, SB_H:], a[SB_T + SB_H:, SB_H:]], axis=0)
        return left, right

    def unquarter(left, right):
        right_full = jnp.concatenate([zeros_h, right[:SB_H], zeros_h, right[SB_H:]], axis=0)
        return jnp.concatenate([left, right_full], axis=1)

    def cumsum(hi):
        return jnp.dot(hi, tri, preferred_element_type=F32)

    def values(kb, w, u, v_valid=None):
        start = pl.multiple_of(kb * SB_T, SB_T)
        vblk = v_ref[pl.ds(start, SB_T), cols(u)]
        if v_valid is not None:
            vblk = jnp.where(v_valid, vblk, jnp.zeros_like(vblk))
        return jnp.dot(w, vblk, preferred_element_type=F32)

    def stages(cur):
        old = 1 - cur
        units = range(SB_UNITS)
        qs_n = [stack_heads(n, u) for u in units]
        z_d = [qk(qs_n[u], n, u) for u in units]
        z_p = [qk(qs_n[u], jnp.maximum(n - 1, 0), u) for u in units]
        cs_d = [cumsum(hi_ref[u, old, 0]) for u in units]
        cs_p = [cumsum(hi_ref[u, old, 1]) for u in units]
        for u in units:
            acc_ref[u] = (values(t, w_ref[u, old, 0], u)
                          + values(jnp.maximum(t - 1, 0), w_ref[u, old, 1], u, v_valid=t > 0))
            carry_t = carry_w_ref[u, old]
            carry_ref[u] = carry_t
            go_ref[u] = (jnp.max(carry_t) >= SB_STOP).astype(jnp.int32)

        for u in units:
            zl, zr = quarters(z_d[u])
            zl = jnp.concatenate([cut(zl[:SB_H]), zl[SB_H:SB_T],
                                  cut(zl[SB_T:SB_T + SB_H]), zl[SB_T + SB_H:]], axis=0)
            zr = jnp.concatenate([cut(zr[:SB_H]), cut(zr[SB_H:])], axis=0)
            hi_ref[u, cur, 0] = unquarter(softplus(zl), softplus(zr))
            hi_ref[u, cur, 1] = softplus(z_p[u])
            zl_ref[u, cur] = zl
            zr_ref[u, cur] = zr
            zp_ref[u, cur] = z_p[u]

            cs_l, cs_r = quarters(cs_d[u])
            w_l = jnp.exp(zl_ref[u, old] + cs_l).astype(BF16)
            w_r = jnp.exp(zr_ref[u, old] + cs_r).astype(BF16)
            carry_d = cs_d[u][:, 0:1]
            w_p = jnp.exp(zp_ref[u, old] + cs_p[u] + carry_d).astype(BF16)
            w_ref[u, cur, 0] = unquarter(w_l, w_r)
            w_ref[u, cur, 1] = w_p
            carry_w_ref[u, cur] = carry_d + cs_p[u][:, 0:1]

    @pl.when(s % 2 == 0)
    def _():
        stages(0)

    @pl.when(s % 2 == 1)
    def _():
        stages(1)

    def block(qs, kb, u):
        z = qk(qs, kb, u)
        cs = cumsum(softplus(z))
        carry = carry_ref[u]
        w = jnp.exp(z + cs + carry).astype(BF16)
        acc_ref[u] += values(kb, w, u)
        carry_ref[u] = carry + cs[:, 0:1]

    def cond(state):
        kb, go = state
        return jnp.logical_and(kb >= 0, go > 0)

    for u in range(SB_UNITS):
        def body(state, u=u):
            kb, _ = state
            block(stack_heads(t, u), kb, u)
            go = (jnp.max(carry_ref[u]) >= SB_STOP).astype(jnp.int32)
            return kb - 1, go

        lax.while_loop(cond, body, (t - 2, go_ref[u]))
        o_ref[:, cols(u)] = jnp.where(lane < HEAD_DIM, acc_ref[u, :SB_T, :], acc_ref[u, SB_T:, :])


def _sb_tri():
    j = np.arange(SB_T)[:, None]
    s = np.arange(SB_T)[None, :]
    return jnp.asarray(np.where(j >= s, -1.0, 0.0).astype(np.float32), BF16)


def _sb_attention(proj):
    wide = SB_UNITS * LANES
    tiles = SEQ // SB_T
    U = SB_UNITS
    return pl.pallas_call(
        _sb_kernel,
        grid=(SB_WIDTH // wide, tiles + 2),
        in_specs=[
            pl.BlockSpec((SEQ, wide), lambda p, s: (0, OFF_Q_SB // wide + p)),
            pl.BlockSpec((SEQ, wide), lambda p, s: (0, OFF_K_SB // wide + p)),
            pl.BlockSpec((SEQ, wide), lambda p, s: (0, OFF_V_SB // wide + p)),
            pl.BlockSpec((SB_T, SB_T), lambda p, s: (0, 0)),
        ],
        out_specs=pl.BlockSpec((SB_T, wide), lambda p, s: (jnp.maximum(s - 2, 0), p)),
        out_shape=jax.ShapeDtypeStruct((SEQ, SB_WIDTH), F32),
        scratch_shapes=[
            pltpu.VMEM((U, 2, 2, 2 * SB_T, SB_T), BF16),
            pltpu.VMEM((U, 2, 2 * SB_T, SB_H), F32),
            pltpu.VMEM((U, 2, SB_T, SB_H), F32),
            pltpu.VMEM((U, 2, 2 * SB_T, SB_T), F32),
            pltpu.VMEM((U, 2, 2, 2 * SB_T, SB_T), BF16),
            pltpu.VMEM((U, 2, 2 * SB_T, 1), F32),
            pltpu.SMEM((U,), jnp.int32),
            pltpu.VMEM((U, 2 * SB_T, LANES), F32),
            pltpu.VMEM((U, 2 * SB_T, 1), F32),
        ],
        compiler_params=pltpu.CompilerParams(
            dimension_semantics=("arbitrary", "arbitrary"), vmem_limit_bytes=VMEM_LIMIT),
        name="sb_attention",
    )(proj, proj, proj, _sb_tri())


def _bucket_table():
    qi = np.arange(BLOCK)[:, None]
    ci = np.arange(2 * BLOCK)[None, :]
    dist = qi + BLOCK - ci
    n = np.maximum(dist, 0)
    max_exact = N_BUCKETS // 2
    nf = np.maximum(n, 1).astype(np.float64)
    val = np.log(nf / max_exact) / math.log(MAX_DISTANCE / max_exact) * (N_BUCKETS - max_exact)
    in_window = (dist >= 0) & (dist < WINDOW)
    frac = val - np.floor(val)
    chk = in_window & (n > max_exact)
    assert np.all((frac[chk] > 1e-3) & (frac[chk] < 1 - 1e-3))
    large = np.minimum(max_exact + val.astype(np.int64), N_BUCKETS - 1)
    bucket = np.where(n < max_exact, n, large)
    later = np.where(in_window, bucket, -1)
    first = np.where(ci >= BLOCK, later, -1)
    return jnp.asarray(np.stack([first, later]).astype(np.int32))


def _bias_kernel(rb_ref, bucket_ref, o_ref):
    b = bucket_ref[0]
    hits = [b == k for k in range(N_BUCKETS)]
    for h in range(SW_HEADS):
        acc = jnp.full(b.shape, NEG_BIG, F32)
        for k in range(N_BUCKETS):
            acc = jnp.where(hits[k], rb_ref[k, h], acc)
        o_ref[0, h] = acc


def _bias_table(rel_bias):
    return pl.pallas_call(
        _bias_kernel,
        grid=(2,),
        in_specs=[
            pl.BlockSpec(memory_space=pltpu.SMEM),
            pl.BlockSpec((1, BLOCK, 2 * BLOCK), lambda v: (v, 0, 0)),
        ],
        out_specs=pl.BlockSpec((1, SW_HEADS, BLOCK, 2 * BLOCK), lambda v: (v, 0, 0, 0)),
        out_shape=jax.ShapeDtypeStruct((2, SW_HEADS, BLOCK, 2 * BLOCK), F32),
        name="t5_bias",
    )(rel_bias.astype(F32), _bucket_table())


def _sw_kernel(sink_ref, q_ref, kp_ref, kc_ref, vp_ref, vc_ref, bias_ref, o_ref, p_ref, st_ref):
    s = pl.program_id(0)

    @pl.when(s == 0)
    def _():
        p_ref[...] = jnp.zeros_like(p_ref)
        st_ref[...] = jnp.zeros_like(st_ref)

    lane = lax.broadcasted_iota(jnp.int32, (BLOCK, LANES), 1)
    ones = jnp.ones((2 * BLOCK, LANES), BF16)
    scale = jnp.asarray(SCALE, BF16)
    pairs_per_kv = SW_HEADS // SW_KV_HEADS // 2
    dims = (((1,), (1,)), ((), ()))

    def swap_halves(t):
        return jnp.concatenate([t[:, HEAD_DIM:], t[:, :HEAD_DIM]], axis=1)

    def on_side(kv):
        return (lane >= HEAD_DIM) if kv % 2 else (lane < HEAD_DIM)

    def kv_cols(kv):
        return slice((kv // 2) * LANES, (kv // 2 + 1) * LANES)

    def stages(cur):
        old = 1 - cur
        k = jnp.concatenate([kp_ref[...], kc_ref[...]], axis=0)
        v = jnp.concatenate([vp_ref[...], vc_ref[...]], axis=0)
        logits, heads = [], []
        for kv in range(SW_KV_HEADS):
            side = kv % 2
            same, other, hs, ho = [], [], [], []
            for pp in range(pairs_per_kv):
                p = kv * pairs_per_kv + pp
                q_pair = q_ref[:, p * LANES:(p + 1) * LANES] * scale
                zero = jnp.zeros_like(q_pair)
                same.append(jnp.where(on_side(kv), q_pair, zero))
                other.append(jnp.where(on_side(kv), swap_halves(q_pair), zero))
                hs.append(2 * p + side)
                ho.append(2 * p + 1 - side)
            lhs = jnp.concatenate(same + other, axis=0)
            logits.append(lax.dot_general(lhs, k[:, kv_cols(kv)], dims,
                                          preferred_element_type=F32))
            heads.append(hs + ho)
        results = []
        for kv in range(SW_KV_HEADS):
            v_pair = v[:, kv_cols(kv)]
            half = pairs_per_kv * BLOCK
            r_same = jnp.dot(p_ref[old, kv, :half, :], jnp.concatenate([v_pair, ones], axis=1),
                             preferred_element_type=F32)
            r_other = jnp.dot(p_ref[old, kv, half:, :],
                              jnp.concatenate([swap_halves(v_pair), ones], axis=1),
                              preferred_element_type=F32)
            results.append((r_same, r_other))

        for kv in range(SW_KV_HEADS):
            maxes = []
            for i, h in enumerate(heads[kv]):
                lg = logits[kv][i * BLOCK:(i + 1) * BLOCK, :] + bias_ref[0, h]
                m = jnp.maximum(jnp.max(lg, axis=-1, keepdims=True), sink_ref[h])
                p_ref[cur, kv, i * BLOCK:(i + 1) * BLOCK, :] = jnp.exp(lg - m).astype(BF16)
                maxes.append(m)
            for pp in range(pairs_per_kv):
                h_same, h_other = heads[kv][pp], heads[kv][pairs_per_kv + pp]
                m = jnp.where(on_side(kv), maxes[pp], maxes[pairs_per_kv + pp])
                sink = jnp.where(on_side(kv), sink_ref[h_same], sink_ref[h_other])
                st_ref[cur, kv * pairs_per_kv + pp] = jnp.exp(sink - m)

        for kv in range(SW_KV_HEADS):
            r_same, r_other = results[kv]
            for pp in range(pairs_per_kv):
                p = kv * pairs_per_kv + pp
                rows = slice(pp * BLOCK, (pp + 1) * BLOCK)
                num = jnp.where(on_side(kv), r_same[rows, :LANES], r_other[rows, :LANES])
                den = jnp.where(on_side(kv), r_same[rows, LANES:], r_other[rows, LANES:])
                o_ref[:, p * LANES:(p + 1) * LANES] = num / (den + st_ref[old, p])

    @pl.when(s % 2 == 0)
    def _():
        stages(0)

    @pl.when(s % 2 == 1)
    def _():
        stages(1)


def _sw_attention(proj, sinks, bias):
    kq = OFF_Q_SW // SW_WIDTH
    kk = OFF_K_SW // SW_KV_WIDTH
    kv = OFF_V_SW // SW_KV_WIDTH
    blocks = SEQ // BLOCK
    nxt = lambda s: jnp.minimum(s, blocks - 1)
    cur = lambda s: jnp.maximum(s - 1, 0)
    prev = lambda n: jnp.maximum(n - 1, 0)
    return pl.pallas_call(
        _sw_kernel,
        grid=(blocks + 1,),
        in_specs=[
            pl.BlockSpec(memory_space=pltpu.SMEM),
            pl.BlockSpec((BLOCK, SW_WIDTH), lambda s: (nxt(s), kq)),
            pl.BlockSpec((BLOCK, SW_KV_WIDTH), lambda s: (prev(nxt(s)), kk)),
            pl.BlockSpec((BLOCK, SW_KV_WIDTH), lambda s: (nxt(s), kk)),
            pl.BlockSpec((BLOCK, SW_KV_WIDTH), lambda s: (prev(cur(s)), kv)),
            pl.BlockSpec((BLOCK, SW_KV_WIDTH), lambda s: (cur(s), kv)),
            pl.BlockSpec((1, SW_HEADS, BLOCK, 2 * BLOCK),
                         lambda s: (jnp.minimum(nxt(s), 1), 0, 0, 0)),
        ],
        out_specs=pl.BlockSpec((BLOCK, SW_WIDTH), lambda s: (cur(s), 0)),
        out_shape=jax.ShapeDtypeStruct((SEQ, SW_WIDTH), F32),
        scratch_shapes=[
            pltpu.VMEM((2, SW_KV_HEADS, SW_HEADS // SW_KV_HEADS * BLOCK, 2 * BLOCK), BF16),
            pltpu.VMEM((2, SW_HEADS // 2, BLOCK, LANES), F32),
        ],
        compiler_params=pltpu.CompilerParams(
            dimension_semantics=("arbitrary",), vmem_limit_bytes=VMEM_LIMIT),
        name="sw_attention",
    )(sinks.astype(F32), proj, proj, proj, proj, proj, bias)


def _rms(x, g):
    ms = jnp.mean(x * x, axis=-1, keepdims=True)
    return (x * lax.rsqrt(ms + RMS_EPS)) * g


def _silu(z):
    return z * (1.0 / (1.0 + jnp.exp(-z)))


def _out_kernel(osb_ref, osw_ref, zsb_ref, zswa_ref, zswb_ref, gsb_ref, gsw_ref, gpost_ref,
                w_ref, x_ref, o_ref):
    chunk = OUT_TM // OUT_CHUNKS
    ys = []
    for c in range(OUT_CHUNKS):
        rows = pl.ds(c * chunk, chunk)
        a_sb = _rms(osb_ref[rows, :], gsb_ref[...]) * _silu(zsb_ref[rows, :].astype(F32))
        z_sw = jnp.concatenate([zswa_ref[rows, :], zswb_ref[rows, :]], axis=1).astype(F32)
        a_sw = _rms(osw_ref[rows, :], gsw_ref[...]) * _silu(z_sw)
        y = jnp.dot(a_sb.astype(BF16), w_ref[:SB_WIDTH, :], preferred_element_type=F32)
        ys.append(y + jnp.dot(a_sw.astype(BF16), w_ref[SB_WIDTH:, :],
                              preferred_element_type=F32))
    for c in range(OUT_CHUNKS):
        rows = pl.ds(c * chunk, chunk)
        o_ref[rows, :] = x_ref[rows, :] + _rms(ys[c], gpost_ref[...])


def _outproj(o_sb, o_sw, proj, gn_sb, gn_sw, norm_post, w_out_bf, x2):
    half = SW_WIDTH // 2
    return pl.pallas_call(
        _out_kernel,
        grid=(SEQ // OUT_TM,),
        in_specs=[
            pl.BlockSpec((OUT_TM, SB_WIDTH), lambda i: (i, 0)),
            pl.BlockSpec((OUT_TM, SW_WIDTH), lambda i: (i, 0)),
            pl.BlockSpec((OUT_TM, SB_WIDTH), lambda i: (i, OFF_Z_SB // SB_WIDTH)),
            pl.BlockSpec((OUT_TM, half), lambda i: (i, OFF_Z_SW // half)),
            pl.BlockSpec((OUT_TM, half), lambda i: (i, OFF_Z_SW // half + 1)),
            pl.BlockSpec((1, SB_WIDTH), lambda i: (0, 0)),
            pl.BlockSpec((1, SW_WIDTH), lambda i: (0, 0)),
            pl.BlockSpec((1, D_MODEL), lambda i: (0, 0)),
            pl.BlockSpec((D_MIX, D_MODEL), lambda i: (0, 0)),
            pl.BlockSpec((OUT_TM, D_MODEL), lambda i: (i, 0)),
        ],
        out_specs=pl.BlockSpec((OUT_TM, D_MODEL), lambda i: (i, 0)),
        out_shape=jax.ShapeDtypeStruct((SEQ, D_MODEL), F32),
        compiler_params=pltpu.CompilerParams(
            dimension_semantics=("arbitrary",), vmem_limit_bytes=VMEM_LIMIT),
        name="outproj",
    )(o_sb, o_sw, proj, proj, proj, gn_sb, gn_sw, norm_post, w_out_bf, x2)


def kernel(x, w_in, w_out, norm_pre, norm_post, gn_sb, gn_sw, sinks, rel_bias):
    assert x.shape == (1, SEQ, D_MODEL) and w_in.shape == (1, D_MODEL, D_IN_PROJ)
    assert OFF_Z_SW % (SW_WIDTH // 2) == 0 and OFF_Q_SW % SW_WIDTH == 0
    x2 = x.reshape(SEQ, D_MODEL)
    proj = _inproj(x2, norm_pre.astype(F32), w_in[0].astype(BF16))
    o_sb = _sb_attention(proj)
    bias = _bias_table(rel_bias)
    o_sw = _sw_attention(proj, sinks[0], bias)
    out = _outproj(o_sb, o_sw, proj, gn_sb.astype(F32), gn_sw.astype(F32), norm_post.astype(F32),
                   w_out[0].astype(BF16), x2)
    return out.reshape(1, SEQ, D_MODEL)
```

```python
import math

import numpy as np
import jax
import jax.numpy as jnp
from jax import lax
from jax.experimental import pallas as pl
from jax.experimental.pallas import tpu as pltpu

D_MODEL = 2048
SEQ = 16384
HEAD_DIM = 64
SB_HEADS = 16
SW_HEADS = 16
SW_KV_HEADS = 4
SB_WIDTH = SB_HEADS * HEAD_DIM
SW_WIDTH = SW_HEADS * HEAD_DIM
SW_KV_WIDTH = SW_KV_HEADS * HEAD_DIM
D_MIX = SB_WIDTH + SW_WIDTH
D_IN_PROJ = 4 * SB_WIDTH + 2 * SW_WIDTH + 2 * SW_KV_WIDTH
WINDOW = 128
BLOCK = 128
N_BUCKETS = 32
MAX_DISTANCE = 128
RMS_EPS = 1e-6
SCALE = HEAD_DIM ** -0.5

OFF_Q_SB = 0
OFF_K_SB = SB_WIDTH
OFF_V_SB = 2 * SB_WIDTH
OFF_Z_SB = 3 * SB_WIDTH
OFF_Q_SW = 4 * SB_WIDTH
OFF_K_SW = OFF_Q_SW + SW_WIDTH
OFF_V_SW = OFF_K_SW + SW_KV_WIDTH
OFF_Z_SW = OFF_V_SW + SW_KV_WIDTH

LANES = 128
VMEM_LIMIT = 56 * 1024 * 1024

PROJ_TM = 512
PROJ_TN = 3328
SB_T = 256
SB_H = SB_T // 2
SB_UNITS = 1
OUT_TM = 512
OUT_CHUNKS = 4
NEG_BIG = -1e30
SB_STOP = -110.0

F32 = jnp.float32
BF16 = jnp.bfloat16


def _inproj_kernel(x_ref, g_ref, w_ref, o_ref):
    x = x_ref[...]
    ms = jnp.mean(x * x, axis=-1, keepdims=True)
    xn = ((x * lax.rsqrt(ms + RMS_EPS)) * g_ref[...]).astype(BF16)
    for j in range(D_IN_PROJ // PROJ_TN):
        cols = slice(j * PROJ_TN, (j + 1) * PROJ_TN)
        o_ref[:, cols] = jnp.dot(xn, w_ref[:, cols], preferred_element_type=F32).astype(BF16)


def _inproj(x2, g, w_bf):
    return pl.pallas_call(
        _inproj_kernel,
        grid=(SEQ // PROJ_TM,),
        in_specs=[
            pl.BlockSpec((PROJ_TM, D_MODEL), lambda i: (i, 0)),
            pl.BlockSpec((1, D_MODEL), lambda i: (0, 0)),
            pl.BlockSpec((D_MODEL, D_IN_PROJ), lambda i: (0, 0), pipeline_mode=pl.Buffered(1)),
        ],
        out_specs=pl.BlockSpec((PROJ_TM, D_IN_PROJ), lambda i: (i, 0)),
        out_shape=jax.ShapeDtypeStruct((SEQ, D_IN_PROJ), BF16),
        compiler_params=pltpu.CompilerParams(
            dimension_semantics=("arbitrary",), vmem_limit_bytes=VMEM_LIMIT),
        name="inproj",
    )(x2, g, w_bf)


def _sb_kernel(q_ref, k_ref, v_ref, tri_ref, o_ref,
               hi_ref, zl_ref, zr_ref, zp_ref, w_ref, carry_w_ref, go_ref,
               acc_ref, carry_ref):
    s = pl.program_id(1)
    last = SEQ // SB_T - 1
    n = jnp.minimum(s, last)
    t = jnp.maximum(s - 2, 0)

    @pl.when(s == 0)
    def _():
        for ref in (hi_ref, zl_ref, zr_ref, zp_ref, w_ref, carry_w_ref):
            ref[...] = jnp.zeros_like(ref)

    lane = lax.broadcasted_iota(jnp.int32, (SB_T, LANES), 1)

    def cols(u):
        return slice(u * LANES, (u + 1) * LANES)

    def stack_heads(tile, u):
        start = pl.multiple_of(tile * SB_T, SB_T)
        q = q_ref[pl.ds(start, SB_T), cols(u)] * jnp.asarray(SCALE, BF16)
        zero = jnp.zeros_like(q)
        return jnp.concatenate([jnp.where(lane < HEAD_DIM, q, zero),
                                jnp.where(lane >= HEAD_DIM, q, zero)], axis=0)

    tri = tri_ref[...]
    r_h = lax.broadcasted_iota(jnp.int32, (SB_H, SB_H), 0)
    c_h = lax.broadcasted_iota(jnp.int32, (SB_H, SB_H), 1)
    before_h = c_h < r_h
    zeros_h = jnp.zeros((SB_H, SB_H), BF16)

    def qk(qs, kb, u):
        start = pl.multiple_of(kb * SB_T, SB_T)
        kblk = k_ref[pl.ds(start, SB_T), cols(u)]
        return lax.dot_general(qs, kblk, (((1,), (1,)), ((), ())), preferred_element_type=F32)

    def softplus(z):
        neg_abs = lax.bitcast_convert_type(
            lax.bitcast_convert_type(z, jnp.int32) | jnp.int32(-2 ** 31), F32)
        return (jnp.maximum(z, 0.0) + jnp.log(1.0 + jnp.exp(neg_abs))).astype(BF16)

    def cut(z):
        return jnp.where(before_h, z, NEG_BIG)

    def quarters(a):
        left = a[:, :SB_H]
        right = jnp.concatenate([a[SB_H:SB_T, SB_H:], a[SB_T + SB_H:, SB_H:]], axis=0)
        return left, right

    def unquarter(left, right):
        right_full = jnp.concatenate([zeros_h, right[:SB_H], zeros_h, right[SB_H:]], axis=0)
        return jnp.concatenate([left, right_full], axis=1)

    def cumsum(hi):
        return jnp.dot(hi, tri, preferred_element_type=F32)

    def values(kb, w, u, v_valid=None):
        start = pl.multiple_of(kb * SB_T, SB_T)
        vblk = v_ref[pl.ds(start, SB_T), cols(u)]
        if v_valid is not None:
            vblk = jnp.where(v_valid, vblk, jnp.zeros_like(vblk))
        return jnp.dot(w, vblk, preferred_element_type=F32)

    def stages(cur):
        old = 1 - cur
        units = range(SB_UNITS)
        qs_n = [stack_heads(n, u) for u in units]
        z_d = [qk(qs_n[u], n, u) for u in units]
        z_p = [qk(qs_n[u], jnp.maximum(n - 1, 0), u) for u in units]
        cs_d = [cumsum(hi_ref[u, old, 0]) for u in units]
        cs_p = [cumsum(hi_ref[u, old, 1]) for u in units]
        for u in units:
            acc_ref[u] = (values(t, w_ref[u, old, 0], u)
                          + values(jnp.maximum(t - 1, 0), w_ref[u, old, 1], u, v_valid=t > 0))
            carry_t = carry_w_ref[u, old]
            carry_ref[u] = carry_t
            go_ref[u] = (jnp.max(carry_t) >= SB_STOP).astype(jnp.int32)

        for u in units:
            zl, zr = quarters(z_d[u])
            zl = jnp.concatenate([cut(zl[:SB_H]), zl[SB_H:SB_T],
                                  cut(zl[SB_T:SB_T + SB_H]), zl[SB_T + SB_H:]], axis=0)
            zr = jnp.concatenate([cut(zr[:SB_H]), cut(zr[SB_H:])], axis=0)
            hi_ref[u, cur, 0] = unquarter(softplus(zl), softplus(zr))
            hi_ref[u, cur, 1] = softplus(z_p[u])
            zl_ref[u, cur] = zl
            zr_ref[u, cur] = zr
            zp_ref[u, cur] = z_p[u]

            cs_l, cs_r = quarters(cs_d[u])
            w_l = jnp.exp(zl_ref[u, old] + cs_l).astype(BF16)
            w_r = jnp.exp(zr_ref[u, old] + cs_r).astype(BF16)
            carry_d = cs_d[u][:, 0:1]
            w_p = jnp.exp(zp_ref[u, old] + cs_p[u] + carry_d).astype(BF16)
            w_ref[u, cur, 0] = unquarter(w_l, w_r)
            w_ref[u, cur, 1] = w_p
            carry_w_ref[u, cur] = carry_d + cs_p[u][:, 0:1]

    @pl.when(s % 2 == 0)
    def _():
        stages(0)

    @pl.when(s % 2 == 1)
    def _():
        stages(1)

    def block(qs, kb, u):
        z = qk(qs, kb, u)
        cs = cumsum(softplus(z))
        carry = carry_ref[u]
        w = jnp.exp(z + cs + carry).astype(BF16)
        acc_ref[u] += values(kb, w, u)
        carry_ref[u] = carry + cs[:, 0:1]

    def cond(state):
        kb, go = state
        return jnp.logical_and(kb >= 0, go > 0)

    for u in range(SB_UNITS):
        def body(state, u=u):
            kb, _ = state
            block(stack_heads(t, u), kb, u)
            go = (jnp.max(carry_ref[u]) >= SB_STOP).astype(jnp.int32)
            return kb - 1, go

        lax.while_loop(cond, body, (t - 2, go_ref[u]))
        o_ref[:, cols(u)] = jnp.where(lane < HEAD_DIM, acc_ref[u, :SB_T, :],
                                      acc_ref[u, SB_T:, :]).astype(BF16)


def _sb_tri():
    j = np.arange(SB_T)[:, None]
    s = np.arange(SB_T)[None, :]
    return jnp.asarray(np.where(j >= s, -1.0, 0.0).astype(np.float32), BF16)


def _sb_attention(proj):
    wide = SB_UNITS * LANES
    tiles = SEQ // SB_T
    U = SB_UNITS
    return pl.pallas_call(
        _sb_kernel,
        grid=(SB_WIDTH // wide, tiles + 2),
        in_specs=[
            pl.BlockSpec((SEQ, wide), lambda p, s: (0, OFF_Q_SB // wide + p)),
            pl.BlockSpec((SEQ, wide), lambda p, s: (0, OFF_K_SB // wide + p)),
            pl.BlockSpec((SEQ, wide), lambda p, s: (0, OFF_V_SB // wide + p)),
            pl.BlockSpec((SB_T, SB_T), lambda p, s: (0, 0)),
        ],
        out_specs=pl.BlockSpec((SB_T, wide), lambda p, s: (jnp.maximum(s - 2, 0), p)),
        out_shape=jax.ShapeDtypeStruct((SEQ, SB_WIDTH), BF16),
        scratch_shapes=[
            pltpu.VMEM((U, 2, 2, 2 * SB_T, SB_T), BF16),
            pltpu.VMEM((U, 2, 2 * SB_T, SB_H), F32),
            pltpu.VMEM((U, 2, SB_T, SB_H), F32),
            pltpu.VMEM((U, 2, 2 * SB_T, SB_T), F32),
            pltpu.VMEM((U, 2, 2, 2 * SB_T, SB_T), BF16),
            pltpu.VMEM((U, 2, 2 * SB_T, 1), F32),
            pltpu.SMEM((U,), jnp.int32),
            pltpu.VMEM((U, 2 * SB_T, LANES), F32),
            pltpu.VMEM((U, 2 * SB_T, 1), F32),
        ],
        compiler_params=pltpu.CompilerParams(
            dimension_semantics=("arbitrary", "arbitrary"), vmem_limit_bytes=VMEM_LIMIT),
        name="sb_attention",
    )(proj, proj, proj, _sb_tri())


def _bucket_table():
    qi = np.arange(BLOCK)[:, None]
    ci = np.arange(2 * BLOCK)[None, :]
    dist = qi + BLOCK - ci
    n = np.maximum(dist, 0)
    max_exact = N_BUCKETS // 2
    nf = np.maximum(n, 1).astype(np.float64)
    val = np.log(nf / max_exact) / math.log(MAX_DISTANCE / max_exact) * (N_BUCKETS - max_exact)
    in_window = (dist >= 0) & (dist < WINDOW)
    frac = val - np.floor(val)
    chk = in_window & (n > max_exact)
    assert np.all((frac[chk] > 1e-3) & (frac[chk] < 1 - 1e-3))
    large = np.minimum(max_exact + val.astype(np.int64), N_BUCKETS - 1)
    bucket = np.where(n < max_exact, n, large)
    later = np.where(in_window, bucket, -1)
    first = np.where(ci >= BLOCK, later, -1)
    return jnp.asarray(np.stack([first, later]).astype(np.int32))


def _bias_kernel(rb_ref, bucket_ref, o_ref):
    b = bucket_ref[0]
    hits = [b == k for k in range(N_BUCKETS)]
    for h in range(SW_HEADS):
        acc = jnp.full(b.shape, NEG_BIG, F32)
        for k in range(N_BUCKETS):
            acc = jnp.where(hits[k], rb_ref[k, h], acc)
        o_ref[0, h] = acc


def _bias_table(rel_bias):
    return pl.pallas_call(
        _bias_kernel,
        grid=(2,),
        in_specs=[
            pl.BlockSpec(memory_space=pltpu.SMEM),
            pl.BlockSpec((1, BLOCK, 2 * BLOCK), lambda v: (v, 0, 0)),
        ],
        out_specs=pl.BlockSpec((1, SW_HEADS, BLOCK, 2 * BLOCK), lambda v: (v, 0, 0, 0)),
        out_shape=jax.ShapeDtypeStruct((2, SW_HEADS, BLOCK, 2 * BLOCK), F32),
        name="t5_bias",
    )(rel_bias.astype(F32), _bucket_table())


def _sw_kernel(sink_ref, q_ref, kp_ref, kc_ref, vp_ref, vc_ref, bias_ref, o_ref, p_ref, st_ref):
    s = pl.program_id(0)

    @pl.when(s == 0)
    def _():
        p_ref[...] = jnp.zeros_like(p_ref)
        st_ref[...] = jnp.zeros_like(st_ref)

    lane = lax.broadcasted_iota(jnp.int32, (BLOCK, LANES), 1)
    ones = jnp.ones((2 * BLOCK, LANES), BF16)
    scale = jnp.asarray(SCALE, BF16)
    pairs_per_kv = SW_HEADS // SW_KV_HEADS // 2
    dims = (((1,), (1,)), ((), ()))

    def swap_halves(t):
        return jnp.concatenate([t[:, HEAD_DIM:], t[:, :HEAD_DIM]], axis=1)

    def on_side(kv):
        return (lane >= HEAD_DIM) if kv % 2 else (lane < HEAD_DIM)

    def kv_cols(kv):
        return slice((kv // 2) * LANES, (kv // 2 + 1) * LANES)

    def stages(cur):
        old = 1 - cur
        k = jnp.concatenate([kp_ref[...], kc_ref[...]], axis=0)
        v = jnp.concatenate([vp_ref[...], vc_ref[...]], axis=0)
        logits, heads = [], []
        for kv in range(SW_KV_HEADS):
            side = kv % 2
            same, other, hs, ho = [], [], [], []
            for pp in range(pairs_per_kv):
                p = kv * pairs_per_kv + pp
                q_pair = q_ref[:, p * LANES:(p + 1) * LANES] * scale
                zero = jnp.zeros_like(q_pair)
                same.append(jnp.where(on_side(kv), q_pair, zero))
                other.append(jnp.where(on_side(kv), swap_halves(q_pair), zero))
                hs.append(2 * p + side)
                ho.append(2 * p + 1 - side)
            lhs = jnp.concatenate(same + other, axis=0)
            logits.append(lax.dot_general(lhs, k[:, kv_cols(kv)], dims,
                                          preferred_element_type=F32))
            heads.append(hs + ho)
        results = []
        for kv in range(SW_KV_HEADS):
            v_pair = v[:, kv_cols(kv)]
            half = pairs_per_kv * BLOCK
            r_same = jnp.dot(p_ref[old, kv, :half, :], jnp.concatenate([v_pair, ones], axis=1),
                             preferred_element_type=F32)
            r_other = jnp.dot(p_ref[old, kv, half:, :],
                              jnp.concatenate([swap_halves(v_pair), ones], axis=1),
                              preferred_element_type=F32)
            results.append((r_same, r_other))

        for kv in range(SW_KV_HEADS):
            maxes = []
            for i, h in enumerate(heads[kv]):
                lg = logits[kv][i * BLOCK:(i + 1) * BLOCK, :] + bias_ref[0, h]
                m = jnp.maximum(jnp.max(lg, axis=-1, keepdims=True), sink_ref[h])
                p_ref[cur, kv, i * BLOCK:(i + 1) * BLOCK, :] = jnp.exp(lg - m).astype(BF16)
                maxes.append(m)
            for pp in range(pairs_per_kv):
                h_same, h_other = heads[kv][pp], heads[kv][pairs_per_kv + pp]
                m = jnp.where(on_side(kv), maxes[pp], maxes[pairs_per_kv + pp])
                sink = jnp.where(on_side(kv), sink_ref[h_same], sink_ref[h_other])
                st_ref[cur, kv * pairs_per_kv + pp] = jnp.exp(sink - m)

        for kv in range(SW_KV_HEADS):
            r_same, r_other = results[kv]
            for pp in range(pairs_per_kv):
                p = kv * pairs_per_kv + pp
                rows = slice(pp * BLOCK, (pp + 1) * BLOCK)
                num = jnp.where(on_side(kv), r_same[rows, :LANES], r_other[rows, :LANES])
                den = jnp.where(on_side(kv), r_same[rows, LANES:], r_other[rows, LANES:])
                o_ref[:, p * LANES:(p + 1) * LANES] = (num / (den + st_ref[old, p])).astype(BF16)

    @pl.when(s % 2 == 0)
    def _():
        stages(0)

    @pl.when(s % 2 == 1)
    def _():
        stages(1)


def _sw_attention(proj, sinks, bias):
    kq = OFF_Q_SW // SW_WIDTH
    kk = OFF_K_SW // SW_KV_WIDTH
    kv = OFF_V_SW // SW_KV_WIDTH
    blocks = SEQ // BLOCK
    nxt = lambda s: jnp.minimum(s, blocks - 1)
    cur = lambda s: jnp.maximum(s - 1, 0)
    prev = lambda n: jnp.maximum(n - 1, 0)
    return pl.pallas_call(
        _sw_kernel,
        grid=(blocks + 1,),
        in_specs=[
            pl.BlockSpec(memory_space=pltpu.SMEM),
            pl.BlockSpec((BLOCK, SW_WIDTH), lambda s: (nxt(s), kq)),
            pl.BlockSpec((BLOCK, SW_KV_WIDTH), lambda s: (prev(nxt(s)), kk)),
            pl.BlockSpec((BLOCK, SW_KV_WIDTH), lambda s: (nxt(s), kk)),
            pl.BlockSpec((BLOCK, SW_KV_WIDTH), lambda s: (prev(cur(s)), kv)),
            pl.BlockSpec((BLOCK, SW_KV_WIDTH), lambda s: (cur(s), kv)),
            pl.BlockSpec((1, SW_HEADS, BLOCK, 2 * BLOCK),
                         lambda s: (jnp.minimum(nxt(s), 1), 0, 0, 0)),
        ],
        out_specs=pl.BlockSpec((BLOCK, SW_WIDTH), lambda s: (cur(s), 0)),
        out_shape=jax.ShapeDtypeStruct((SEQ, SW_WIDTH), BF16),
        scratch_shapes=[
            pltpu.VMEM((2, SW_KV_HEADS, SW_HEADS // SW_KV_HEADS * BLOCK, 2 * BLOCK), BF16),
            pltpu.VMEM((2, SW_HEADS // 2, BLOCK, LANES), F32),
        ],
        compiler_params=pltpu.CompilerParams(
            dimension_semantics=("arbitrary",), vmem_limit_bytes=VMEM_LIMIT),
        name="sw_attention",
    )(sinks.astype(F32), proj, proj, proj, proj, proj, bias)


def _rms(x, g):
    ms = jnp.mean(x * x, axis=-1, keepdims=True)
    return (x * lax.rsqrt(ms + RMS_EPS)) * g


def _silu(z):
    return z * (1.0 / (1.0 + jnp.exp(-z)))


def _out_kernel(osb_ref, osw_ref, zsb_ref, zswa_ref, zswb_ref, gsb_ref, gsw_ref, gpost_ref,
                w_ref, x_ref, o_ref):
    chunk = OUT_TM // OUT_CHUNKS
    ys = []
    for c in range(OUT_CHUNKS):
        rows = pl.ds(c * chunk, chunk)
        a_sb = (_rms(osb_ref[rows, :].astype(F32), gsb_ref[...])
                * _silu(zsb_ref[rows, :].astype(F32)))
        z_sw = jnp.concatenate([zswa_ref[rows, :], zswb_ref[rows, :]], axis=1).astype(F32)
        a_sw = _rms(osw_ref[rows, :].astype(F32), gsw_ref[...]) * _silu(z_sw)
        y = jnp.dot(a_sb.astype(BF16), w_ref[:SB_WIDTH, :], preferred_element_type=F32)
        ys.append(y + jnp.dot(a_sw.astype(BF16), w_ref[SB_WIDTH:, :],
                              preferred_element_type=F32))
    for c in range(OUT_CHUNKS):
        rows = pl.ds(c * chunk, chunk)
        o_ref[rows, :] = x_ref[rows, :] + _rms(ys[c], gpost_ref[...])


def _outproj(o_sb, o_sw, proj, gn_sb, gn_sw, norm_post, w_out_bf, x2):
    half = SW_WIDTH // 2
    return pl.pallas_call(
        _out_kernel,
        grid=(SEQ // OUT_TM,),
        in_specs=[
            pl.BlockSpec((OUT_TM, SB_WIDTH), lambda i: (i, 0)),
            pl.BlockSpec((OUT_TM, SW_WIDTH), lambda i: (i, 0)),
            pl.BlockSpec((OUT_TM, SB_WIDTH), lambda i: (i, OFF_Z_SB // SB_WIDTH)),
            pl.BlockSpec((OUT_TM, half), lambda i: (i, OFF_Z_SW // half)),
            pl.BlockSpec((OUT_TM, half), lambda i: (i, OFF_Z_SW // half + 1)),
            pl.BlockSpec((1, SB_WIDTH), lambda i: (0, 0)),
            pl.BlockSpec((1, SW_WIDTH), lambda i: (0, 0)),
            pl.BlockSpec((1, D_MODEL), lambda i: (0, 0)),
            pl.BlockSpec((D_MIX, D_MODEL), lambda i: (0, 0), pipeline_mode=pl.Buffered(1)),
            pl.BlockSpec((OUT_TM, D_MODEL), lambda i: (i, 0)),
        ],
        out_specs=pl.BlockSpec((OUT_TM, D_MODEL), lambda i: (i, 0)),
        out_shape=jax.ShapeDtypeStruct((SEQ, D_MODEL), F32),
        compiler_params=pltpu.CompilerParams(
            dimension_semantics=("arbitrary",), vmem_limit_bytes=VMEM_LIMIT),
        name="outproj",
    )(o_sb, o_sw, proj, proj, proj, gn_sb, gn_sw, norm_post, w_out_bf, x2)


def kernel(x, w_in, w_out, norm_pre, norm_post, gn_sb, gn_sw, sinks, rel_bias):
    assert x.shape == (1, SEQ, D_MODEL) and w_in.shape == (1, D_MODEL, D_IN_PROJ)
    assert OFF_Z_SW % (SW_WIDTH // 2) == 0 and OFF_Q_SW % SW_WIDTH == 0
    x2 = x.reshape(SEQ, D_MODEL)
    proj = _inproj(x2, norm_pre.astype(F32), w_in[0].astype(BF16))
    o_sb = _sb_attention(proj)
    bias = _bias_table(rel_bias)
    o_sw = _sw_attention(proj, sinks[0], bias)
    out = _outproj(o_sb, o_sw, proj, gn_sb.astype(F32), gn_sw.astype(F32), norm_post.astype(F32),
                   w_out[0].astype(BF16), x2)
    return out.reshape(1, SEQ, D_MODEL)
```

```python
import math

import numpy as np
import jax
import jax.numpy as jnp
from jax import lax
from jax.experimental import pallas as pl
from jax.experimental.pallas import tpu as pltpu

D_MODEL = 2048
SEQ = 16384
HEAD_DIM = 64
SB_HEADS = 16
SW_HEADS = 16
SW_KV_HEADS = 4
SB_WIDTH = SB_HEADS * HEAD_DIM
SW_WIDTH = SW_HEADS * HEAD_DIM
SW_KV_WIDTH = SW_KV_HEADS * HEAD_DIM
D_MIX = SB_WIDTH + SW_WIDTH
D_IN_PROJ = 4 * SB_WIDTH + 2 * SW_WIDTH + 2 * SW_KV_WIDTH
WINDOW = 128
BLOCK = 128
N_BUCKETS = 32
MAX_DISTANCE = 128
RMS_EPS = 1e-6
SCALE = HEAD_DIM ** -0.5

OFF_Q_SB = 0
OFF_K_SB = SB_WIDTH
OFF_V_SB = 2 * SB_WIDTH
OFF_Z_SB = 3 * SB_WIDTH
OFF_Q_SW = 4 * SB_WIDTH
OFF_K_SW = OFF_Q_SW + SW_WIDTH
OFF_V_SW = OFF_K_SW + SW_KV_WIDTH
OFF_Z_SW = OFF_V_SW + SW_KV_WIDTH

LANES = 128
VMEM_LIMIT = 56 * 1024 * 1024

PROJ_TM = 512
PROJ_TN = 3328
SB_T = 256
SB_H = SB_T // 2
SB_UNITS = 1
OUT_TM = 512
OUT_CHUNKS = 4
NEG_BIG = -1e30
SB_STOP = -110.0

F32 = jnp.float32
BF16 = jnp.bfloat16


def _inproj_kernel(x_ref, g_ref, w_ref, o_ref):
    x = x_ref[...]
    ms = jnp.mean(x * x, axis=-1, keepdims=True)
    xn = ((x * lax.rsqrt(ms + RMS_EPS)) * g_ref[...]).astype(BF16)
    gates = ((OFF_Z_SB, OFF_Z_SB + SB_WIDTH), (OFF_Z_SW, OFF_Z_SW + SW_WIDTH))
    for j in reversed(range(D_IN_PROJ // PROJ_TN)):
        lo, hi = j * PROJ_TN, (j + 1) * PROJ_TN
        res = jnp.dot(xn, w_ref[:, lo:hi], preferred_element_type=F32)
        cuts = sorted({lo, hi} | {min(max(c, lo), hi) for g in gates for c in g})
        for a, b in zip(cuts[:-1], cuts[1:]):
            part = res[:, a - lo:b - lo]
            if any(g0 <= a and b <= g1 for g0, g1 in gates):
                part = _silu(part)
            o_ref[:, a:b] = part.astype(BF16)


def _inproj(x2, g, w_bf):
    return pl.pallas_call(
        _inproj_kernel,
        grid=(SEQ // PROJ_TM,),
        in_specs=[
            pl.BlockSpec((PROJ_TM, D_MODEL), lambda i: (i, 0)),
            pl.BlockSpec((1, D_MODEL), lambda i: (0, 0)),
            pl.BlockSpec((D_MODEL, D_IN_PROJ), lambda i: (0, 0), pipeline_mode=pl.Buffered(1)),
        ],
        out_specs=pl.BlockSpec((PROJ_TM, D_IN_PROJ), lambda i: (i, 0)),
        out_shape=jax.ShapeDtypeStruct((SEQ, D_IN_PROJ), BF16),
        compiler_params=pltpu.CompilerParams(
            dimension_semantics=("arbitrary",), vmem_limit_bytes=VMEM_LIMIT),
        name="inproj",
    )(x2, g, w_bf)


def _sb_kernel(q_ref, k_ref, v_ref, tri_ref, o_ref,
               hi_ref, zl_ref, zr_ref, zp_ref, w_ref, carry_w_ref, go_ref,
               acc_ref, carry_ref):
    step = pl.program_id(1)
    last = SEQ // SB_T - 1

    @pl.when(step == 0)
    def _():
        for ref in (hi_ref, zl_ref, zr_ref, zp_ref, w_ref, carry_w_ref):
            ref[...] = jnp.zeros_like(ref)

    lane = lax.broadcasted_iota(jnp.int32, (SB_T, LANES), 1)

    def cols(u):
        return slice(u * LANES, (u + 1) * LANES)

    def stack_heads(tile, u):
        start = pl.multiple_of(tile * SB_T, SB_T)
        q = q_ref[pl.ds(start, SB_T), cols(u)] * jnp.asarray(SCALE, BF16)
        zero = jnp.zeros_like(q)
        return jnp.concatenate([jnp.where(lane < HEAD_DIM, q, zero),
                                jnp.where(lane >= HEAD_DIM, q, zero)], axis=0)

    tri = tri_ref[...]
    r_h = lax.broadcasted_iota(jnp.int32, (SB_H, SB_H), 0)
    c_h = lax.broadcasted_iota(jnp.int32, (SB_H, SB_H), 1)
    before_h = c_h < r_h
    zeros_h = jnp.zeros((SB_H, SB_H), BF16)

    def qk(qs, kb, u):
        start = pl.multiple_of(kb * SB_T, SB_T)
        kblk = k_ref[pl.ds(start, SB_T), cols(u)]
        return lax.dot_general(qs, kblk, (((1,), (1,)), ((), ())), preferred_element_type=F32)

    def softplus(z):
        neg_abs = lax.bitcast_convert_type(
            lax.bitcast_convert_type(z, jnp.int32) | jnp.int32(-2 ** 31), F32)
        return (jnp.maximum(z, 0.0) + jnp.log(1.0 + jnp.exp(neg_abs))).astype(BF16)

    def cut(z):
        return jnp.where(before_h, z, NEG_BIG)

    def quarters(a):
        left = a[:, :SB_H]
        right = jnp.concatenate([a[SB_H:SB_T, SB_H:], a[SB_T + SB_H:, SB_H:]], axis=0)
        return left, right

    def unquarter(left, right):
        right_full = jnp.concatenate([zeros_h, right[:SB_H], zeros_h, right[SB_H:]], axis=0)
        return jnp.concatenate([left, right_full], axis=1)

    def cumsum(hi):
        return jnp.dot(hi, tri, preferred_element_type=F32)

    def values(kb, w, u, v_valid=None):
        start = pl.multiple_of(kb * SB_T, SB_T)
        vblk = v_ref[pl.ds(start, SB_T), cols(u)]
        if v_valid is not None:
            vblk = jnp.where(v_valid, vblk, jnp.zeros_like(vblk))
        return jnp.dot(w, vblk, preferred_element_type=F32)

    def stages(cur, n, t):
        old = 1 - cur
        units = range(SB_UNITS)
        qs_n = [stack_heads(n, u) for u in units]
        z_d = [qk(qs_n[u], n, u) for u in units]
        z_p = [qk(qs_n[u], jnp.maximum(n - 1, 0), u) for u in units]
        cs_d = [cumsum(hi_ref[u, old, 0]) for u in units]
        cs_p = [cumsum(hi_ref[u, old, 1]) for u in units]
        for u in units:
            acc_ref[u] = (values(t, w_ref[u, old, 0], u)
                          + values(jnp.maximum(t - 1, 0), w_ref[u, old, 1], u, v_valid=t > 0))
            carry_t = carry_w_ref[u, old]
            carry_ref[u] = carry_t
            go_ref[u] = (jnp.max(carry_t) >= SB_STOP).astype(jnp.int32)

        for u in units:
            zl, zr = quarters(z_d[u])
            zl = jnp.concatenate([cut(zl[:SB_H]), zl[SB_H:SB_T],
                                  cut(zl[SB_T:SB_T + SB_H]), zl[SB_T + SB_H:]], axis=0)
            zr = jnp.concatenate([cut(zr[:SB_H]), cut(zr[SB_H:])], axis=0)
            hi_ref[u, cur, 0] = unquarter(softplus(zl), softplus(zr))
            hi_ref[u, cur, 1] = softplus(z_p[u])
            zl_ref[u, cur] = zl
            zr_ref[u, cur] = zr
            zp_ref[u, cur] = z_p[u]

            cs_l, cs_r = quarters(cs_d[u])
            w_l = jnp.exp(zl_ref[u, old] + cs_l).astype(BF16)
            w_r = jnp.exp(zr_ref[u, old] + cs_r).astype(BF16)
            carry_d = cs_d[u][:, 0:1]
            w_p = jnp.exp(zp_ref[u, old] + cs_p[u] + carry_d).astype(BF16)
            w_ref[u, cur, 0] = unquarter(w_l, w_r)
            w_ref[u, cur, 1] = w_p
            carry_w_ref[u, cur] = carry_d + cs_p[u][:, 0:1]

    def block(qs, kb, u):
        z = qk(qs, kb, u)
        cs = cumsum(softplus(z))
        carry = carry_ref[u]
        w = jnp.exp(z + cs + carry).astype(BF16)
        acc_ref[u] += values(kb, w, u)
        carry_ref[u] = carry + cs[:, 0:1]

    def cond(state):
        kb, go = state
        return jnp.logical_and(kb >= 0, go > 0)

    for half in range(2):
        s = 2 * step + half
        t = jnp.maximum(s - 2, 0)
        stages(half, jnp.minimum(s, last), t)
        for u in range(SB_UNITS):
            def body(state, t=t, u=u):
                kb, _ = state
                block(stack_heads(t, u), kb, u)
                go = (jnp.max(carry_ref[u]) >= SB_STOP).astype(jnp.int32)
                return kb - 1, go

            lax.while_loop(cond, body, (t - 2, go_ref[u]))
            o_ref[half * SB_T:(half + 1) * SB_T, cols(u)] = jnp.where(
                lane < HEAD_DIM, acc_ref[u, :SB_T, :], acc_ref[u, SB_T:, :]).astype(BF16)


def _sb_tri():
    j = np.arange(SB_T)[:, None]
    s = np.arange(SB_T)[None, :]
    return jnp.asarray(np.where(j >= s, -1.0, 0.0).astype(np.float32), BF16)


def _sb_attention(proj):
    wide = SB_UNITS * LANES
    tiles = SEQ // SB_T
    U = SB_UNITS
    return pl.pallas_call(
        _sb_kernel,
        grid=(SB_WIDTH // wide, tiles // 2 + 1),
        in_specs=[
            pl.BlockSpec((SEQ, wide), lambda p, s: (0, OFF_Q_SB // wide + p)),
            pl.BlockSpec((SEQ, wide), lambda p, s: (0, OFF_K_SB // wide + p)),
            pl.BlockSpec((SEQ, wide), lambda p, s: (0, OFF_V_SB // wide + p)),
            pl.BlockSpec((SB_T, SB_T), lambda p, s: (0, 0)),
        ],
        out_specs=pl.BlockSpec((2 * SB_T, wide), lambda p, s: (jnp.maximum(s - 1, 0), p)),
        out_shape=jax.ShapeDtypeStruct((SEQ, SB_WIDTH), BF16),
        scratch_shapes=[
            pltpu.VMEM((U, 2, 2, 2 * SB_T, SB_T), BF16),
            pltpu.VMEM((U, 2, 2 * SB_T, SB_H), F32),
            pltpu.VMEM((U, 2, SB_T, SB_H), F32),
            pltpu.VMEM((U, 2, 2 * SB_T, SB_T), F32),
            pltpu.VMEM((U, 2, 2, 2 * SB_T, SB_T), BF16),
            pltpu.VMEM((U, 2, 2 * SB_T, 1), F32),
            pltpu.SMEM((U,), jnp.int32),
            pltpu.VMEM((U, 2 * SB_T, LANES), F32),
            pltpu.VMEM((U, 2 * SB_T, 1), F32),
        ],
        compiler_params=pltpu.CompilerParams(
            dimension_semantics=("arbitrary", "arbitrary"), vmem_limit_bytes=VMEM_LIMIT),
        name="sb_attention",
    )(proj, proj, proj, _sb_tri())


def _bucket_table():
    qi = np.arange(BLOCK)[:, None]
    ci = np.arange(2 * BLOCK)[None, :]
    dist = qi + BLOCK - ci
    n = np.maximum(dist, 0)
    max_exact = N_BUCKETS // 2
    nf = np.maximum(n, 1).astype(np.float64)
    val = np.log(nf / max_exact) / math.log(MAX_DISTANCE / max_exact) * (N_BUCKETS - max_exact)
    in_window = (dist >= 0) & (dist < WINDOW)
    frac = val - np.floor(val)
    chk = in_window & (n > max_exact)
    assert np.all((frac[chk] > 1e-3) & (frac[chk] < 1 - 1e-3))
    large = np.minimum(max_exact + val.astype(np.int64), N_BUCKETS - 1)
    bucket = np.where(n < max_exact, n, large)
    later = np.where(in_window, bucket, -1)
    first = np.where(ci >= BLOCK, later, -1)
    return jnp.asarray(np.stack([first, later]).astype(np.int32))


def _bias_kernel(rb_ref, bucket_ref, o_ref):
    b = bucket_ref[0]
    hits = [b == k for k in range(N_BUCKETS)]
    for h in range(SW_HEADS):
        acc = jnp.full(b.shape, NEG_BIG, F32)
        for k in range(N_BUCKETS):
            acc = jnp.where(hits[k], rb_ref[k, h], acc)
        o_ref[0, h] = acc


def _bias_table(rel_bias):
    return pl.pallas_call(
        _bias_kernel,
        grid=(2,),
        in_specs=[
            pl.BlockSpec(memory_space=pltpu.SMEM),
            pl.BlockSpec((1, BLOCK, 2 * BLOCK), lambda v: (v, 0, 0)),
        ],
        out_specs=pl.BlockSpec((1, SW_HEADS, BLOCK, 2 * BLOCK), lambda v: (v, 0, 0, 0)),
        out_shape=jax.ShapeDtypeStruct((2, SW_HEADS, BLOCK, 2 * BLOCK), F32),
        name="t5_bias",
    )(rel_bias.astype(F32), _bucket_table())


def _sw_kernel(sink_ref, q_ref, kp_ref, kc_ref, vp_ref, vc_ref, bias_ref, o_ref, p_ref, st_ref):
    s = pl.program_id(0)

    @pl.when(s == 0)
    def _():
        p_ref[...] = jnp.zeros_like(p_ref)
        st_ref[...] = jnp.zeros_like(st_ref)

    lane = lax.broadcasted_iota(jnp.int32, (BLOCK, LANES), 1)
    ones = jnp.ones((2 * BLOCK, LANES), BF16)
    scale = jnp.asarray(SCALE, BF16)
    pairs_per_kv = SW_HEADS // SW_KV_HEADS // 2
    dims = (((1,), (1,)), ((), ()))

    def swap_halves(t):
        return jnp.concatenate([t[:, HEAD_DIM:], t[:, :HEAD_DIM]], axis=1)

    def on_side(kv):
        return (lane >= HEAD_DIM) if kv % 2 else (lane < HEAD_DIM)

    def kv_cols(kv):
        return slice((kv // 2) * LANES, (kv // 2 + 1) * LANES)

    def stages(cur):
        old = 1 - cur
        k = jnp.concatenate([kp_ref[...], kc_ref[...]], axis=0)
        v = jnp.concatenate([vp_ref[...], vc_ref[...]], axis=0)
        logits, heads = [], []
        for kv in range(SW_KV_HEADS):
            side = kv % 2
            same, other, hs, ho = [], [], [], []
            for pp in range(pairs_per_kv):
                p = kv * pairs_per_kv + pp
                q_pair = q_ref[:, p * LANES:(p + 1) * LANES] * scale
                zero = jnp.zeros_like(q_pair)
                same.append(jnp.where(on_side(kv), q_pair, zero))
                other.append(jnp.where(on_side(kv), swap_halves(q_pair), zero))
                hs.append(2 * p + side)
                ho.append(2 * p + 1 - side)
            lhs = jnp.concatenate(same + other, axis=0)
            logits.append(lax.dot_general(lhs, k[:, kv_cols(kv)], dims,
                                          preferred_element_type=F32))
            heads.append(hs + ho)
        results = []
        for kv in range(SW_KV_HEADS):
            v_pair = v[:, kv_cols(kv)]
            half = pairs_per_kv * BLOCK
            r_same = jnp.dot(p_ref[old, kv, :half, :], jnp.concatenate([v_pair, ones], axis=1),
                             preferred_element_type=F32)
            r_other = jnp.dot(p_ref[old, kv, half:, :],
                              jnp.concatenate([swap_halves(v_pair), ones], axis=1),
                              preferred_element_type=F32)
            results.append((r_same, r_other))

        for kv in range(SW_KV_HEADS):
            maxes = []
            for i, h in enumerate(heads[kv]):
                lg = logits[kv][i * BLOCK:(i + 1) * BLOCK, :] + bias_ref[0, h]
                m = jnp.maximum(jnp.max(lg, axis=-1, keepdims=True), sink_ref[h])
                p_ref[cur, kv, i * BLOCK:(i + 1) * BLOCK, :] = jnp.exp(lg - m).astype(BF16)
                maxes.append(m)
            for pp in range(pairs_per_kv):
                h_same, h_other = heads[kv][pp], heads[kv][pairs_per_kv + pp]
                m = jnp.where(on_side(kv), maxes[pp], maxes[pairs_per_kv + pp])
                sink = jnp.where(on_side(kv), sink_ref[h_same], sink_ref[h_other])
                st_ref[cur, kv * pairs_per_kv + pp] = jnp.exp(sink - m)

        for kv in range(SW_KV_HEADS):
            r_same, r_other = results[kv]
            for pp in range(pairs_per_kv):
                p = kv * pairs_per_kv + pp
                rows = slice(pp * BLOCK, (pp + 1) * BLOCK)
                num = jnp.where(on_side(kv), r_same[rows, :LANES], r_other[rows, :LANES])
                den = jnp.where(on_side(kv), r_same[rows, LANES:], r_other[rows, LANES:])
                o_ref[:, p * LANES:(p + 1) * LANES] = (num / (den + st_ref[old, p])).astype(BF16)

    @pl.when(s % 2 == 0)
    def _():
        stages(0)

    @pl.when(s % 2 == 1)
    def _():
        stages(1)


def _sw_attention(proj, sinks, bias):
    kq = OFF_Q_SW // SW_WIDTH
    kk = OFF_K_SW // SW_KV_WIDTH
    kv = OFF_V_SW // SW_KV_WIDTH
    blocks = SEQ // BLOCK
    nxt = lambda s: jnp.minimum(s, blocks - 1)
    cur = lambda s: jnp.maximum(s - 1, 0)
    prev = lambda n: jnp.maximum(n - 1, 0)
    return pl.pallas_call(
        _sw_kernel,
        grid=(blocks + 1,),
        in_specs=[
            pl.BlockSpec(memory_space=pltpu.SMEM),
            pl.BlockSpec((BLOCK, SW_WIDTH), lambda s: (nxt(s), kq)),
            pl.BlockSpec((BLOCK, SW_KV_WIDTH), lambda s: (prev(nxt(s)), kk)),
            pl.BlockSpec((BLOCK, SW_KV_WIDTH), lambda s: (nxt(s), kk)),
            pl.BlockSpec((BLOCK, SW_KV_WIDTH), lambda s: (prev(cur(s)), kv)),
            pl.BlockSpec((BLOCK, SW_KV_WIDTH), lambda s: (cur(s), kv)),
            pl.BlockSpec((1, SW_HEADS, BLOCK, 2 * BLOCK),
                         lambda s: (jnp.minimum(nxt(s), 1), 0, 0, 0)),
        ],
        out_specs=pl.BlockSpec((BLOCK, SW_WIDTH), lambda s: (cur(s), 0)),
        out_shape=jax.ShapeDtypeStruct((SEQ, SW_WIDTH), BF16),
        scratch_shapes=[
            pltpu.VMEM((2, SW_KV_HEADS, SW_HEADS // SW_KV_HEADS * BLOCK, 2 * BLOCK), BF16),
            pltpu.VMEM((2, SW_HEADS // 2, BLOCK, LANES), F32),
        ],
        compiler_params=pltpu.CompilerParams(
            dimension_semantics=("arbitrary",), vmem_limit_bytes=VMEM_LIMIT),
        name="sw_attention",
    )(sinks.astype(F32), proj, proj, proj, proj, proj, bias)


def _rms(x, g):
    ms = jnp.mean(x * x, axis=-1, keepdims=True)
    return (x * lax.rsqrt(ms + RMS_EPS)) * g


def _silu(z):
    return z * (1.0 / (1.0 + jnp.exp(-z)))


def _out_kernel(osb_ref, osw_ref, zsb_ref, zswa_ref, zswb_ref, gsb_ref, gsw_ref, gpost_ref,
                w_ref, x_ref, o_ref):
    chunk = OUT_TM // OUT_CHUNKS
    ys = []
    for c in range(OUT_CHUNKS):
        rows = pl.ds(c * chunk, chunk)
        a_sb = _rms(osb_ref[rows, :].astype(F32), gsb_ref[...]) * zsb_ref[rows, :].astype(F32)
        g_sw = jnp.concatenate([zswa_ref[rows, :], zswb_ref[rows, :]], axis=1).astype(F32)
        a_sw = _rms(osw_ref[rows, :].astype(F32), gsw_ref[...]) * g_sw
        y = jnp.dot(a_sb.astype(BF16), w_ref[:SB_WIDTH, :], preferred_element_type=F32)
        ys.append(y + jnp.dot(a_sw.astype(BF16), w_ref[SB_WIDTH:, :],
                              preferred_element_type=F32))
    for c in range(OUT_CHUNKS):
        rows = pl.ds(c * chunk, chunk)
        o_ref[rows, :] = x_ref[rows, :] + _rms(ys[c], gpost_ref[...])


def _outproj(o_sb, o_sw, proj, gn_sb, gn_sw, norm_post, w_out_bf, x2):
    half = SW_WIDTH // 2
    return pl.pallas_call(
        _out_kernel,
        grid=(SEQ // OUT_TM,),
        in_specs=[
            pl.BlockSpec((OUT_TM, SB_WIDTH), lambda i: (i, 0)),
            pl.BlockSpec((OUT_TM, SW_WIDTH), lambda i: (i, 0)),
            pl.BlockSpec((OUT_TM, SB_WIDTH), lambda i: (i, OFF_Z_SB // SB_WIDTH)),
            pl.BlockSpec((OUT_TM, half), lambda i: (i, OFF_Z_SW // half)),
            pl.BlockSpec((OUT_TM, half), lambda i: (i, OFF_Z_SW // half + 1)),
            pl.BlockSpec((1, SB_WIDTH), lambda i: (0, 0)),
            pl.BlockSpec((1, SW_WIDTH), lambda i: (0, 0)),
            pl.BlockSpec((1, D_MODEL), lambda i: (0, 0)),
            pl.BlockSpec((D_MIX, D_MODEL), lambda i: (0, 0), pipeline_mode=pl.Buffered(1)),
            pl.BlockSpec((OUT_TM, D_MODEL), lambda i: (i, 0)),
        ],
        out_specs=pl.BlockSpec((OUT_TM, D_MODEL), lambda i: (i, 0)),
        out_shape=jax.ShapeDtypeStruct((SEQ, D_MODEL), F32),
        compiler_params=pltpu.CompilerParams(
            dimension_semantics=("arbitrary",), vmem_limit_bytes=VMEM_LIMIT),
        name="outproj",
    )(o_sb, o_sw, proj, proj, proj, gn_sb, gn_sw, norm_post, w_out_bf, x2)


def kernel(x, w_in, w_out, norm_pre, norm_post, gn_sb, gn_sw, sinks, rel_bias):
    assert x.shape == (1, SEQ, D_MODEL) and w_in.shape == (1, D_MODEL, D_IN_PROJ)
    assert OFF_Z_SW % (SW_WIDTH // 2) == 0 and OFF_Q_SW % SW_WIDTH == 0
    x2 = x.reshape(SEQ, D_MODEL)
    proj = _inproj(x2, norm_pre.astype(F32), w_in[0].astype(BF16))
    o_sb = _sb_attention(proj)
    bias = _bias_table(rel_bias)
    o_sw = _sw_attention(proj, sinks[0], bias)
    out = _outproj(o_sb, o_sw, proj, gn_sb.astype(F32), gn_sw.astype(F32), norm_post.astype(F32),
                   w_out[0].astype(BF16), x2)
    return out.reshape(1, SEQ, D_MODEL)
```

```python
import math

import numpy as np
import jax
import jax.numpy as jnp
from jax import lax
from jax.experimental import pallas as pl
from jax.experimental.pallas import tpu as pltpu

D_MODEL = 2048
SEQ = 16384
HEAD_DIM = 64
SB_HEADS = 16
SW_HEADS = 16
SW_KV_HEADS = 4
SB_WIDTH = SB_HEADS * HEAD_DIM
SW_WIDTH = SW_HEADS * HEAD_DIM
SW_KV_WIDTH = SW_KV_HEADS * HEAD_DIM
D_MIX = SB_WIDTH + SW_WIDTH
D_IN_PROJ = 4 * SB_WIDTH + 2 * SW_WIDTH + 2 * SW_KV_WIDTH
WINDOW = 128
BLOCK = 128
N_BUCKETS = 32
MAX_DISTANCE = 128
RMS_EPS = 1e-6
SCALE = HEAD_DIM ** -0.5

OFF_Q_SB = 0
OFF_K_SB = SB_WIDTH
OFF_V_SB = 2 * SB_WIDTH
OFF_Z_SB = 3 * SB_WIDTH
OFF_Q_SW = 4 * SB_WIDTH
OFF_K_SW = OFF_Q_SW + SW_WIDTH
OFF_V_SW = OFF_K_SW + SW_KV_WIDTH
OFF_Z_SW = OFF_V_SW + SW_KV_WIDTH

LANES = 128
VMEM_LIMIT = 56 * 1024 * 1024

PROJ_TM = 512
PROJ_TN = 3328
SB_T = 256
SB_H = SB_T // 2
SB_UNITS = 1
OUT_TM = 512
OUT_CHUNKS = 4
NEG_BIG = -1e30
SB_STOP = -110.0

F32 = jnp.float32
BF16 = jnp.bfloat16


def _inproj_kernel(x_ref, g_ref, w_ref, o_ref):
    x = x_ref[...]
    ms = jnp.mean(x * x, axis=-1, keepdims=True)
    xn = ((x * lax.rsqrt(ms + RMS_EPS)) * g_ref[...]).astype(BF16)
    gates = ((OFF_Z_SB, OFF_Z_SB + SB_WIDTH), (OFF_Z_SW, OFF_Z_SW + SW_WIDTH))
    for j in reversed(range(D_IN_PROJ // PROJ_TN)):
        lo, hi = j * PROJ_TN, (j + 1) * PROJ_TN
        res = jnp.dot(xn, w_ref[:, lo:hi], preferred_element_type=F32)
        cuts = sorted({lo, hi} | {min(max(c, lo), hi) for g in gates for c in g})
        for a, b in zip(cuts[:-1], cuts[1:]):
            part = res[:, a - lo:b - lo]
            if any(g0 <= a and b <= g1 for g0, g1 in gates):
                part = _silu(part)
            o_ref[:, a:b] = part.astype(BF16)


def _inproj(x2, g, w_bf):
    return pl.pallas_call(
        _inproj_kernel,
        grid=(SEQ // PROJ_TM,),
        in_specs=[
            pl.BlockSpec((PROJ_TM, D_MODEL), lambda i: (i, 0)),
            pl.BlockSpec((1, D_MODEL), lambda i: (0, 0)),
            pl.BlockSpec((D_MODEL, D_IN_PROJ), lambda i: (0, 0), pipeline_mode=pl.Buffered(1)),
        ],
        out_specs=pl.BlockSpec((PROJ_TM, D_IN_PROJ), lambda i: (i, 0)),
        out_shape=jax.ShapeDtypeStruct((SEQ, D_IN_PROJ), BF16),
        compiler_params=pltpu.CompilerParams(
            dimension_semantics=("arbitrary",), vmem_limit_bytes=VMEM_LIMIT),
        name="inproj",
    )(x2, g, w_bf)


def _sb_kernel(q_ref, k_ref, v_ref, tri_ref, o_ref,
               hi_ref, zl_ref, zr_ref, zp_ref, w_ref, carry_w_ref, go_ref,
               acc_ref, carry_ref):
    last = SEQ // SB_T - 1
    for ref in (hi_ref, zl_ref, zr_ref, zp_ref, w_ref, carry_w_ref):
        ref[...] = jnp.zeros_like(ref)

    lane = lax.broadcasted_iota(jnp.int32, (SB_T, LANES), 1)

    def cols(u):
        return slice(u * LANES, (u + 1) * LANES)

    def stack_heads(tile, u):
        start = pl.multiple_of(tile * SB_T, SB_T)
        q = q_ref[pl.ds(start, SB_T), cols(u)] * jnp.asarray(SCALE, BF16)
        zero = jnp.zeros_like(q)
        return jnp.concatenate([jnp.where(lane < HEAD_DIM, q, zero),
                                jnp.where(lane >= HEAD_DIM, q, zero)], axis=0)

    tri = tri_ref[...]
    r_h = lax.broadcasted_iota(jnp.int32, (SB_H, SB_H), 0)
    c_h = lax.broadcasted_iota(jnp.int32, (SB_H, SB_H), 1)
    before_h = c_h < r_h
    zeros_h = jnp.zeros((SB_H, SB_H), BF16)

    def qk(qs, kb, u):
        start = pl.multiple_of(kb * SB_T, SB_T)
        kblk = k_ref[pl.ds(start, SB_T), cols(u)]
        return lax.dot_general(qs, kblk, (((1,), (1,)), ((), ())), preferred_element_type=F32)

    def softplus(z):
        neg_abs = lax.bitcast_convert_type(
            lax.bitcast_convert_type(z, jnp.int32) | jnp.int32(-2 ** 31), F32)
        return (jnp.maximum(z, 0.0) + jnp.log(1.0 + jnp.exp(neg_abs))).astype(BF16)

    def cut(z):
        return jnp.where(before_h, z, NEG_BIG)

    def quarters(a):
        left = a[:, :SB_H]
        right = jnp.concatenate([a[SB_H:SB_T, SB_H:], a[SB_T + SB_H:, SB_H:]], axis=0)
        return left, right

    def unquarter(left, right):
        right_full = jnp.concatenate([zeros_h, right[:SB_H], zeros_h, right[SB_H:]], axis=0)
        return jnp.concatenate([left, right_full], axis=1)

    def cumsum(hi):
        return jnp.dot(hi, tri, preferred_element_type=F32)

    def values(kb, w, u, v_valid=None):
        start = pl.multiple_of(kb * SB_T, SB_T)
        vblk = v_ref[pl.ds(start, SB_T), cols(u)]
        if v_valid is not None:
            vblk = jnp.where(v_valid, vblk, jnp.zeros_like(vblk))
        return jnp.dot(w, vblk, preferred_element_type=F32)

    def stages(cur, n, t):
        old = 1 - cur
        units = range(SB_UNITS)
        qs_n = [stack_heads(n, u) for u in units]
        z_d = [qk(qs_n[u], n, u) for u in units]
        z_p = [qk(qs_n[u], jnp.maximum(n - 1, 0), u) for u in units]
        cs_d = [cumsum(hi_ref[u, old, 0]) for u in units]
        cs_p = [cumsum(hi_ref[u, old, 1]) for u in units]
        for u in units:
            acc_ref[u] = (values(t, w_ref[u, old, 0], u)
                          + values(jnp.maximum(t - 1, 0), w_ref[u, old, 1], u, v_valid=t > 0))
            carry_t = carry_w_ref[u, old]
            carry_ref[u] = carry_t
            go_ref[u] = (jnp.max(carry_t) >= SB_STOP).astype(jnp.int32)

        for u in units:
            zl, zr = quarters(z_d[u])
            zl = jnp.concatenate([cut(zl[:SB_H]), zl[SB_H:SB_T],
                                  cut(zl[SB_T:SB_T + SB_H]), zl[SB_T + SB_H:]], axis=0)
            zr = jnp.concatenate([cut(zr[:SB_H]), cut(zr[SB_H:])], axis=0)
            hi_ref[u, cur, 0] = unquarter(softplus(zl), softplus(zr))
            hi_ref[u, cur, 1] = softplus(z_p[u])
            zl_ref[u, cur] = zl
            zr_ref[u, cur] = zr
            zp_ref[u, cur] = z_p[u]

            cs_l, cs_r = quarters(cs_d[u])
            w_l = jnp.exp(zl_ref[u, old] + cs_l).astype(BF16)
            w_r = jnp.exp(zr_ref[u, old] + cs_r).astype(BF16)
            carry_d = cs_d[u][:, 0:1]
            w_p = jnp.exp(zp_ref[u, old] + cs_p[u] + carry_d).astype(BF16)
            w_ref[u, cur, 0] = unquarter(w_l, w_r)
            w_ref[u, cur, 1] = w_p
            carry_w_ref[u, cur] = carry_d + cs_p[u][:, 0:1]

    def block(qs, kb, u):
        z = qk(qs, kb, u)
        cs = cumsum(softplus(z))
        carry = carry_ref[u]
        w = jnp.exp(z + cs + carry).astype(BF16)
        acc_ref[u] += values(kb, w, u)
        carry_ref[u] = carry + cs[:, 0:1]

    def cond(state):
        kb, go = state
        return jnp.logical_and(kb >= 0, go > 0)

    def two_steps(i, _):
        for half in range(2):
            s = 2 * i + half
            t = jnp.maximum(s - 2, 0)
            stages(half, jnp.minimum(s, last), t)
            for u in range(SB_UNITS):
                def body(state, t=t, u=u):
                    kb, _ = state
                    block(stack_heads(t, u), kb, u)
                    go = (jnp.max(carry_ref[u]) >= SB_STOP).astype(jnp.int32)
                    return kb - 1, go

                lax.while_loop(cond, body, (t - 2, go_ref[u]))
                rows = pl.ds(pl.multiple_of(t * SB_T, SB_T), SB_T)
                o_ref[rows, cols(u)] = jnp.where(
                    lane < HEAD_DIM, acc_ref[u, :SB_T, :], acc_ref[u, SB_T:, :]).astype(BF16)
        return 0

    lax.fori_loop(0, (last + 1 + 2) // 2, two_steps, 0)


def _sb_tri():
    j = np.arange(SB_T)[:, None]
    s = np.arange(SB_T)[None, :]
    return jnp.asarray(np.where(j >= s, -1.0, 0.0).astype(np.float32), BF16)


def _sb_attention(proj):
    wide = SB_UNITS * LANES
    tiles = SEQ // SB_T
    U = SB_UNITS
    return pl.pallas_call(
        _sb_kernel,
        grid=(SB_WIDTH // wide,),
        in_specs=[
            pl.BlockSpec((SEQ, wide), lambda p: (0, OFF_Q_SB // wide + p)),
            pl.BlockSpec((SEQ, wide), lambda p: (0, OFF_K_SB // wide + p)),
            pl.BlockSpec((SEQ, wide), lambda p: (0, OFF_V_SB // wide + p)),
            pl.BlockSpec((SB_T, SB_T), lambda p: (0, 0)),
        ],
        out_specs=pl.BlockSpec((SEQ, wide), lambda p: (0, p)),
        out_shape=jax.ShapeDtypeStruct((SEQ, SB_WIDTH), BF16),
        scratch_shapes=[
            pltpu.VMEM((U, 2, 2, 2 * SB_T, SB_T), BF16),
            pltpu.VMEM((U, 2, 2 * SB_T, SB_H), F32),
            pltpu.VMEM((U, 2, SB_T, SB_H), F32),
            pltpu.VMEM((U, 2, 2 * SB_T, SB_T), F32),
            pltpu.VMEM((U, 2, 2, 2 * SB_T, SB_T), BF16),
            pltpu.VMEM((U, 2, 2 * SB_T, 1), F32),
            pltpu.SMEM((U,), jnp.int32),
            pltpu.VMEM((U, 2 * SB_T, LANES), F32),
            pltpu.VMEM((U, 2 * SB_T, 1), F32),
        ],
        compiler_params=pltpu.CompilerParams(
            dimension_semantics=("arbitrary",), vmem_limit_bytes=VMEM_LIMIT),
        name="sb_attention",
    )(proj, proj, proj, _sb_tri())


def _bucket_table():
    qi = np.arange(BLOCK)[:, None]
    ci = np.arange(2 * BLOCK)[None, :]
    dist = qi + BLOCK - ci
    n = np.maximum(dist, 0)
    max_exact = N_BUCKETS // 2
    nf = np.maximum(n, 1).astype(np.float64)
    val = np.log(nf / max_exact) / math.log(MAX_DISTANCE / max_exact) * (N_BUCKETS - max_exact)
    in_window = (dist >= 0) & (dist < WINDOW)
    frac = val - np.floor(val)
    chk = in_window & (n > max_exact)
    assert np.all((frac[chk] > 1e-3) & (frac[chk] < 1 - 1e-3))
    large = np.minimum(max_exact + val.astype(np.int64), N_BUCKETS - 1)
    bucket = np.where(n < max_exact, n, large)
    later = np.where(in_window, bucket, -1)
    first = np.where(ci >= BLOCK, later, -1)
    return jnp.asarray(np.stack([first, later]).astype(np.int32))


def _bias_kernel(rb_ref, bucket_ref, o_ref):
    b = bucket_ref[0]
    hits = [b == k for k in range(N_BUCKETS)]
    for h in range(SW_HEADS):
        acc = jnp.full(b.shape, NEG_BIG, F32)
        for k in range(N_BUCKETS):
            acc = jnp.where(hits[k], rb_ref[k, h], acc)
        o_ref[0, h] = acc


def _bias_table(rel_bias):
    return pl.pallas_call(
        _bias_kernel,
        grid=(2,),
        in_specs=[
            pl.BlockSpec(memory_space=pltpu.SMEM),
            pl.BlockSpec((1, BLOCK, 2 * BLOCK), lambda v: (v, 0, 0)),
        ],
        out_specs=pl.BlockSpec((1, SW_HEADS, BLOCK, 2 * BLOCK), lambda v: (v, 0, 0, 0)),
        out_shape=jax.ShapeDtypeStruct((2, SW_HEADS, BLOCK, 2 * BLOCK), F32),
        name="t5_bias",
    )(rel_bias.astype(F32), _bucket_table())


def _sw_kernel(sink_ref, q_ref, kp_ref, kc_ref, vp_ref, vc_ref, bias_ref, o_ref, p_ref, st_ref):
    s = pl.program_id(0)

    @pl.when(s == 0)
    def _():
        p_ref[...] = jnp.zeros_like(p_ref)
        st_ref[...] = jnp.zeros_like(st_ref)

    lane = lax.broadcasted_iota(jnp.int32, (BLOCK, LANES), 1)
    ones = jnp.ones((2 * BLOCK, LANES), BF16)
    scale = jnp.asarray(SCALE, BF16)
    pairs_per_kv = SW_HEADS // SW_KV_HEADS // 2
    dims = (((1,), (1,)), ((), ()))

    def swap_halves(t):
        return jnp.concatenate([t[:, HEAD_DIM:], t[:, :HEAD_DIM]], axis=1)

    def on_side(kv):
        return (lane >= HEAD_DIM) if kv % 2 else (lane < HEAD_DIM)

    def kv_cols(kv):
        return slice((kv // 2) * LANES, (kv // 2 + 1) * LANES)

    def stages(cur):
        old = 1 - cur
        k = jnp.concatenate([kp_ref[...], kc_ref[...]], axis=0)
        v = jnp.concatenate([vp_ref[...], vc_ref[...]], axis=0)
        logits, heads = [], []
        for kv in range(SW_KV_HEADS):
            side = kv % 2
            same, other, hs, ho = [], [], [], []
            for pp in range(pairs_per_kv):
                p = kv * pairs_per_kv + pp
                q_pair = q_ref[:, p * LANES:(p + 1) * LANES] * scale
                zero = jnp.zeros_like(q_pair)
                same.append(jnp.where(on_side(kv), q_pair, zero))
                other.append(jnp.where(on_side(kv), swap_halves(q_pair), zero))
                hs.append(2 * p + side)
                ho.append(2 * p + 1 - side)
            lhs = jnp.concatenate(same + other, axis=0)
            logits.append(lax.dot_general(lhs, k[:, kv_cols(kv)], dims,
                                          preferred_element_type=F32))
            heads.append(hs + ho)
        results = []
        for kv in range(SW_KV_HEADS):
            v_pair = v[:, kv_cols(kv)]
            half = pairs_per_kv * BLOCK
            r_same = jnp.dot(p_ref[old, kv, :half, :], jnp.concatenate([v_pair, ones], axis=1),
                             preferred_element_type=F32)
            r_other = jnp.dot(p_ref[old, kv, half:, :],
                              jnp.concatenate([swap_halves(v_pair), ones], axis=1),
                              preferred_element_type=F32)
            results.append((r_same, r_other))

        for kv in range(SW_KV_HEADS):
            maxes = []
            for i, h in enumerate(heads[kv]):
                lg = logits[kv][i * BLOCK:(i + 1) * BLOCK, :] + bias_ref[0, h]
                m = jnp.maximum(jnp.max(lg, axis=-1, keepdims=True), sink_ref[h])
                p_ref[cur, kv, i * BLOCK:(i + 1) * BLOCK, :] = jnp.exp(lg - m).astype(BF16)
                maxes.append(m)
            for pp in range(pairs_per_kv):
                h_same, h_other = heads[kv][pp], heads[kv][pairs_per_kv + pp]
                m = jnp.where(on_side(kv), maxes[pp], maxes[pairs_per_kv + pp])
                sink = jnp.where(on_side(kv), sink_ref[h_same], sink_ref[h_other])
                st_ref[cur, kv * pairs_per_kv + pp] = jnp.exp(sink - m)

        for kv in range(SW_KV_HEADS):
            r_same, r_other = results[kv]
            for pp in range(pairs_per_kv):
                p = kv * pairs_per_kv + pp
                rows = slice(pp * BLOCK, (pp + 1) * BLOCK)
                num = jnp.where(on_side(kv), r_same[rows, :LANES], r_other[rows, :LANES])
                den = jnp.where(on_side(kv), r_same[rows, LANES:], r_other[rows, LANES:])
                o_ref[:, p * LANES:(p + 1) * LANES] = (num / (den + st_ref[old, p])).astype(BF16)

    @pl.when(s % 2 == 0)
    def _():
        stages(0)

    @pl.when(s % 2 == 1)
    def _():
        stages(1)


def _sw_attention(proj, sinks, bias):
    kq = OFF_Q_SW // SW_WIDTH
    kk = OFF_K_SW // SW_KV_WIDTH
    kv = OFF_V_SW // SW_KV_WIDTH
    blocks = SEQ // BLOCK
    nxt = lambda s: jnp.minimum(s, blocks - 1)
    cur = lambda s: jnp.maximum(s - 1, 0)
    prev = lambda n: jnp.maximum(n - 1, 0)
    return pl.pallas_call(
        _sw_kernel,
        grid=(blocks + 1,),
        in_specs=[
            pl.BlockSpec(memory_space=pltpu.SMEM),
            pl.BlockSpec((BLOCK, SW_WIDTH), lambda s: (nxt(s), kq)),
            pl.BlockSpec((BLOCK, SW_KV_WIDTH), lambda s: (prev(nxt(s)), kk)),
            pl.BlockSpec((BLOCK, SW_KV_WIDTH), lambda s: (nxt(s), kk)),
            pl.BlockSpec((BLOCK, SW_KV_WIDTH), lambda s: (prev(cur(s)), kv)),
            pl.BlockSpec((BLOCK, SW_KV_WIDTH), lambda s: (cur(s), kv)),
            pl.BlockSpec((1, SW_HEADS, BLOCK, 2 * BLOCK),
                         lambda s: (jnp.minimum(nxt(s), 1), 0, 0, 0)),
        ],
        out_specs=pl.BlockSpec((BLOCK, SW_WIDTH), lambda s: (cur(s), 0)),
        out_shape=jax.ShapeDtypeStruct((SEQ, SW_WIDTH), BF16),
        scratch_shapes=[
            pltpu.VMEM((2, SW_KV_HEADS, SW_HEADS // SW_KV_HEADS * BLOCK, 2 * BLOCK), BF16),
            pltpu.VMEM((2, SW_HEADS // 2, BLOCK, LANES), F32),
        ],
        compiler_params=pltpu.CompilerParams(
            dimension_semantics=("arbitrary",), vmem_limit_bytes=VMEM_LIMIT),
        name="sw_attention",
    )(sinks.astype(F32), proj, proj, proj, proj, proj, bias)


def _rms(x, g):
    ms = jnp.mean(x * x, axis=-1, keepdims=True)
    return (x * lax.rsqrt(ms + RMS_EPS)) * g


def _silu(z):
    return z * (1.0 / (1.0 + jnp.exp(-z)))


def _out_kernel(osb_ref, osw_ref, zsb_ref, zswa_ref, zswb_ref, gsb_ref, gsw_ref, gpost_ref,
                w_ref, x_ref, o_ref):
    chunk = OUT_TM // OUT_CHUNKS
    ys = []
    for c in range(OUT_CHUNKS):
        rows = pl.ds(c * chunk, chunk)
        a_sb = _rms(osb_ref[rows, :].astype(F32), gsb_ref[...]) * zsb_ref[rows, :].astype(F32)
        g_sw = jnp.concatenate([zswa_ref[rows, :], zswb_ref[rows, :]], axis=1).astype(F32)
        a_sw = _rms(osw_ref[rows, :].astype(F32), gsw_ref[...]) * g_sw
        y = jnp.dot(a_sb.astype(BF16), w_ref[:SB_WIDTH, :], preferred_element_type=F32)
        ys.append(y + jnp.dot(a_sw.astype(BF16), w_ref[SB_WIDTH:, :],
                              preferred_element_type=F32))
    for c in range(OUT_CHUNKS):
        rows = pl.ds(c * chunk, chunk)
        o_ref[rows, :] = x_ref[rows, :] + _rms(ys[c], gpost_ref[...])


def _outproj(o_sb, o_sw, proj, gn_sb, gn_sw, norm_post, w_out_bf, x2):
    half = SW_WIDTH // 2
    return pl.pallas_call(
        _out_kernel,
        grid=(SEQ // OUT_TM,),
        in_specs=[
            pl.BlockSpec((OUT_TM, SB_WIDTH), lambda i: (i, 0)),
            pl.BlockSpec((OUT_TM, SW_WIDTH), lambda i: (i, 0)),
            pl.BlockSpec((OUT_TM, SB_WIDTH), lambda i: (i, OFF_Z_SB // SB_WIDTH)),
            pl.BlockSpec((OUT_TM, half), lambda i: (i, OFF_Z_SW // half)),
            pl.BlockSpec((OUT_TM, half), lambda i: (i, OFF_Z_SW // half + 1)),
            pl.BlockSpec((1, SB_WIDTH), lambda i: (0, 0)),
            pl.BlockSpec((1, SW_WIDTH), lambda i: (0, 0)),
            pl.BlockSpec((1, D_MODEL), lambda i: (0, 0)),
            pl.BlockSpec((D_MIX, D_MODEL), lambda i: (0, 0), pipeline_mode=pl.Buffered(1)),
            pl.BlockSpec((OUT_TM, D_MODEL), lambda i: (i, 0)),
        ],
        out_specs=pl.BlockSpec((OUT_TM, D_MODEL), lambda i: (i, 0)),
        out_shape=jax.ShapeDtypeStruct((SEQ, D_MODEL), F32),
        compiler_params=pltpu.CompilerParams(
            dimension_semantics=("arbitrary",), vmem_limit_bytes=VMEM_LIMIT),
        name="outproj",
    )(o_sb, o_sw, proj, proj, proj, gn_sb, gn_sw, norm_post, w_out_bf, x2)


def kernel(x, w_in, w_out, norm_pre, norm_post, gn_sb, gn_sw, sinks, rel_bias):
    assert x.shape == (1, SEQ, D_MODEL) and w_in.shape == (1, D_MODEL, D_IN_PROJ)
    assert OFF_Z_SW % (SW_WIDTH // 2) == 0 and OFF_Q_SW % SW_WIDTH == 0
    x2 = x.reshape(SEQ, D_MODEL)
    proj = _inproj(x2, norm_pre.astype(F32), w_in[0].astype(BF16))
    o_sb = _sb_attention(proj)
    bias = _bias_table(rel_bias)
    o_sw = _sw_attention(proj, sinks[0], bias)
    out = _outproj(o_sb, o_sw, proj, gn_sb.astype(F32), gn_sw.astype(F32), norm_post.astype(F32),
                   w_out[0].astype(BF16), x2)
    return out.reshape(1, SEQ, D_MODEL)
```

```python
import math

import numpy as np
import jax
import jax.numpy as jnp
from jax import lax
from jax.experimental import pallas as pl
from jax.experimental.pallas import tpu as pltpu

D_MODEL = 2048
SEQ = 16384
HEAD_DIM = 64
SB_HEADS = 16
SW_HEADS = 16
SW_KV_HEADS = 4
SB_WIDTH = SB_HEADS * HEAD_DIM
SW_WIDTH = SW_HEADS * HEAD_DIM
SW_KV_WIDTH = SW_KV_HEADS * HEAD_DIM
D_MIX = SB_WIDTH + SW_WIDTH
D_IN_PROJ = 4 * SB_WIDTH + 2 * SW_WIDTH + 2 * SW_KV_WIDTH
WINDOW = 128
BLOCK = 128
N_BUCKETS = 32
MAX_DISTANCE = 128
RMS_EPS = 1e-6
SCALE = HEAD_DIM ** -0.5

OFF_Q_SB = 0
OFF_K_SB = SB_WIDTH
OFF_V_SB = 2 * SB_WIDTH
OFF_Z_SB = 3 * SB_WIDTH
OFF_Q_SW = 4 * SB_WIDTH
OFF_K_SW = OFF_Q_SW + SW_WIDTH
OFF_V_SW = OFF_K_SW + SW_KV_WIDTH
OFF_Z_SW = OFF_V_SW + SW_KV_WIDTH

LANES = 128
VMEM_LIMIT = 56 * 1024 * 1024

PROJ_TM = 512
PROJ_TN = 3328
CAST_ROWS = 32
CAST_SLOTS = 4
SB_T = 256
SB_H = SB_T // 2
SB_UNITS = 1
OUT_TM = 512
OUT_CHUNKS = 4
NEG_BIG = -1e30
SB_STOP = -110.0

F32 = jnp.float32
BF16 = jnp.bfloat16


def _cast_weights(w_hbm, w_ref, stage_ref, sem):
    slots, rows = stage_ref.shape[0], stage_ref.shape[1]
    n_chunks = w_hbm.shape[0] // rows
    ahead = slots - 1

    def chunk_copy(c):
        slot = c % slots
        src = w_hbm.at[pl.ds(pl.multiple_of(c * rows, rows), rows), :]
        return pltpu.make_async_copy(src, stage_ref.at[slot], sem.at[slot])

    for c in range(ahead):
        chunk_copy(c).start()

    def body(c, _):
        @pl.when(c + ahead < n_chunks)
        def _():
            chunk_copy(c + ahead).start()

        chunk_copy(c).wait()
        w_ref[pl.ds(pl.multiple_of(c * rows, rows), rows), :] = stage_ref[c % slots].astype(BF16)
        return 0

    lax.fori_loop(0, n_chunks, body, 0)


def _inproj_kernel(x_ref, g_ref, w_hbm, o_ref, w_ref, stage_ref, sem):
    @pl.when(pl.program_id(0) == 0)
    def _():
        _cast_weights(w_hbm, w_ref, stage_ref, sem)

    x = x_ref[...]
    ms = jnp.mean(x * x, axis=-1, keepdims=True)
    xn = ((x * lax.rsqrt(ms + RMS_EPS)) * g_ref[...]).astype(BF16)
    gates = ((OFF_Z_SB, OFF_Z_SB + SB_WIDTH), (OFF_Z_SW, OFF_Z_SW + SW_WIDTH))
    for j in reversed(range(D_IN_PROJ // PROJ_TN)):
        lo, hi = j * PROJ_TN, (j + 1) * PROJ_TN
        res = jnp.dot(xn, w_ref[:, lo:hi], preferred_element_type=F32)
        cuts = sorted({lo, hi} | {min(max(c, lo), hi) for g in gates for c in g})
        for a, b in zip(cuts[:-1], cuts[1:]):
            part = res[:, a - lo:b - lo]
            if any(g0 <= a and b <= g1 for g0, g1 in gates):
                part = _silu(part)
            o_ref[:, a:b] = part.astype(BF16)


def _inproj(x2, g, w):
    return pl.pallas_call(
        _inproj_kernel,
        grid=(SEQ // PROJ_TM,),
        in_specs=[
            pl.BlockSpec((PROJ_TM, D_MODEL), lambda i: (i, 0)),
            pl.BlockSpec((1, D_MODEL), lambda i: (0, 0)),
            pl.BlockSpec(memory_space=pl.ANY),
        ],
        out_specs=pl.BlockSpec((PROJ_TM, D_IN_PROJ), lambda i: (i, 0)),
        out_shape=jax.ShapeDtypeStruct((SEQ, D_IN_PROJ), BF16),
        scratch_shapes=[
            pltpu.VMEM((D_MODEL, D_IN_PROJ), BF16),
            pltpu.VMEM((CAST_SLOTS, CAST_ROWS, D_IN_PROJ), F32),
            pltpu.SemaphoreType.DMA((CAST_SLOTS,)),
        ],
        compiler_params=pltpu.CompilerParams(
            dimension_semantics=("arbitrary",), vmem_limit_bytes=VMEM_LIMIT),
        name="inproj",
    )(x2, g, w)


def _sb_kernel(q_ref, k_ref, v_ref, tri_ref, o_ref,
               hi_ref, zl_ref, zr_ref, zp_ref, w_ref, carry_w_ref, go_ref,
               acc_ref, carry_ref):
    last = SEQ // SB_T - 1
    for ref in (hi_ref, zl_ref, zr_ref, zp_ref, w_ref, carry_w_ref):
        ref[...] = jnp.zeros_like(ref)

    lane = lax.broadcasted_iota(jnp.int32, (SB_T, LANES), 1)

    def cols(u):
        return slice(u * LANES, (u + 1) * LANES)

    def stack_heads(tile, u):
        start = pl.multiple_of(tile * SB_T, SB_T)
        q = q_ref[pl.ds(start, SB_T), cols(u)] * jnp.asarray(SCALE, BF16)
        zero = jnp.zeros_like(q)
        return jnp.concatenate([jnp.where(lane < HEAD_DIM, q, zero),
                                jnp.where(lane >= HEAD_DIM, q, zero)], axis=0)

    tri = tri_ref[...]
    r_h = lax.broadcasted_iota(jnp.int32, (SB_H, SB_H), 0)
    c_h = lax.broadcasted_iota(jnp.int32, (SB_H, SB_H), 1)
    before_h = c_h < r_h
    zeros_h = jnp.zeros((SB_H, SB_H), BF16)

    def qk(qs, kb, u):
        start = pl.multiple_of(kb * SB_T, SB_T)
        kblk = k_ref[pl.ds(start, SB_T), cols(u)]
        return lax.dot_general(qs, kblk, (((1,), (1,)), ((), ())), preferred_element_type=F32)

    def softplus(z):
        neg_abs = lax.bitcast_convert_type(
            lax.bitcast_convert_type(z, jnp.int32) | jnp.int32(-2 ** 31), F32)
        return (jnp.maximum(z, 0.0) + jnp.log(1.0 + jnp.exp(neg_abs))).astype(BF16)

    def cut(z):
        return jnp.where(before_h, z, NEG_BIG)

    def quarters(a):
        left = a[:, :SB_H]
        right = jnp.concatenate([a[SB_H:SB_T, SB_H:], a[SB_T + SB_H:, SB_H:]], axis=0)
        return left, right

    def unquarter(left, right):
        right_full = jnp.concatenate([zeros_h, right[:SB_H], zeros_h, right[SB_H:]], axis=0)
        return jnp.concatenate([left, right_full], axis=1)

    def cumsum(hi):
        return jnp.dot(hi, tri, preferred_element_type=F32)

    def values(kb, w, u, v_valid=None):
        start = pl.multiple_of(kb * SB_T, SB_T)
        vblk = v_ref[pl.ds(start, SB_T), cols(u)]
        if v_valid is not None:
            vblk = jnp.where(v_valid, vblk, jnp.zeros_like(vblk))
        return jnp.dot(w, vblk, preferred_element_type=F32)

    def stages(cur, n, t):
        old = 1 - cur
        units = range(SB_UNITS)
        qs_n = [stack_heads(n, u) for u in units]
        z_d = [qk(qs_n[u], n, u) for u in units]
        z_p = [qk(qs_n[u], jnp.maximum(n - 1, 0), u) for u in units]
        cs_d = [cumsum(hi_ref[u, old, 0]) for u in units]
        cs_p = [cumsum(hi_ref[u, old, 1]) for u in units]
        for u in units:
            acc_ref[u] = (values(t, w_ref[u, old, 0], u)
                          + values(jnp.maximum(t - 1, 0), w_ref[u, old, 1], u, v_valid=t > 0))
            carry_t = carry_w_ref[u, old]
            carry_ref[u] = carry_t
            go_ref[u] = (jnp.max(carry_t) >= SB_STOP).astype(jnp.int32)

        for u in units:
            zl, zr = quarters(z_d[u])
            zl = jnp.concatenate([cut(zl[:SB_H]), zl[SB_H:SB_T],
                                  cut(zl[SB_T:SB_T + SB_H]), zl[SB_T + SB_H:]], axis=0)
            zr = jnp.concatenate([cut(zr[:SB_H]), cut(zr[SB_H:])], axis=0)
            hi_ref[u, cur, 0] = unquarter(softplus(zl), softplus(zr))
            hi_ref[u, cur, 1] = softplus(z_p[u])
            zl_ref[u, cur] = zl
            zr_ref[u, cur] = zr
            zp_ref[u, cur] = z_p[u]

            cs_l, cs_r = quarters(cs_d[u])
            w_l = jnp.exp(zl_ref[u, old] + cs_l).astype(BF16)
            w_r = jnp.exp(zr_ref[u, old] + cs_r).astype(BF16)
            carry_d = cs_d[u][:, 0:1]
            w_p = jnp.exp(zp_ref[u, old] + cs_p[u] + carry_d).astype(BF16)
            w_ref[u, cur, 0] = unquarter(w_l, w_r)
            w_ref[u, cur, 1] = w_p
            carry_w_ref[u, cur] = carry_d + cs_p[u][:, 0:1]

    def block(qs, kb, u):
        z = qk(qs, kb, u)
        cs = cumsum(softplus(z))
        carry = carry_ref[u]
        w = jnp.exp(z + cs + carry).astype(BF16)
        acc_ref[u] += values(kb, w, u)
        carry_ref[u] = carry + cs[:, 0:1]

    def cond(state):
        kb, go = state
        return jnp.logical_and(kb >= 0, go > 0)

    def two_steps(i, _):
        for half in range(2):
            s = 2 * i + half
            t = jnp.maximum(s - 2, 0)
            stages(half, jnp.minimum(s, last), t)
            for u in range(SB_UNITS):
                def body(state, t=t, u=u):
                    kb, _ = state
                    block(stack_heads(t, u), kb, u)
                    go = (jnp.max(carry_ref[u]) >= SB_STOP).astype(jnp.int32)
                    return kb - 1, go

                lax.while_loop(cond, body, (t - 2, go_ref[u]))
                rows = pl.ds(pl.multiple_of(t * SB_T, SB_T), SB_T)
                o_ref[rows, cols(u)] = jnp.where(
                    lane < HEAD_DIM, acc_ref[u, :SB_T, :], acc_ref[u, SB_T:, :]).astype(BF16)
        return 0

    lax.fori_loop(0, (last + 1 + 2) // 2, two_steps, 0)


def _sb_tri():
    j = np.arange(SB_T)[:, None]
    s = np.arange(SB_T)[None, :]
    return jnp.asarray(np.where(j >= s, -1.0, 0.0).astype(np.float32), BF16)


def _sb_attention(proj):
    wide = SB_UNITS * LANES
    tiles = SEQ // SB_T
    U = SB_UNITS
    return pl.pallas_call(
        _sb_kernel,
        grid=(SB_WIDTH // wide,),
        in_specs=[
            pl.BlockSpec((SEQ, wide), lambda p: (0, OFF_Q_SB // wide + p)),
            pl.BlockSpec((SEQ, wide), lambda p: (0, OFF_K_SB // wide + p)),
            pl.BlockSpec((SEQ, wide), lambda p: (0, OFF_V_SB // wide + p)),
            pl.BlockSpec((SB_T, SB_T), lambda p: (0, 0)),
        ],
        out_specs=pl.BlockSpec((SEQ, wide), lambda p: (0, p)),
        out_shape=jax.ShapeDtypeStruct((SEQ, SB_WIDTH), BF16),
        scratch_shapes=[
            pltpu.VMEM((U, 2, 2, 2 * SB_T, SB_T), BF16),
            pltpu.VMEM((U, 2, 2 * SB_T, SB_H), F32),
            pltpu.VMEM((U, 2, SB_T, SB_H), F32),
            pltpu.VMEM((U, 2, 2 * SB_T, SB_T), F32),
            pltpu.VMEM((U, 2, 2, 2 * SB_T, SB_T), BF16),
            pltpu.VMEM((U, 2, 2 * SB_T, 1), F32),
            pltpu.SMEM((U,), jnp.int32),
            pltpu.VMEM((U, 2 * SB_T, LANES), F32),
            pltpu.VMEM((U, 2 * SB_T, 1), F32),
        ],
        compiler_params=pltpu.CompilerParams(
            dimension_semantics=("arbitrary",), vmem_limit_bytes=VMEM_LIMIT),
        name="sb_attention",
    )(proj, proj, proj, _sb_tri())


def _bucket_table():
    qi = np.arange(BLOCK)[:, None]
    ci = np.arange(2 * BLOCK)[None, :]
    dist = qi + BLOCK - ci
    n = np.maximum(dist, 0)
    max_exact = N_BUCKETS // 2
    nf = np.maximum(n, 1).astype(np.float64)
    val = np.log(nf / max_exact) / math.log(MAX_DISTANCE / max_exact) * (N_BUCKETS - max_exact)
    in_window = (dist >= 0) & (dist < WINDOW)
    frac = val - np.floor(val)
    chk = in_window & (n > max_exact)
    assert np.all((frac[chk] > 1e-3) & (frac[chk] < 1 - 1e-3))
    large = np.minimum(max_exact + val.astype(np.int64), N_BUCKETS - 1)
    bucket = np.where(n < max_exact, n, large)
    later = np.where(in_window, bucket, -1)
    first = np.where(ci >= BLOCK, later, -1)
    return jnp.asarray(np.stack([first, later]).astype(np.int32))


def _bias_kernel(rb_ref, bucket_ref, o_ref):
    b = bucket_ref[0]
    hits = [b == k for k in range(N_BUCKETS)]
    for h in range(SW_HEADS):
        acc = jnp.full(b.shape, NEG_BIG, F32)
        for k in range(N_BUCKETS):
            acc = jnp.where(hits[k], rb_ref[k, h], acc)
        o_ref[0, h] = acc


def _bias_table(rel_bias):
    return pl.pallas_call(
        _bias_kernel,
        grid=(2,),
        in_specs=[
            pl.BlockSpec(memory_space=pltpu.SMEM),
            pl.BlockSpec((1, BLOCK, 2 * BLOCK), lambda v: (v, 0, 0)),
        ],
        out_specs=pl.BlockSpec((1, SW_HEADS, BLOCK, 2 * BLOCK), lambda v: (v, 0, 0, 0)),
        out_shape=jax.ShapeDtypeStruct((2, SW_HEADS, BLOCK, 2 * BLOCK), F32),
        name="t5_bias",
    )(rel_bias.astype(F32), _bucket_table())


def _sw_kernel(sink_ref, q_ref, kp_ref, kc_ref, vp_ref, vc_ref, bias_ref, o_ref, p_ref, st_ref):
    s = pl.program_id(0)

    @pl.when(s == 0)
    def _():
        p_ref[...] = jnp.zeros_like(p_ref)
        st_ref[...] = jnp.zeros_like(st_ref)

    lane = lax.broadcasted_iota(jnp.int32, (BLOCK, LANES), 1)
    ones = jnp.ones((2 * BLOCK, LANES), BF16)
    scale = jnp.asarray(SCALE, BF16)
    pairs_per_kv = SW_HEADS // SW_KV_HEADS // 2
    dims = (((1,), (1,)), ((), ()))

    def swap_halves(t):
        return jnp.concatenate([t[:, HEAD_DIM:], t[:, :HEAD_DIM]], axis=1)

    def on_side(kv):
        return (lane >= HEAD_DIM) if kv % 2 else (lane < HEAD_DIM)

    def kv_cols(kv):
        return slice((kv // 2) * LANES, (kv // 2 + 1) * LANES)

    def stages(cur):
        old = 1 - cur
        k = jnp.concatenate([kp_ref[...], kc_ref[...]], axis=0)
        v = jnp.concatenate([vp_ref[...], vc_ref[...]], axis=0)
        logits, heads = [], []
        for kv in range(SW_KV_HEADS):
            side = kv % 2
            same, other, hs, ho = [], [], [], []
            for pp in range(pairs_per_kv):
                p = kv * pairs_per_kv + pp
                q_pair = q_ref[:, p * LANES:(p + 1) * LANES] * scale
                zero = jnp.zeros_like(q_pair)
                same.append(jnp.where(on_side(kv), q_pair, zero))
                other.append(jnp.where(on_side(kv), swap_halves(q_pair), zero))
                hs.append(2 * p + side)
                ho.append(2 * p + 1 - side)
            lhs = jnp.concatenate(same + other, axis=0)
            logits.append(lax.dot_general(lhs, k[:, kv_cols(kv)], dims,
                                          preferred_element_type=F32))
            heads.append(hs + ho)
        results = []
        for kv in range(SW_KV_HEADS):
            v_pair = v[:, kv_cols(kv)]
            half = pairs_per_kv * BLOCK
            r_same = jnp.dot(p_ref[old, kv, :half, :], jnp.concatenate([v_pair, ones], axis=1),
                             preferred_element_type=F32)
            r_other = jnp.dot(p_ref[old, kv, half:, :],
                              jnp.concatenate([swap_halves(v_pair), ones], axis=1),
                              preferred_element_type=F32)
            results.append((r_same, r_other))

        for kv in range(SW_KV_HEADS):
            maxes = []
            for i, h in enumerate(heads[kv]):
                lg = logits[kv][i * BLOCK:(i + 1) * BLOCK, :] + bias_ref[0, h]
                m = jnp.maximum(jnp.max(lg, axis=-1, keepdims=True), sink_ref[h])
                p_ref[cur, kv, i * BLOCK:(i + 1) * BLOCK, :] = jnp.exp(lg - m).astype(BF16)
                maxes.append(m)
            for pp in range(pairs_per_kv):
                h_same, h_other = heads[kv][pp], heads[kv][pairs_per_kv + pp]
                m = jnp.where(on_side(kv), maxes[pp], maxes[pairs_per_kv + pp])
                sink = jnp.where(on_side(kv), sink_ref[h_same], sink_ref[h_other])
                st_ref[cur, kv * pairs_per_kv + pp] = jnp.exp(sink - m)

        for kv in range(SW_KV_HEADS):
            r_same, r_other = results[kv]
            for pp in range(pairs_per_kv):
                p = kv * pairs_per_kv + pp
                rows = slice(pp * BLOCK, (pp + 1) * BLOCK)
                num = jnp.where(on_side(kv), r_same[rows, :LANES], r_other[rows, :LANES])
                den = jnp.where(on_side(kv), r_same[rows, LANES:], r_other[rows, LANES:])
                o_ref[:, p * LANES:(p + 1) * LANES] = (num / (den + st_ref[old, p])).astype(BF16)

    @pl.when(s % 2 == 0)
    def _():
        stages(0)

    @pl.when(s % 2 == 1)
    def _():
        stages(1)


def _sw_attention(proj, sinks, bias):
    kq = OFF_Q_SW // SW_WIDTH
    kk = OFF_K_SW // SW_KV_WIDTH
    kv = OFF_V_SW // SW_KV_WIDTH
    blocks = SEQ // BLOCK
    nxt = lambda s: jnp.minimum(s, blocks - 1)
    cur = lambda s: jnp.maximum(s - 1, 0)
    prev = lambda n: jnp.maximum(n - 1, 0)
    return pl.pallas_call(
        _sw_kernel,
        grid=(blocks + 1,),
        in_specs=[
            pl.BlockSpec(memory_space=pltpu.SMEM),
            pl.BlockSpec((BLOCK, SW_WIDTH), lambda s: (nxt(s), kq)),
            pl.BlockSpec((BLOCK, SW_KV_WIDTH), lambda s: (prev(nxt(s)), kk)),
            pl.BlockSpec((BLOCK, SW_KV_WIDTH), lambda s: (nxt(s), kk)),
            pl.BlockSpec((BLOCK, SW_KV_WIDTH), lambda s: (prev(cur(s)), kv)),
            pl.BlockSpec((BLOCK, SW_KV_WIDTH), lambda s: (cur(s), kv)),
            pl.BlockSpec((1, SW_HEADS, BLOCK, 2 * BLOCK),
                         lambda s: (jnp.minimum(nxt(s), 1), 0, 0, 0)),
        ],
        out_specs=pl.BlockSpec((BLOCK, SW_WIDTH), lambda s: (cur(s), 0)),
        out_shape=jax.ShapeDtypeStruct((SEQ, SW_WIDTH), BF16),
        scratch_shapes=[
            pltpu.VMEM((2, SW_KV_HEADS, SW_HEADS // SW_KV_HEADS * BLOCK, 2 * BLOCK), BF16),
            pltpu.VMEM((2, SW_HEADS // 2, BLOCK, LANES), F32),
        ],
        compiler_params=pltpu.CompilerParams(
            dimension_semantics=("arbitrary",), vmem_limit_bytes=VMEM_LIMIT),
        name="sw_attention",
    )(sinks.astype(F32), proj, proj, proj, proj, proj, bias)


def _rms(x, g):
    ms = jnp.mean(x * x, axis=-1, keepdims=True)
    return (x * lax.rsqrt(ms + RMS_EPS)) * g


def _silu(z):
    return z * (1.0 / (1.0 + jnp.exp(-z)))


def _out_kernel(osb_ref, osw_ref, zsb_ref, zswa_ref, zswb_ref, gsb_ref, gsw_ref, gpost_ref,
                w_ref, x_ref, o_ref):
    chunk = OUT_TM // OUT_CHUNKS
    ys = []
    for c in range(OUT_CHUNKS):
        rows = pl.ds(c * chunk, chunk)
        a_sb = _rms(osb_ref[rows, :].astype(F32), gsb_ref[...]) * zsb_ref[rows, :].astype(F32)
        g_sw = jnp.concatenate([zswa_ref[rows, :], zswb_ref[rows, :]], axis=1).astype(F32)
        a_sw = _rms(osw_ref[rows, :].astype(F32), gsw_ref[...]) * g_sw
        y = jnp.dot(a_sb.astype(BF16), w_ref[:SB_WIDTH, :], preferred_element_type=F32)
        ys.append(y + jnp.dot(a_sw.astype(BF16), w_ref[SB_WIDTH:, :],
                              preferred_element_type=F32))
    for c in range(OUT_CHUNKS):
        rows = pl.ds(c * chunk, chunk)
        o_ref[rows, :] = x_ref[rows, :] + _rms(ys[c], gpost_ref[...])


def _outproj(o_sb, o_sw, proj, gn_sb, gn_sw, norm_post, w_out_bf, x2):
    half = SW_WIDTH // 2
    return pl.pallas_call(
        _out_kernel,
        grid=(SEQ // OUT_TM,),
        in_specs=[
            pl.BlockSpec((OUT_TM, SB_WIDTH), lambda i: (i, 0)),
            pl.BlockSpec((OUT_TM, SW_WIDTH), lambda i: (i, 0)),
            pl.BlockSpec((OUT_TM, SB_WIDTH), lambda i: (i, OFF_Z_SB // SB_WIDTH)),
            pl.BlockSpec((OUT_TM, half), lambda i: (i, OFF_Z_SW // half)),
            pl.BlockSpec((OUT_TM, half), lambda i: (i, OFF_Z_SW // half + 1)),
            pl.BlockSpec((1, SB_WIDTH), lambda i: (0, 0)),
            pl.BlockSpec((1, SW_WIDTH), lambda i: (0, 0)),
            pl.BlockSpec((1, D_MODEL), lambda i: (0, 0)),
            pl.BlockSpec((D_MIX, D_MODEL), lambda i: (0, 0), pipeline_mode=pl.Buffered(1)),
            pl.BlockSpec((OUT_TM, D_MODEL), lambda i: (i, 0)),
        ],
        out_specs=pl.BlockSpec((OUT_TM, D_MODEL), lambda i: (i, 0)),
        out_shape=jax.ShapeDtypeStruct((SEQ, D_MODEL), F32),
        compiler_params=pltpu.CompilerParams(
            dimension_semantics=("arbitrary",), vmem_limit_bytes=VMEM_LIMIT),
        name="outproj",
    )(o_sb, o_sw, proj, proj, proj, gn_sb, gn_sw, norm_post, w_out_bf, x2)


def kernel(x, w_in, w_out, norm_pre, norm_post, gn_sb, gn_sw, sinks, rel_bias):
    assert x.shape == (1, SEQ, D_MODEL) and w_in.shape == (1, D_MODEL, D_IN_PROJ)
    assert OFF_Z_SW % (SW_WIDTH // 2) == 0 and OFF_Q_SW % SW_WIDTH == 0
    x2 = x.reshape(SEQ, D_MODEL)
    proj = _inproj(x2, norm_pre.astype(F32), w_in[0].astype(F32))
    o_sb = _sb_attention(proj)
    bias = _bias_table(rel_bias)
    o_sw = _sw_attention(proj, sinks[0], bias)
    out = _outproj(o_sb, o_sw, proj, gn_sb.astype(F32), gn_sw.astype(F32), norm_post.astype(F32),
                   w_out[0].astype(BF16), x2)
    return out.reshape(1, SEQ, D_MODEL)
```

```python
import math

import numpy as np
import jax
import jax.numpy as jnp
from jax import lax
from jax.experimental import pallas as pl
from jax.experimental.pallas import tpu as pltpu

D_MODEL = 2048
SEQ = 16384
HEAD_DIM = 64
SB_HEADS = 16
SW_HEADS = 16
SW_KV_HEADS = 4
SB_WIDTH = SB_HEADS * HEAD_DIM
SW_WIDTH = SW_HEADS * HEAD_DIM
SW_KV_WIDTH = SW_KV_HEADS * HEAD_DIM
D_MIX = SB_WIDTH + SW_WIDTH
D_IN_PROJ = 4 * SB_WIDTH + 2 * SW_WIDTH + 2 * SW_KV_WIDTH
WINDOW = 128
BLOCK = 128
N_BUCKETS = 32
MAX_DISTANCE = 128
RMS_EPS = 1e-6
SCALE = HEAD_DIM ** -0.5

OFF_Q_SB = 0
OFF_K_SB = SB_WIDTH
OFF_V_SB = 2 * SB_WIDTH
OFF_Z_SB = 3 * SB_WIDTH
OFF_Q_SW = 4 * SB_WIDTH
OFF_K_SW = OFF_Q_SW + SW_WIDTH
OFF_V_SW = OFF_K_SW + SW_KV_WIDTH
OFF_Z_SW = OFF_V_SW + SW_KV_WIDTH

LANES = 128
VMEM_LIMIT = 56 * 1024 * 1024

PROJ_TM = 512
PROJ_TN = 3328
CAST_ROWS = 16
OUT_CAST_ROWS = 32
CAST_SLOTS = 8
SB_T = 256
SB_H = SB_T // 2
SB_UNITS = 1
OUT_TM = 512
OUT_CHUNKS = 4
NEG_BIG = -1e30
SB_STOP = -110.0

F32 = jnp.float32
BF16 = jnp.bfloat16


def _cast_weights(w_hbm, w_ref, stage_ref, sem):
    slots, rows = stage_ref.shape[0], stage_ref.shape[1]
    n_chunks = w_hbm.shape[0] // rows
    ahead = slots - 1

    def chunk_copy(c):
        slot = c % slots
        src = w_hbm.at[pl.ds(pl.multiple_of(c * rows, rows), rows), :]
        return pltpu.make_async_copy(src, stage_ref.at[slot], sem.at[slot])

    for c in range(ahead):
        chunk_copy(c).start()

    def body(c, _):
        @pl.when(c + ahead < n_chunks)
        def _():
            chunk_copy(c + ahead).start()

        chunk_copy(c).wait()
        w_ref[pl.ds(pl.multiple_of(c * rows, rows), rows), :] = stage_ref[c % slots].astype(BF16)
        return 0

    lax.fori_loop(0, n_chunks, body, 0)


def _inproj_kernel(x_ref, g_ref, w_hbm, o_ref, w_ref, stage_ref, sem):
    @pl.when(pl.program_id(0) == 0)
    def _():
        _cast_weights(w_hbm, w_ref, stage_ref, sem)

    x = x_ref[...]
    ms = jnp.mean(x * x, axis=-1, keepdims=True)
    xn = ((x * lax.rsqrt(ms + RMS_EPS)) * g_ref[...]).astype(BF16)
    gates = ((OFF_Z_SB, OFF_Z_SB + SB_WIDTH), (OFF_Z_SW, OFF_Z_SW + SW_WIDTH))
    for j in reversed(range(D_IN_PROJ // PROJ_TN)):
        lo, hi = j * PROJ_TN, (j + 1) * PROJ_TN
        res = jnp.dot(xn, w_ref[:, lo:hi], preferred_element_type=F32)
        cuts = sorted({lo, hi} | {min(max(c, lo), hi) for g in gates for c in g})
        for a, b in zip(cuts[:-1], cuts[1:]):
            part = res[:, a - lo:b - lo]
            if any(g0 <= a and b <= g1 for g0, g1 in gates):
                part = _silu(part)
            o_ref[:, a:b] = part.astype(BF16)


def _inproj(x2, g, w):
    return pl.pallas_call(
        _inproj_kernel,
        grid=(SEQ // PROJ_TM,),
        in_specs=[
            pl.BlockSpec((PROJ_TM, D_MODEL), lambda i: (i, 0)),
            pl.BlockSpec((1, D_MODEL), lambda i: (0, 0)),
            pl.BlockSpec(memory_space=pl.ANY),
        ],
        out_specs=pl.BlockSpec((PROJ_TM, D_IN_PROJ), lambda i: (i, 0)),
        out_shape=jax.ShapeDtypeStruct((SEQ, D_IN_PROJ), BF16),
        scratch_shapes=[
            pltpu.VMEM((D_MODEL, D_IN_PROJ), BF16),
            pltpu.VMEM((CAST_SLOTS, CAST_ROWS, D_IN_PROJ), F32),
            pltpu.SemaphoreType.DMA((CAST_SLOTS,)),
        ],
        compiler_params=pltpu.CompilerParams(
            dimension_semantics=("arbitrary",), vmem_limit_bytes=VMEM_LIMIT),
        name="inproj",
    )(x2, g, w)


def _sb_kernel(q_ref, k_ref, v_ref, tri_ref, o_ref,
               hi_ref, zl_ref, zr_ref, zp_ref, w_ref, carry_w_ref, go_ref,
               acc_ref, carry_ref):
    last = SEQ // SB_T - 1
    for ref in (hi_ref, zl_ref, zr_ref, zp_ref, w_ref, carry_w_ref):
        ref[...] = jnp.zeros_like(ref)

    lane = lax.broadcasted_iota(jnp.int32, (SB_T, LANES), 1)

    def cols(u):
        return slice(u * LANES, (u + 1) * LANES)

    def stack_heads(tile, u):
        start = pl.multiple_of(tile * SB_T, SB_T)
        q = q_ref[pl.ds(start, SB_T), cols(u)] * jnp.asarray(SCALE, BF16)
        zero = jnp.zeros_like(q)
        return jnp.concatenate([jnp.where(lane < HEAD_DIM, q, zero),
                                jnp.where(lane >= HEAD_DIM, q, zero)], axis=0)

    tri = tri_ref[...]
    r_h = lax.broadcasted_iota(jnp.int32, (SB_H, SB_H), 0)
    c_h = lax.broadcasted_iota(jnp.int32, (SB_H, SB_H), 1)
    before_h = c_h < r_h
    zeros_h = jnp.zeros((SB_H, SB_H), BF16)

    def qk(qs, kb, u):
        start = pl.multiple_of(kb * SB_T, SB_T)
        kblk = k_ref[pl.ds(start, SB_T), cols(u)]
        return lax.dot_general(qs, kblk, (((1,), (1,)), ((), ())), preferred_element_type=F32)

    def softplus(z):
        neg_abs = lax.bitcast_convert_type(
            lax.bitcast_convert_type(z, jnp.int32) | jnp.int32(-2 ** 31), F32)
        return (jnp.maximum(z, 0.0) + jnp.log(1.0 + jnp.exp(neg_abs))).astype(BF16)

    def cut(z):
        return jnp.where(before_h, z, NEG_BIG)

    def quarters(a):
        left = a[:, :SB_H]
        right = jnp.concatenate([a[SB_H:SB_T, SB_H:], a[SB_T + SB_H:, SB_H:]], axis=0)
        return left, right

    def unquarter(left, right):
        right_full = jnp.concatenate([zeros_h, right[:SB_H], zeros_h, right[SB_H:]], axis=0)
        return jnp.concatenate([left, right_full], axis=1)

    def cumsum(hi):
        return jnp.dot(hi, tri, preferred_element_type=F32)

    def values(kb, w, u, v_valid=None):
        start = pl.multiple_of(kb * SB_T, SB_T)
        vblk = v_ref[pl.ds(start, SB_T), cols(u)]
        if v_valid is not None:
            vblk = jnp.where(v_valid, vblk, jnp.zeros_like(vblk))
        return jnp.dot(w, vblk, preferred_element_type=F32)

    def stages(cur, n, t):
        old = 1 - cur
        units = range(SB_UNITS)
        qs_n = [stack_heads(n, u) for u in units]
        z_d = [qk(qs_n[u], n, u) for u in units]
        z_p = [qk(qs_n[u], jnp.maximum(n - 1, 0), u) for u in units]
        cs_d = [cumsum(hi_ref[u, old, 0]) for u in units]
        cs_p = [cumsum(hi_ref[u, old, 1]) for u in units]
        for u in units:
            acc_ref[u] = (values(t, w_ref[u, old, 0], u)
                          + values(jnp.maximum(t - 1, 0), w_ref[u, old, 1], u, v_valid=t > 0))
            carry_t = carry_w_ref[u, old]
            carry_ref[u] = carry_t
            go_ref[u] = (jnp.max(carry_t) >= SB_STOP).astype(jnp.int32)

        for u in units:
            zl, zr = quarters(z_d[u])
            zl = jnp.concatenate([cut(zl[:SB_H]), zl[SB_H:SB_T],
                                  cut(zl[SB_T:SB_T + SB_H]), zl[SB_T + SB_H:]], axis=0)
            zr = jnp.concatenate([cut(zr[:SB_H]), cut(zr[SB_H:])], axis=0)
            hi_ref[u, cur, 0] = unquarter(softplus(zl), softplus(zr))
            hi_ref[u, cur, 1] = softplus(z_p[u])
            zl_ref[u, cur] = zl
            zr_ref[u, cur] = zr
            zp_ref[u, cur] = z_p[u]

            cs_l, cs_r = quarters(cs_d[u])
            w_l = jnp.exp(zl_ref[u, old] + cs_l).astype(BF16)
            w_r = jnp.exp(zr_ref[u, old] + cs_r).astype(BF16)
            carry_d = cs_d[u][:, 0:1]
            w_p = jnp.exp(zp_ref[u, old] + cs_p[u] + carry_d).astype(BF16)
            w_ref[u, cur, 0] = unquarter(w_l, w_r)
            w_ref[u, cur, 1] = w_p
            carry_w_ref[u, cur] = carry_d + cs_p[u][:, 0:1]

    def block(qs, kb, u):
        z = qk(qs, kb, u)
        cs = cumsum(softplus(z))
        carry = carry_ref[u]
        w = jnp.exp(z + cs + carry).astype(BF16)
        acc_ref[u] += values(kb, w, u)
        carry_ref[u] = carry + cs[:, 0:1]

    def cond(state):
        kb, go = state
        return jnp.logical_and(kb >= 0, go > 0)

    def two_steps(i, _):
        for half in range(2):
            s = 2 * i + half
            t = jnp.maximum(s - 2, 0)
            stages(half, jnp.minimum(s, last), t)
            for u in range(SB_UNITS):
                def body(state, t=t, u=u):
                    kb, _ = state
                    block(stack_heads(t, u), kb, u)
                    go = (jnp.max(carry_ref[u]) >= SB_STOP).astype(jnp.int32)
                    return kb - 1, go

                lax.while_loop(cond, body, (t - 2, go_ref[u]))
                rows = pl.ds(pl.multiple_of(t * SB_T, SB_T), SB_T)
                o_ref[rows, cols(u)] = jnp.where(
                    lane < HEAD_DIM, acc_ref[u, :SB_T, :], acc_ref[u, SB_T:, :]).astype(BF16)
        return 0

    lax.fori_loop(0, (last + 1 + 2) // 2, two_steps, 0)


def _sb_tri():
    j = np.arange(SB_T)[:, None]
    s = np.arange(SB_T)[None, :]
    return jnp.asarray(np.where(j >= s, -1.0, 0.0).astype(np.float32), BF16)


def _sb_attention(proj):
    wide = SB_UNITS * LANES
    tiles = SEQ // SB_T
    U = SB_UNITS
    return pl.pallas_call(
        _sb_kernel,
        grid=(SB_WIDTH // wide,),
        in_specs=[
            pl.BlockSpec((SEQ, wide), lambda p: (0, OFF_Q_SB // wide + p)),
            pl.BlockSpec((SEQ, wide), lambda p: (0, OFF_K_SB // wide + p)),
            pl.BlockSpec((SEQ, wide), lambda p: (0, OFF_V_SB // wide + p)),
            pl.BlockSpec((SB_T, SB_T), lambda p: (0, 0)),
        ],
        out_specs=pl.BlockSpec((SEQ, wide), lambda p: (0, p)),
        out_shape=jax.ShapeDtypeStruct((SEQ, SB_WIDTH), BF16),
        scratch_shapes=[
            pltpu.VMEM((U, 2, 2, 2 * SB_T, SB_T), BF16),
            pltpu.VMEM((U, 2, 2 * SB_T, SB_H), F32),
            pltpu.VMEM((U, 2, SB_T, SB_H), F32),
            pltpu.VMEM((U, 2, 2 * SB_T, SB_T), F32),
            pltpu.VMEM((U, 2, 2, 2 * SB_T, SB_T), BF16),
            pltpu.VMEM((U, 2, 2 * SB_T, 1), F32),
            pltpu.SMEM((U,), jnp.int32),
            pltpu.VMEM((U, 2 * SB_T, LANES), F32),
            pltpu.VMEM((U, 2 * SB_T, 1), F32),
        ],
        compiler_params=pltpu.CompilerParams(
            dimension_semantics=("arbitrary",), vmem_limit_bytes=VMEM_LIMIT),
        name="sb_attention",
    )(proj, proj, proj, _sb_tri())


def _bucket_table():
    qi = np.arange(BLOCK)[:, None]
    ci = np.arange(2 * BLOCK)[None, :]
    dist = qi + BLOCK - ci
    n = np.maximum(dist, 0)
    max_exact = N_BUCKETS // 2
    nf = np.maximum(n, 1).astype(np.float64)
    val = np.log(nf / max_exact) / math.log(MAX_DISTANCE / max_exact) * (N_BUCKETS - max_exact)
    in_window = (dist >= 0) & (dist < WINDOW)
    frac = val - np.floor(val)
    chk = in_window & (n > max_exact)
    assert np.all((frac[chk] > 1e-3) & (frac[chk] < 1 - 1e-3))
    large = np.minimum(max_exact + val.astype(np.int64), N_BUCKETS - 1)
    bucket = np.where(n < max_exact, n, large)
    later = np.where(in_window, bucket, -1)
    first = np.where(ci >= BLOCK, later, -1)
    return jnp.asarray(np.stack([first, later]).astype(np.int32))


def _bias_kernel(rb_ref, bucket_ref, o_ref):
    b = bucket_ref[0]
    hits = [b == k for k in range(N_BUCKETS)]
    for h in range(SW_HEADS):
        acc = jnp.full(b.shape, NEG_BIG, F32)
        for k in range(N_BUCKETS):
            acc = jnp.where(hits[k], rb_ref[k, h], acc)
        o_ref[0, h] = acc


def _bias_table(rel_bias):
    return pl.pallas_call(
        _bias_kernel,
        grid=(2,),
        in_specs=[
            pl.BlockSpec(memory_space=pltpu.SMEM),
            pl.BlockSpec((1, BLOCK, 2 * BLOCK), lambda v: (v, 0, 0)),
        ],
        out_specs=pl.BlockSpec((1, SW_HEADS, BLOCK, 2 * BLOCK), lambda v: (v, 0, 0, 0)),
        out_shape=jax.ShapeDtypeStruct((2, SW_HEADS, BLOCK, 2 * BLOCK), F32),
        name="t5_bias",
    )(rel_bias.astype(F32), _bucket_table())


def _sw_kernel(sink_ref, q_ref, kp_ref, kc_ref, vp_ref, vc_ref, bias_ref, o_ref, p_ref, st_ref):
    s = pl.program_id(0)

    @pl.when(s == 0)
    def _():
        p_ref[...] = jnp.zeros_like(p_ref)
        st_ref[...] = jnp.zeros_like(st_ref)

    lane = lax.broadcasted_iota(jnp.int32, (BLOCK, LANES), 1)
    ones = jnp.ones((2 * BLOCK, LANES), BF16)
    scale = jnp.asarray(SCALE, BF16)
    pairs_per_kv = SW_HEADS // SW_KV_HEADS // 2
    dims = (((1,), (1,)), ((), ()))

    def swap_halves(t):
        return jnp.concatenate([t[:, HEAD_DIM:], t[:, :HEAD_DIM]], axis=1)

    def on_side(kv):
        return (lane >= HEAD_DIM) if kv % 2 else (lane < HEAD_DIM)

    def kv_cols(kv):
        return slice((kv // 2) * LANES, (kv // 2 + 1) * LANES)

    def stages(cur):
        old = 1 - cur
        k = jnp.concatenate([kp_ref[...], kc_ref[...]], axis=0)
        v = jnp.concatenate([vp_ref[...], vc_ref[...]], axis=0)
        logits, heads = [], []
        for kv in range(SW_KV_HEADS):
            side = kv % 2
            same, other, hs, ho = [], [], [], []
            for pp in range(pairs_per_kv):
                p = kv * pairs_per_kv + pp
                q_pair = q_ref[:, p * LANES:(p + 1) * LANES] * scale
                zero = jnp.zeros_like(q_pair)
                same.append(jnp.where(on_side(kv), q_pair, zero))
                other.append(jnp.where(on_side(kv), swap_halves(q_pair), zero))
                hs.append(2 * p + side)
                ho.append(2 * p + 1 - side)
            lhs = jnp.concatenate(same + other, axis=0)
            logits.append(lax.dot_general(lhs, k[:, kv_cols(kv)], dims,
                                          preferred_element_type=F32))
            heads.append(hs + ho)
        results = []
        for kv in range(SW_KV_HEADS):
            v_pair = v[:, kv_cols(kv)]
            half = pairs_per_kv * BLOCK
            r_same = jnp.dot(p_ref[old, kv, :half, :], jnp.concatenate([v_pair, ones], axis=1),
                             preferred_element_type=F32)
            r_other = jnp.dot(p_ref[old, kv, half:, :],
                              jnp.concatenate([swap_halves(v_pair), ones], axis=1),
                              preferred_element_type=F32)
            results.append((r_same, r_other))

        for kv in range(SW_KV_HEADS):
            maxes = []
            for i, h in enumerate(heads[kv]):
                lg = logits[kv][i * BLOCK:(i + 1) * BLOCK, :] + bias_ref[0, h]
                m = jnp.maximum(jnp.max(lg, axis=-1, keepdims=True), sink_ref[h])
                p_ref[cur, kv, i * BLOCK:(i + 1) * BLOCK, :] = jnp.exp(lg - m).astype(BF16)
                maxes.append(m)
            for pp in range(pairs_per_kv):
                h_same, h_other = heads[kv][pp], heads[kv][pairs_per_kv + pp]
                m = jnp.where(on_side(kv), maxes[pp], maxes[pairs_per_kv + pp])
                sink = jnp.where(on_side(kv), sink_ref[h_same], sink_ref[h_other])
                st_ref[cur, kv * pairs_per_kv + pp] = jnp.exp(sink - m)

        for kv in range(SW_KV_HEADS):
            r_same, r_other = results[kv]
            for pp in range(pairs_per_kv):
                p = kv * pairs_per_kv + pp
                rows = slice(pp * BLOCK, (pp + 1) * BLOCK)
                num = jnp.where(on_side(kv), r_same[rows, :LANES], r_other[rows, :LANES])
                den = jnp.where(on_side(kv), r_same[rows, LANES:], r_other[rows, LANES:])
                o_ref[:, p * LANES:(p + 1) * LANES] = (num / (den + st_ref[old, p])).astype(BF16)

    @pl.when(s % 2 == 0)
    def _():
        stages(0)

    @pl.when(s % 2 == 1)
    def _():
        stages(1)


def _sw_attention(proj, sinks, bias):
    kq = OFF_Q_SW // SW_WIDTH
    kk = OFF_K_SW // SW_KV_WIDTH
    kv = OFF_V_SW // SW_KV_WIDTH
    blocks = SEQ // BLOCK
    nxt = lambda s: jnp.minimum(s, blocks - 1)
    cur = lambda s: jnp.maximum(s - 1, 0)
    prev = lambda n: jnp.maximum(n - 1, 0)
    return pl.pallas_call(
        _sw_kernel,
        grid=(blocks + 1,),
        in_specs=[
            pl.BlockSpec(memory_space=pltpu.SMEM),
            pl.BlockSpec((BLOCK, SW_WIDTH), lambda s: (nxt(s), kq)),
            pl.BlockSpec((BLOCK, SW_KV_WIDTH), lambda s: (prev(nxt(s)), kk)),
            pl.BlockSpec((BLOCK, SW_KV_WIDTH), lambda s: (nxt(s), kk)),
            pl.BlockSpec((BLOCK, SW_KV_WIDTH), lambda s: (prev(cur(s)), kv)),
            pl.BlockSpec((BLOCK, SW_KV_WIDTH), lambda s: (cur(s), kv)),
            pl.BlockSpec((1, SW_HEADS, BLOCK, 2 * BLOCK),
                         lambda s: (jnp.minimum(nxt(s), 1), 0, 0, 0)),
        ],
        out_specs=pl.BlockSpec((BLOCK, SW_WIDTH), lambda s: (cur(s), 0)),
        out_shape=jax.ShapeDtypeStruct((SEQ, SW_WIDTH), BF16),
        scratch_shapes=[
            pltpu.VMEM((2, SW_KV_HEADS, SW_HEADS // SW_KV_HEADS * BLOCK, 2 * BLOCK), BF16),
            pltpu.VMEM((2, SW_HEADS // 2, BLOCK, LANES), F32),
        ],
        compiler_params=pltpu.CompilerParams(
            dimension_semantics=("arbitrary",), vmem_limit_bytes=VMEM_LIMIT),
        name="sw_attention",
    )(sinks.astype(F32), proj, proj, proj, proj, proj, bias)


def _rms(x, g):
    ms = jnp.mean(x * x, axis=-1, keepdims=True)
    return (x * lax.rsqrt(ms + RMS_EPS)) * g


def _silu(z):
    return z * (1.0 / (1.0 + jnp.exp(-z)))


def _out_kernel(osb_ref, osw_ref, zsb_ref, zswa_ref, zswb_ref, gsb_ref, gsw_ref, gpost_ref,
                w_hbm, x_ref, o_ref, w_ref, stage_ref, sem):
    @pl.when(pl.program_id(0) == 0)
    def _():
        _cast_weights(w_hbm, w_ref, stage_ref, sem)

    chunk = OUT_TM // OUT_CHUNKS
    ys = []
    for c in range(OUT_CHUNKS):
        rows = pl.ds(c * chunk, chunk)
        a_sb = _rms(osb_ref[rows, :].astype(F32), gsb_ref[...]) * zsb_ref[rows, :].astype(F32)
        g_sw = jnp.concatenate([zswa_ref[rows, :], zswb_ref[rows, :]], axis=1).astype(F32)
        a_sw = _rms(osw_ref[rows, :].astype(F32), gsw_ref[...]) * g_sw
        y = jnp.dot(a_sb.astype(BF16), w_ref[:SB_WIDTH, :], preferred_element_type=F32)
        ys.append(y + jnp.dot(a_sw.astype(BF16), w_ref[SB_WIDTH:, :],
                              preferred_element_type=F32))
    for c in range(OUT_CHUNKS):
        rows = pl.ds(c * chunk, chunk)
        o_ref[rows, :] = x_ref[rows, :] + _rms(ys[c], gpost_ref[...])


def _outproj(o_sb, o_sw, proj, gn_sb, gn_sw, norm_post, w_out, x2):
    half = SW_WIDTH // 2
    return pl.pallas_call(
        _out_kernel,
        grid=(SEQ // OUT_TM,),
        in_specs=[
            pl.BlockSpec((OUT_TM, SB_WIDTH), lambda i: (i, 0)),
            pl.BlockSpec((OUT_TM, SW_WIDTH), lambda i: (i, 0)),
            pl.BlockSpec((OUT_TM, SB_WIDTH), lambda i: (i, OFF_Z_SB // SB_WIDTH)),
            pl.BlockSpec((OUT_TM, half), lambda i: (i, OFF_Z_SW // half)),
            pl.BlockSpec((OUT_TM, half), lambda i: (i, OFF_Z_SW // half + 1)),
            pl.BlockSpec((1, SB_WIDTH), lambda i: (0, 0)),
            pl.BlockSpec((1, SW_WIDTH), lambda i: (0, 0)),
            pl.BlockSpec((1, D_MODEL), lambda i: (0, 0)),
            pl.BlockSpec(memory_space=pl.ANY),
            pl.BlockSpec((OUT_TM, D_MODEL), lambda i: (i, 0)),
        ],
        out_specs=pl.BlockSpec((OUT_TM, D_MODEL), lambda i: (i, 0)),
        out_shape=jax.ShapeDtypeStruct((SEQ, D_MODEL), F32),
        scratch_shapes=[
            pltpu.VMEM((D_MIX, D_MODEL), BF16),
            pltpu.VMEM((CAST_SLOTS, OUT_CAST_ROWS, D_MODEL), F32),
            pltpu.SemaphoreType.DMA((CAST_SLOTS,)),
        ],
        compiler_params=pltpu.CompilerParams(
            dimension_semantics=("arbitrary",), vmem_limit_bytes=VMEM_LIMIT),
        name="outproj",
    )(o_sb, o_sw, proj, proj, proj, gn_sb, gn_sw, norm_post, w_out, x2)


def kernel(x, w_in, w_out, norm_pre, norm_post, gn_sb, gn_sw, sinks, rel_bias):
    assert x.shape == (1, SEQ, D_MODEL) and w_in.shape == (1, D_MODEL, D_IN_PROJ)
    assert OFF_Z_SW % (SW_WIDTH // 2) == 0 and OFF_Q_SW % SW_WIDTH == 0
    x2 = x.reshape(SEQ, D_MODEL)
    proj = _inproj(x2, norm_pre.astype(F32), w_in[0].astype(F32))
    o_sb = _sb_attention(proj)
    bias = _bias_table(rel_bias)
    o_sw = _sw_attention(proj, sinks[0], bias)
    out = _outproj(o_sb, o_sw, proj, gn_sb.astype(F32), gn_sw.astype(F32), norm_post.astype(F32),
                   w_out[0].astype(F32), x2)
    return out.reshape(1, SEQ, D_MODEL)
```

```python
import math

import numpy as np
import jax
import jax.numpy as jnp
from jax import lax
from jax.experimental import pallas as pl
from jax.experimental.pallas import tpu as pltpu

D_MODEL = 2048
SEQ = 16384
HEAD_DIM = 64
SB_HEADS = 16
SW_HEADS = 16
SW_KV_HEADS = 4
SB_WIDTH = SB_HEADS * HEAD_DIM
SW_WIDTH = SW_HEADS * HEAD_DIM
SW_KV_WIDTH = SW_KV_HEADS * HEAD_DIM
D_MIX = SB_WIDTH + SW_WIDTH
D_IN_PROJ = 4 * SB_WIDTH + 2 * SW_WIDTH + 2 * SW_KV_WIDTH
WINDOW = 128
BLOCK = 128
N_BUCKETS = 32
MAX_DISTANCE = 128
RMS_EPS = 1e-6
SCALE = HEAD_DIM ** -0.5

OFF_Q_SB = 0
OFF_K_SB = SB_WIDTH
OFF_V_SB = 2 * SB_WIDTH
OFF_Z_SB = 3 * SB_WIDTH
OFF_Q_SW = 4 * SB_WIDTH
OFF_K_SW = OFF_Q_SW + SW_WIDTH
OFF_V_SW = OFF_K_SW + SW_KV_WIDTH
OFF_Z_SW = OFF_V_SW + SW_KV_WIDTH

LANES = 128
VMEM_LIMIT = 56 * 1024 * 1024

PROJ_TM = 512
PROJ_TN = 3328
CAST_ROWS = 16
OUT_CAST_ROWS = 32
CAST_SLOTS = 8
SB_T = 256
SB_H = SB_T // 2
SB_UNITS = 1
OUT_TM = 512
OUT_CHUNKS = 4
NEG_BIG = -1e30
SB_STOP = -110.0

F32 = jnp.float32
BF16 = jnp.bfloat16


def _cast_weights(w_hbm, w_ref, stage_ref, sem):
    slots, rows = stage_ref.shape[0], stage_ref.shape[1]
    n_chunks = w_hbm.shape[0] // rows
    ahead = slots - 1

    def chunk_copy(c):
        slot = c % slots
        src = w_hbm.at[pl.ds(pl.multiple_of(c * rows, rows), rows), :]
        return pltpu.make_async_copy(src, stage_ref.at[slot], sem.at[slot])

    for c in range(ahead):
        chunk_copy(c).start()

    def body(c, _):
        @pl.when(c + ahead < n_chunks)
        def _():
            chunk_copy(c + ahead).start()

        chunk_copy(c).wait()
        w_ref[pl.ds(pl.multiple_of(c * rows, rows), rows), :] = stage_ref[c % slots].astype(BF16)
        return 0

    lax.fori_loop(0, n_chunks, body, 0)


def _inproj_kernel(x_ref, g_ref, w_hbm, o_ref, w_ref, stage_ref, sem):
    @pl.when(pl.program_id(0) == 0)
    def _():
        _cast_weights(w_hbm, w_ref, stage_ref, sem)

    x = x_ref[...]
    ms = jnp.mean(x * x, axis=-1, keepdims=True)
    xn = ((x * lax.rsqrt(ms + RMS_EPS)) * g_ref[...]).astype(BF16)
    gates = ((OFF_Z_SB, OFF_Z_SB + SB_WIDTH), (OFF_Z_SW, OFF_Z_SW + SW_WIDTH))
    for j in reversed(range(D_IN_PROJ // PROJ_TN)):
        lo, hi = j * PROJ_TN, (j + 1) * PROJ_TN
        res = jnp.dot(xn, w_ref[:, lo:hi], preferred_element_type=F32)
        cuts = sorted({lo, hi} | {min(max(c, lo), hi) for g in gates for c in g})
        for a, b in zip(cuts[:-1], cuts[1:]):
            part = res[:, a - lo:b - lo]
            if any(g0 <= a and b <= g1 for g0, g1 in gates):
                part = _silu(part)
            o_ref[:, a:b] = part.astype(BF16)


def _inproj(x2, g, w):
    return pl.pallas_call(
        _inproj_kernel,
        grid=(SEQ // PROJ_TM,),
        in_specs=[
            pl.BlockSpec((PROJ_TM, D_MODEL), lambda i: (i, 0)),
            pl.BlockSpec((1, D_MODEL), lambda i: (0, 0)),
            pl.BlockSpec(memory_space=pl.ANY),
        ],
        out_specs=pl.BlockSpec((PROJ_TM, D_IN_PROJ), lambda i: (i, 0)),
        out_shape=jax.ShapeDtypeStruct((SEQ, D_IN_PROJ), BF16),
        scratch_shapes=[
            pltpu.VMEM((D_MODEL, D_IN_PROJ), BF16),
            pltpu.VMEM((CAST_SLOTS, CAST_ROWS, D_IN_PROJ), F32),
            pltpu.SemaphoreType.DMA((CAST_SLOTS,)),
        ],
        compiler_params=pltpu.CompilerParams(
            dimension_semantics=("arbitrary",), vmem_limit_bytes=VMEM_LIMIT),
        name="inproj",
    )(x2, g, w)


def _sb_kernel(q_ref, k_ref, v_ref, tri_ref, o_ref,
               hi_ref, lbl_ref, lbr_ref, lbp_ref, sp0_ref, w_ref, carry_w_ref, go_ref,
               acc_ref, carry_ref):
    last = SEQ // SB_T - 1
    for ref in (hi_ref, lbl_ref, lbr_ref, lbp_ref, sp0_ref, w_ref, carry_w_ref):
        ref[...] = jnp.zeros_like(ref)

    lane = lax.broadcasted_iota(jnp.int32, (SB_T, LANES), 1)

    def cols(u):
        return slice(u * LANES, (u + 1) * LANES)

    def stack_heads(tile, u):
        start = pl.multiple_of(tile * SB_T, SB_T)
        q = q_ref[pl.ds(start, SB_T), cols(u)] * jnp.asarray(SCALE, BF16)
        zero = jnp.zeros_like(q)
        return jnp.concatenate([jnp.where(lane < HEAD_DIM, q, zero),
                                jnp.where(lane >= HEAD_DIM, q, zero)], axis=0)

    tri = tri_ref[...]
    r_h = lax.broadcasted_iota(jnp.int32, (SB_H, SB_H), 0)
    c_h = lax.broadcasted_iota(jnp.int32, (SB_H, SB_H), 1)
    before_h = c_h < r_h
    zeros_h = jnp.zeros((SB_H, SB_H), BF16)

    def qk(qs, kb, u):
        start = pl.multiple_of(kb * SB_T, SB_T)
        kblk = k_ref[pl.ds(start, SB_T), cols(u)]
        return lax.dot_general(qs, kblk, (((1,), (1,)), ((), ())), preferred_element_type=F32)

    def softplus_parts(z):
        neg_abs = lax.bitcast_convert_type(
            lax.bitcast_convert_type(z, jnp.int32) | jnp.int32(-2 ** 31), F32)
        sp = jnp.maximum(z, 0.0) + jnp.log(1.0 + jnp.exp(neg_abs))
        return sp.astype(BF16), z - sp, sp[:, 0:1]

    def cut(z):
        return jnp.where(before_h, z, NEG_BIG)

    def quarters(a):
        left = a[:, :SB_H]
        right = jnp.concatenate([a[SB_H:SB_T, SB_H:], a[SB_T + SB_H:, SB_H:]], axis=0)
        return left, right

    def unquarter(left, right):
        right_full = jnp.concatenate([zeros_h, right[:SB_H], zeros_h, right[SB_H:]], axis=0)
        return jnp.concatenate([left, right_full], axis=1)

    def cumsum(hi):
        return jnp.dot(hi, tri, preferred_element_type=F32)

    def values(kb, w, u, v_valid=None):
        start = pl.multiple_of(kb * SB_T, SB_T)
        vblk = v_ref[pl.ds(start, SB_T), cols(u)]
        if v_valid is not None:
            vblk = jnp.where(v_valid, vblk, jnp.zeros_like(vblk))
        return jnp.dot(w, vblk, preferred_element_type=F32)

    def stages(cur, n, t):
        old = 1 - cur
        units = range(SB_UNITS)
        qs_n = [stack_heads(n, u) for u in units]
        z_d = [qk(qs_n[u], n, u) for u in units]
        z_p = [qk(qs_n[u], jnp.maximum(n - 1, 0), u) for u in units]
        cs_d = [cumsum(hi_ref[u, old, 0]) for u in units]
        cs_p = [cumsum(hi_ref[u, old, 1]) for u in units]
        for u in units:
            acc_ref[u] = (values(t, w_ref[u, old, 0], u)
                          + values(jnp.maximum(t - 1, 0), w_ref[u, old, 1], u, v_valid=t > 0))
            carry_t = carry_w_ref[u, old]
            carry_ref[u] = carry_t
            go_ref[u] = (jnp.max(carry_t) >= SB_STOP).astype(jnp.int32)

        for u in units:
            zl, zr = quarters(z_d[u])
            zl = jnp.concatenate([cut(zl[:SB_H]), zl[SB_H:SB_T],
                                  cut(zl[SB_T:SB_T + SB_H]), zl[SB_T + SB_H:]], axis=0)
            zr = jnp.concatenate([cut(zr[:SB_H]), cut(zr[SB_H:])], axis=0)
            hi_l, lbl_ref[u, cur], sp0_ref[u, cur, 0] = softplus_parts(zl)
            hi_r, lbr_ref[u, cur], _ = softplus_parts(zr)
            hi_ref[u, cur, 0] = unquarter(hi_l, hi_r)
            hi_ref[u, cur, 1], lbp_ref[u, cur], sp0_ref[u, cur, 1] = softplus_parts(z_p[u])

            cs_l, cs_r = quarters(cs_d[u])
            w_l = jnp.exp(lbl_ref[u, old] + cs_l).astype(BF16)
            w_r = jnp.exp(lbr_ref[u, old] + cs_r).astype(BF16)
            carry_d = cs_d[u][:, 0:1] - sp0_ref[u, old, 0]
            w_p = jnp.exp(lbp_ref[u, old] + cs_p[u] + carry_d).astype(BF16)
            w_ref[u, cur, 0] = unquarter(w_l, w_r)
            w_ref[u, cur, 1] = w_p
            carry_w_ref[u, cur] = carry_d + (cs_p[u][:, 0:1] - sp0_ref[u, old, 1])

    def block(qs, kb, u):
        hi, log_beta, sp0 = softplus_parts(qk(qs, kb, u))
        cs = cumsum(hi)
        carry = carry_ref[u]
        w = jnp.exp(log_beta + cs + carry).astype(BF16)
        acc_ref[u] += values(kb, w, u)
        carry_ref[u] = carry + (cs[:, 0:1] - sp0)

    def cond(state):
        kb, go = state
        return jnp.logical_and(kb >= 0, go > 0)

    def two_steps(i, _):
        for half in range(2):
            s = 2 * i + half
            t = jnp.maximum(s - 2, 0)
            stages(half, jnp.minimum(s, last), t)
            for u in range(SB_UNITS):
                def body(state, t=t, u=u):
                    kb, _ = state
                    block(stack_heads(t, u), kb, u)
                    go = (jnp.max(carry_ref[u]) >= SB_STOP).astype(jnp.int32)
                    return kb - 1, go

                lax.while_loop(cond, body, (t - 2, go_ref[u]))
                rows = pl.ds(pl.multiple_of(t * SB_T, SB_T), SB_T)
                o_ref[rows, cols(u)] = jnp.where(
                    lane < HEAD_DIM, acc_ref[u, :SB_T, :], acc_ref[u, SB_T:, :]).astype(BF16)
        return 0

    lax.fori_loop(0, (last + 1 + 2) // 2, two_steps, 0)


def _sb_tri():
    j = np.arange(SB_T)[:, None]
    s = np.arange(SB_T)[None, :]
    return jnp.asarray(np.where(j > s, -1.0, 0.0).astype(np.float32), BF16)


def _sb_attention(proj):
    wide = SB_UNITS * LANES
    tiles = SEQ // SB_T
    U = SB_UNITS
    return pl.pallas_call(
        _sb_kernel,
        grid=(SB_WIDTH // wide,),
        in_specs=[
            pl.BlockSpec((SEQ, wide), lambda p: (0, OFF_Q_SB // wide + p)),
            pl.BlockSpec((SEQ, wide), lambda p: (0, OFF_K_SB // wide + p)),
            pl.BlockSpec((SEQ, wide), lambda p: (0, OFF_V_SB // wide + p)),
            pl.BlockSpec((SB_T, SB_T), lambda p: (0, 0)),
        ],
        out_specs=pl.BlockSpec((SEQ, wide), lambda p: (0, p)),
        out_shape=jax.ShapeDtypeStruct((SEQ, SB_WIDTH), BF16),
        scratch_shapes=[
            pltpu.VMEM((U, 2, 2, 2 * SB_T, SB_T), BF16),
            pltpu.VMEM((U, 2, 2 * SB_T, SB_H), F32),
            pltpu.VMEM((U, 2, SB_T, SB_H), F32),
            pltpu.VMEM((U, 2, 2 * SB_T, SB_T), F32),
            pltpu.VMEM((U, 2, 2, 2 * SB_T, 1), F32),
            pltpu.VMEM((U, 2, 2, 2 * SB_T, SB_T), BF16),
            pltpu.VMEM((U, 2, 2 * SB_T, 1), F32),
            pltpu.SMEM((U,), jnp.int32),
            pltpu.VMEM((U, 2 * SB_T, LANES), F32),
            pltpu.VMEM((U, 2 * SB_T, 1), F32),
        ],
        compiler_params=pltpu.CompilerParams(
            dimension_semantics=("arbitrary",), vmem_limit_bytes=VMEM_LIMIT),
        name="sb_attention",
    )(proj, proj, proj, _sb_tri())


def _bucket_table():
    qi = np.arange(BLOCK)[:, None]
    ci = np.arange(2 * BLOCK)[None, :]
    dist = qi + BLOCK - ci
    n = np.maximum(dist, 0)
    max_exact = N_BUCKETS // 2
    nf = np.maximum(n, 1).astype(np.float64)
    val = np.log(nf / max_exact) / math.log(MAX_DISTANCE / max_exact) * (N_BUCKETS - max_exact)
    in_window = (dist >= 0) & (dist < WINDOW)
    frac = val - np.floor(val)
    chk = in_window & (n > max_exact)
    assert np.all((frac[chk] > 1e-3) & (frac[chk] < 1 - 1e-3))
    large = np.minimum(max_exact + val.astype(np.int64), N_BUCKETS - 1)
    bucket = np.where(n < max_exact, n, large)
    later = np.where(in_window, bucket, -1)
    first = np.where(ci >= BLOCK, later, -1)
    return jnp.asarray(np.stack([first, later]).astype(np.int32))


def _bias_kernel(rb_ref, bucket_ref, o_ref):
    b = bucket_ref[0]
    hits = [b == k for k in range(N_BUCKETS)]
    for h in range(SW_HEADS):
        acc = jnp.full(b.shape, NEG_BIG, F32)
        for k in range(N_BUCKETS):
            acc = jnp.where(hits[k], rb_ref[k, h], acc)
        o_ref[0, h] = acc


def _bias_table(rel_bias):
    return pl.pallas_call(
        _bias_kernel,
        grid=(2,),
        in_specs=[
            pl.BlockSpec(memory_space=pltpu.SMEM),
            pl.BlockSpec((1, BLOCK, 2 * BLOCK), lambda v: (v, 0, 0)),
        ],
        out_specs=pl.BlockSpec((1, SW_HEADS, BLOCK, 2 * BLOCK), lambda v: (v, 0, 0, 0)),
        out_shape=jax.ShapeDtypeStruct((2, SW_HEADS, BLOCK, 2 * BLOCK), F32),
        name="t5_bias",
    )(rel_bias.astype(F32), _bucket_table())


def _sw_kernel(sink_ref, qa_ref, qb_ref, k0_ref, k1_ref, k2_ref, v0_ref, v1_ref, v2_ref, bias_ref,
               o_ref, p_ref, st_ref):
    g = pl.program_id(0)

    @pl.when(g == 0)
    def _():
        p_ref[...] = jnp.zeros_like(p_ref)
        st_ref[...] = jnp.zeros_like(st_ref)

    lane = lax.broadcasted_iota(jnp.int32, (BLOCK, LANES), 1)
    ones = jnp.ones((2 * BLOCK, LANES), BF16)
    scale = jnp.asarray(SCALE, BF16)
    pairs_per_kv = SW_HEADS // SW_KV_HEADS // 2
    dims = (((1,), (1,)), ((), ()))

    def swap_halves(t):
        return jnp.concatenate([t[:, HEAD_DIM:], t[:, :HEAD_DIM]], axis=1)

    def on_side(kv):
        return (lane >= HEAD_DIM) if kv % 2 else (lane < HEAD_DIM)

    def kv_cols(kv):
        return slice((kv // 2) * LANES, (kv // 2 + 1) * LANES)

    def stages(cur, q_ref, kp_ref, kc_ref, vp_ref, vc_ref, out_rows):
        old = 1 - cur
        k = jnp.concatenate([kp_ref[...], kc_ref[...]], axis=0)
        v = jnp.concatenate([vp_ref[...], vc_ref[...]], axis=0)
        logits, heads = [], []
        for kv in range(SW_KV_HEADS):
            side = kv % 2
            same, other, hs, ho = [], [], [], []
            for pp in range(pairs_per_kv):
                p = kv * pairs_per_kv + pp
                q_pair = q_ref[:, p * LANES:(p + 1) * LANES] * scale
                zero = jnp.zeros_like(q_pair)
                same.append(jnp.where(on_side(kv), q_pair, zero))
                other.append(jnp.where(on_side(kv), swap_halves(q_pair), zero))
                hs.append(2 * p + side)
                ho.append(2 * p + 1 - side)
            lhs = jnp.concatenate(same + other, axis=0)
            logits.append(lax.dot_general(lhs, k[:, kv_cols(kv)], dims,
                                          preferred_element_type=F32))
            heads.append(hs + ho)
        results = []
        for kv in range(SW_KV_HEADS):
            v_pair = v[:, kv_cols(kv)]
            half = pairs_per_kv * BLOCK
            r_same = jnp.dot(p_ref[old, kv, :half, :], jnp.concatenate([v_pair, ones], axis=1),
                             preferred_element_type=F32)
            r_other = jnp.dot(p_ref[old, kv, half:, :],
                              jnp.concatenate([swap_halves(v_pair), ones], axis=1),
                              preferred_element_type=F32)
            results.append((r_same, r_other))

        for kv in range(SW_KV_HEADS):
            maxes = []
            for i, h in enumerate(heads[kv]):
                lg = logits[kv][i * BLOCK:(i + 1) * BLOCK, :] + bias_ref[0, h]
                m = jnp.maximum(jnp.max(lg, axis=-1, keepdims=True), sink_ref[h])
                p_ref[cur, kv, i * BLOCK:(i + 1) * BLOCK, :] = jnp.exp(lg - m).astype(BF16)
                maxes.append(m)
            for pp in range(pairs_per_kv):
                h_same, h_other = heads[kv][pp], heads[kv][pairs_per_kv + pp]
                m = jnp.where(on_side(kv), maxes[pp], maxes[pairs_per_kv + pp])
                sink = jnp.where(on_side(kv), sink_ref[h_same], sink_ref[h_other])
                st_ref[cur, kv * pairs_per_kv + pp] = jnp.exp(sink - m)

        for kv in range(SW_KV_HEADS):
            r_same, r_other = results[kv]
            for pp in range(pairs_per_kv):
                p = kv * pairs_per_kv + pp
                rows = slice(pp * BLOCK, (pp + 1) * BLOCK)
                num = jnp.where(on_side(kv), r_same[rows, :LANES], r_other[rows, :LANES])
                den = jnp.where(on_side(kv), r_same[rows, LANES:], r_other[rows, LANES:])
                o_ref[out_rows, p * LANES:(p + 1) * LANES] = (
                    num / (den + st_ref[old, p])).astype(BF16)

    @pl.when(g >= 0)
    def _():
        stages(1, qa_ref, k0_ref, k1_ref, v0_ref, v1_ref, slice(0, BLOCK))

    @pl.when(g >= 0)
    def _():
        stages(0, qb_ref, k1_ref, k2_ref, v1_ref, v2_ref, slice(BLOCK, 2 * BLOCK))


def _sw_attention(proj, sinks, bias):
    kq = OFF_Q_SW // SW_WIDTH
    kk = OFF_K_SW // SW_KV_WIDTH
    kv = OFF_V_SW // SW_KV_WIDTH
    last = SEQ // BLOCK - 1
    blk = lambda n: jnp.clip(n, 0, last)
    q_spec = lambda off: pl.BlockSpec((BLOCK, SW_WIDTH), lambda g: (blk(2 * g + off), kq))
    k_spec = lambda off: pl.BlockSpec((BLOCK, SW_KV_WIDTH), lambda g: (blk(2 * g + off), kk))
    v_spec = lambda off: pl.BlockSpec((BLOCK, SW_KV_WIDTH), lambda g: (blk(2 * g + off), kv))
    return pl.pallas_call(
        _sw_kernel,
        grid=((last + 1) // 2 + 1,),
        in_specs=[
            pl.BlockSpec(memory_space=pltpu.SMEM),
            q_spec(-1), q_spec(0),
            k_spec(-2), k_spec(-1), k_spec(0),
            v_spec(-3), v_spec(-2), v_spec(-1),
            pl.BlockSpec((1, SW_HEADS, BLOCK, 2 * BLOCK), lambda g: (jnp.minimum(g, 1), 0, 0, 0)),
        ],
        out_specs=pl.BlockSpec((2 * BLOCK, SW_WIDTH), lambda g: (jnp.maximum(g - 1, 0), 0)),
        out_shape=jax.ShapeDtypeStruct((SEQ, SW_WIDTH), BF16),
        scratch_shapes=[
            pltpu.VMEM((2, SW_KV_HEADS, SW_HEADS // SW_KV_HEADS * BLOCK, 2 * BLOCK), BF16),
            pltpu.VMEM((2, SW_HEADS // 2, BLOCK, LANES), F32),
        ],
        compiler_params=pltpu.CompilerParams(
            dimension_semantics=("arbitrary",), vmem_limit_bytes=VMEM_LIMIT),
        name="sw_attention",
    )(sinks.astype(F32), proj, proj, proj, proj, proj, proj, proj, proj, bias)


def _rms(x, g):
    ms = jnp.mean(x * x, axis=-1, keepdims=True)
    return (x * lax.rsqrt(ms + RMS_EPS)) * g


def _silu(z):
    return z * (1.0 / (1.0 + jnp.exp(-z)))


def _out_kernel(osb_ref, osw_ref, zsb_ref, zswa_ref, zswb_ref, gsb_ref, gsw_ref, gpost_ref,
                w_hbm, x_ref, o_ref, w_ref, stage_ref, sem):
    @pl.when(pl.program_id(0) == 0)
    def _():
        _cast_weights(w_hbm, w_ref, stage_ref, sem)

    chunk = OUT_TM // OUT_CHUNKS
    ys = []
    for c in range(OUT_CHUNKS):
        rows = pl.ds(c * chunk, chunk)
        a_sb = _rms(osb_ref[rows, :].astype(F32), gsb_ref[...]) * zsb_ref[rows, :].astype(F32)
        g_sw = jnp.concatenate([zswa_ref[rows, :], zswb_ref[rows, :]], axis=1).astype(F32)
        a_sw = _rms(osw_ref[rows, :].astype(F32), gsw_ref[...]) * g_sw
        y = jnp.dot(a_sb.astype(BF16), w_ref[:SB_WIDTH, :], preferred_element_type=F32)
        ys.append(y + jnp.dot(a_sw.astype(BF16), w_ref[SB_WIDTH:, :],
                              preferred_element_type=F32))
    for c in range(OUT_CHUNKS):
        rows = pl.ds(c * chunk, chunk)
        o_ref[rows, :] = x_ref[rows, :] + _rms(ys[c], gpost_ref[...])


def _outproj(o_sb, o_sw, proj, gn_sb, gn_sw, norm_post, w_out, x2):
    half = SW_WIDTH // 2
    return pl.pallas_call(
        _out_kernel,
        grid=(SEQ // OUT_TM,),
        in_specs=[
            pl.BlockSpec((OUT_TM, SB_WIDTH), lambda i: (i, 0)),
            pl.BlockSpec((OUT_TM, SW_WIDTH), lambda i: (i, 0)),
            pl.BlockSpec((OUT_TM, SB_WIDTH), lambda i: (i, OFF_Z_SB // SB_WIDTH)),
            pl.BlockSpec((OUT_TM, half), lambda i: (i, OFF_Z_SW // half)),
            pl.BlockSpec((OUT_TM, half), lambda i: (i, OFF_Z_SW // half + 1)),
            pl.BlockSpec((1, SB_WIDTH), lambda i: (0, 0)),
            pl.BlockSpec((1, SW_WIDTH), lambda i: (0, 0)),
            pl.BlockSpec((1, D_MODEL), lambda i: (0, 0)),
            pl.BlockSpec(memory_space=pl.ANY),
            pl.BlockSpec((OUT_TM, D_MODEL), lambda i: (i, 0)),
        ],
        out_specs=pl.BlockSpec((OUT_TM, D_MODEL), lambda i: (i, 0)),
        out_shape=jax.ShapeDtypeStruct((SEQ, D_MODEL), F32),
        scratch_shapes=[
            pltpu.VMEM((D_MIX, D_MODEL), BF16),
            pltpu.VMEM((CAST_SLOTS, OUT_CAST_ROWS, D_MODEL), F32),
            pltpu.SemaphoreType.DMA((CAST_SLOTS,)),
        ],
        compiler_params=pltpu.CompilerParams(
            dimension_semantics=("arbitrary",), vmem_limit_bytes=VMEM_LIMIT),
        name="outproj",
    )(o_sb, o_sw, proj, proj, proj, gn_sb, gn_sw, norm_post, w_out, x2)


def kernel(x, w_in, w_out, norm_pre, norm_post, gn_sb, gn_sw, sinks, rel_bias):
    assert x.shape == (1, SEQ, D_MODEL) and w_in.shape == (1, D_MODEL, D_IN_PROJ)
    assert OFF_Z_SW % (SW_WIDTH // 2) == 0 and OFF_Q_SW % SW_WIDTH == 0
    x2 = x.reshape(SEQ, D_MODEL)
    proj = _inproj(x2, norm_pre.astype(F32), w_in[0].astype(F32))
    o_sb = _sb_attention(proj)
    bias = _bias_table(rel_bias)
    o_sw = _sw_attention(proj, sinks[0], bias)
    out = _outproj(o_sb, o_sw, proj, gn_sb.astype(F32), gn_sw.astype(F32), norm_post.astype(F32),
                   w_out[0].astype(F32), x2)
    return out.reshape(1, SEQ, D_MODEL)
```

```python
import math

import numpy as np
import jax
import jax.numpy as jnp
from jax import lax
from jax.experimental import pallas as pl
from jax.experimental.pallas import tpu as pltpu

D_MODEL = 2048
SEQ = 16384
HEAD_DIM = 64
SB_HEADS = 16
SW_HEADS = 16
SW_KV_HEADS = 4
SB_WIDTH = SB_HEADS * HEAD_DIM
SW_WIDTH = SW_HEADS * HEAD_DIM
SW_KV_WIDTH = SW_KV_HEADS * HEAD_DIM
D_MIX = SB_WIDTH + SW_WIDTH
D_IN_PROJ = 4 * SB_WIDTH + 2 * SW_WIDTH + 2 * SW_KV_WIDTH
WINDOW = 128
BLOCK = 128
N_BUCKETS = 32
MAX_DISTANCE = 128
RMS_EPS = 1e-6
SCALE = HEAD_DIM ** -0.5

OFF_Q_SB = 0
OFF_K_SB = SB_WIDTH
OFF_V_SB = 2 * SB_WIDTH
OFF_Z_SB = 3 * SB_WIDTH
OFF_Q_SW = 4 * SB_WIDTH
OFF_K_SW = OFF_Q_SW + SW_WIDTH
OFF_V_SW = OFF_K_SW + SW_KV_WIDTH
OFF_Z_SW = OFF_V_SW + SW_KV_WIDTH

LANES = 128
VMEM_LIMIT = 56 * 1024 * 1024

PROJ_TM = 512
PROJ_TN = 3328
CAST_ROWS = 16
OUT_CAST_ROWS = 32
CAST_SLOTS = 8
SB_T = 256
SB_H = SB_T // 2
SB_UNITS = 1
OUT_TM = 512
OUT_CHUNKS = 4
NEG_BIG = -1e30
SB_STOP = -110.0

F32 = jnp.float32
BF16 = jnp.bfloat16


def _cast_weights(w_hbm, w_ref, stage_ref, sem):
    slots, rows = stage_ref.shape[0], stage_ref.shape[1]
    n_chunks = w_hbm.shape[0] // rows
    ahead = slots - 1

    def chunk_copy(c):
        slot = c % slots
        src = w_hbm.at[pl.ds(pl.multiple_of(c * rows, rows), rows), :]
        return pltpu.make_async_copy(src, stage_ref.at[slot], sem.at[slot])

    for c in range(ahead):
        chunk_copy(c).start()

    def body(c, _):
        @pl.when(c + ahead < n_chunks)
        def _():
            chunk_copy(c + ahead).start()

        chunk_copy(c).wait()
        w_ref[pl.ds(pl.multiple_of(c * rows, rows), rows), :] = stage_ref[c % slots].astype(BF16)
        return 0

    lax.fori_loop(0, n_chunks, body, 0)


def _inproj_kernel(x_ref, g_ref, w_hbm, o_ref, w_ref, stage_ref, sem):
    @pl.when(pl.program_id(0) == 0)
    def _():
        _cast_weights(w_hbm, w_ref, stage_ref, sem)

    x = x_ref[...]
    ms = jnp.mean(x * x, axis=-1, keepdims=True)
    xn = ((x * lax.rsqrt(ms + RMS_EPS)) * g_ref[...]).astype(BF16)
    gates = ((OFF_Z_SB, OFF_Z_SB + SB_WIDTH), (OFF_Z_SW, OFF_Z_SW + SW_WIDTH))
    for j in reversed(range(D_IN_PROJ // PROJ_TN)):
        lo, hi = j * PROJ_TN, (j + 1) * PROJ_TN
        res = jnp.dot(xn, w_ref[:, lo:hi], preferred_element_type=F32)
        cuts = sorted({lo, hi} | {min(max(c, lo), hi) for g in gates for c in g})
        for a, b in zip(cuts[:-1], cuts[1:]):
            part = res[:, a - lo:b - lo]
            if any(g0 <= a and b <= g1 for g0, g1 in gates):
                part = _silu(part)
            o_ref[:, a:b] = part.astype(BF16)


def _inproj(x2, g, w):
    return pl.pallas_call(
        _inproj_kernel,
        grid=(SEQ // PROJ_TM,),
        in_specs=[
            pl.BlockSpec((PROJ_TM, D_MODEL), lambda i: (i, 0)),
            pl.BlockSpec((1, D_MODEL), lambda i: (0, 0)),
            pl.BlockSpec(memory_space=pl.ANY),
        ],
        out_specs=pl.BlockSpec((PROJ_TM, D_IN_PROJ), lambda i: (i, 0)),
        out_shape=jax.ShapeDtypeStruct((SEQ, D_IN_PROJ), BF16),
        scratch_shapes=[
            pltpu.VMEM((D_MODEL, D_IN_PROJ), BF16),
            pltpu.VMEM((CAST_SLOTS, CAST_ROWS, D_IN_PROJ), F32),
            pltpu.SemaphoreType.DMA((CAST_SLOTS,)),
        ],
        compiler_params=pltpu.CompilerParams(
            dimension_semantics=("arbitrary",), vmem_limit_bytes=VMEM_LIMIT),
        name="inproj",
    )(x2, g, w)


def _sb_kernel(q_ref, k_ref, v_ref, tri_ref, o_ref,
               hi_ref, lbl_ref, lbr_ref, lbp_ref, sp0_ref, w_ref, carry_w_ref, go_ref,
               acc_ref, carry_ref):
    last = SEQ // SB_T - 1
    for ref in (hi_ref, lbl_ref, lbr_ref, lbp_ref, sp0_ref, w_ref, carry_w_ref):
        ref[...] = jnp.zeros_like(ref)

    lane = lax.broadcasted_iota(jnp.int32, (SB_T, LANES), 1)

    def cols(u):
        return slice(u * LANES, (u + 1) * LANES)

    def stack_heads(tile, u):
        start = pl.multiple_of(tile * SB_T, SB_T)
        q = q_ref[pl.ds(start, SB_T), cols(u)] * jnp.asarray(SCALE, BF16)
        zero = jnp.zeros_like(q)
        return jnp.concatenate([jnp.where(lane < HEAD_DIM, q, zero),
                                jnp.where(lane >= HEAD_DIM, q, zero)], axis=0)

    tri = tri_ref[...]
    r_h = lax.broadcasted_iota(jnp.int32, (SB_H, SB_H), 0)
    c_h = lax.broadcasted_iota(jnp.int32, (SB_H, SB_H), 1)
    before_h = c_h < r_h
    zeros_h = jnp.zeros((SB_H, SB_H), BF16)

    def qk(qs, kb, u):
        start = pl.multiple_of(kb * SB_T, SB_T)
        kblk = k_ref[pl.ds(start, SB_T), cols(u)]
        return lax.dot_general(qs, kblk, (((1,), (1,)), ((), ())), preferred_element_type=F32)

    def softplus_parts(z):
        neg_abs = lax.bitcast_convert_type(
            lax.bitcast_convert_type(z, jnp.int32) | jnp.int32(-2 ** 31), F32)
        sp = jnp.maximum(z, 0.0) + jnp.log(1.0 + jnp.exp(neg_abs))
        return sp.astype(BF16), z - sp, sp[:, 0:1]

    def cut(z):
        return jnp.where(before_h, z, NEG_BIG)

    def quarters(a):
        left = a[:, :SB_H]
        right = jnp.concatenate([a[SB_H:SB_T, SB_H:], a[SB_T + SB_H:, SB_H:]], axis=0)
        return left, right

    def unquarter(left, right):
        right_full = jnp.concatenate([zeros_h, right[:SB_H], zeros_h, right[SB_H:]], axis=0)
        return jnp.concatenate([left, right_full], axis=1)

    def cumsum(hi):
        return jnp.dot(hi, tri, preferred_element_type=F32)

    def values(kb, w, u, v_valid=None):
        start = pl.multiple_of(kb * SB_T, SB_T)
        vblk = v_ref[pl.ds(start, SB_T), cols(u)]
        if v_valid is not None:
            vblk = jnp.where(v_valid, vblk, jnp.zeros_like(vblk))
        return jnp.dot(w, vblk, preferred_element_type=F32)

    def stages(cur, n, t):
        old = 1 - cur
        units = range(SB_UNITS)
        qs_n = [stack_heads(n, u) for u in units]
        z_d = [qk(qs_n[u], n, u) for u in units]
        z_p = [qk(qs_n[u], jnp.maximum(n - 1, 0), u) for u in units]
        cs_d = [cumsum(hi_ref[u, old, 0]) for u in units]
        cs_p = [cumsum(hi_ref[u, old, 1]) for u in units]
        for u in units:
            acc_ref[u] = (values(t, w_ref[u, old, 0], u)
                          + values(jnp.maximum(t - 1, 0), w_ref[u, old, 1], u, v_valid=t > 0))
            carry_t = carry_w_ref[u, old]
            carry_ref[u] = carry_t
            go_ref[u] = (jnp.max(carry_t) >= SB_STOP).astype(jnp.int32)

        for u in units:
            zl, zr = quarters(z_d[u])
            zl = jnp.concatenate([cut(zl[:SB_H]), zl[SB_H:SB_T],
                                  cut(zl[SB_T:SB_T + SB_H]), zl[SB_T + SB_H:]], axis=0)
            zr = jnp.concatenate([cut(zr[:SB_H]), cut(zr[SB_H:])], axis=0)
            hi_l, lbl_ref[u, cur], sp0_ref[u, cur, 0] = softplus_parts(zl)
            hi_r, lbr_ref[u, cur], _ = softplus_parts(zr)
            hi_ref[u, cur, 0] = unquarter(hi_l, hi_r)
            hi_ref[u, cur, 1], lbp_ref[u, cur], sp0_ref[u, cur, 1] = softplus_parts(z_p[u])

            cs_l, cs_r = quarters(cs_d[u])
            w_l = jnp.exp(lbl_ref[u, old] + cs_l).astype(BF16)
            w_r = jnp.exp(lbr_ref[u, old] + cs_r).astype(BF16)
            carry_d = cs_d[u][:, 0:1] - sp0_ref[u, old, 0]
            w_p = jnp.exp(lbp_ref[u, old] + cs_p[u] + carry_d).astype(BF16)
            w_ref[u, cur, 0] = unquarter(w_l, w_r)
            w_ref[u, cur, 1] = w_p
            carry_w_ref[u, cur] = carry_d + (cs_p[u][:, 0:1] - sp0_ref[u, old, 1])

    def block(qs, kb, u):
        hi, log_beta, sp0 = softplus_parts(qk(qs, kb, u))
        cs = cumsum(hi)
        carry = carry_ref[u]
        w = jnp.exp(log_beta + cs + carry).astype(BF16)
        acc_ref[u] += values(kb, w, u)
        carry_ref[u] = carry + (cs[:, 0:1] - sp0)

    def cond(state):
        kb, go = state
        return jnp.logical_and(kb >= 0, go > 0)

    def two_steps(i, _):
        for half in range(2):
            s = 2 * i + half
            t = jnp.maximum(s - 2, 0)
            stages(half, jnp.minimum(s, last), t)
            for u in range(SB_UNITS):
                def body(state, t=t, u=u):
                    kb, _ = state
                    block(stack_heads(t, u), kb, u)
                    go = (jnp.max(carry_ref[u]) >= SB_STOP).astype(jnp.int32)
                    return kb - 1, go

                lax.while_loop(cond, body, (t - 2, go_ref[u]))
                rows = pl.ds(pl.multiple_of(t * SB_T, SB_T), SB_T)
                o_ref[rows, cols(u)] = jnp.where(
                    lane < HEAD_DIM, acc_ref[u, :SB_T, :], acc_ref[u, SB_T:, :]).astype(BF16)
        return 0

    lax.fori_loop(0, (last + 1 + 2) // 2, two_steps, 0)


def _sb_tri():
    j = np.arange(SB_T)[:, None]
    s = np.arange(SB_T)[None, :]
    return jnp.asarray(np.where(j > s, -1.0, 0.0).astype(np.float32), BF16)


def _sb_attention(proj):
    wide = SB_UNITS * LANES
    tiles = SEQ // SB_T
    U = SB_UNITS
    return pl.pallas_call(
        _sb_kernel,
        grid=(SB_WIDTH // wide,),
        in_specs=[
            pl.BlockSpec((SEQ, wide), lambda p: (0, OFF_Q_SB // wide + p)),
            pl.BlockSpec((SEQ, wide), lambda p: (0, OFF_K_SB // wide + p)),
            pl.BlockSpec((SEQ, wide), lambda p: (0, OFF_V_SB // wide + p)),
            pl.BlockSpec((SB_T, SB_T), lambda p: (0, 0)),
        ],
        out_specs=pl.BlockSpec((SEQ, wide), lambda p: (0, p)),
        out_shape=jax.ShapeDtypeStruct((SEQ, SB_WIDTH), BF16),
        scratch_shapes=[
            pltpu.VMEM((U, 2, 2, 2 * SB_T, SB_T), BF16),
            pltpu.VMEM((U, 2, 2 * SB_T, SB_H), F32),
            pltpu.VMEM((U, 2, SB_T, SB_H), F32),
            pltpu.VMEM((U, 2, 2 * SB_T, SB_T), F32),
            pltpu.VMEM((U, 2, 2, 2 * SB_T, 1), F32),
            pltpu.VMEM((U, 2, 2, 2 * SB_T, SB_T), BF16),
            pltpu.VMEM((U, 2, 2 * SB_T, 1), F32),
            pltpu.SMEM((U,), jnp.int32),
            pltpu.VMEM((U, 2 * SB_T, LANES), F32),
            pltpu.VMEM((U, 2 * SB_T, 1), F32),
        ],
        compiler_params=pltpu.CompilerParams(
            dimension_semantics=("arbitrary",), vmem_limit_bytes=VMEM_LIMIT),
        name="sb_attention",
    )(proj, proj, proj, _sb_tri())


def _bucket_table():
    qi = np.arange(BLOCK)[:, None]
    ci = np.arange(2 * BLOCK)[None, :]
    dist = qi + BLOCK - ci
    n = np.maximum(dist, 0)
    max_exact = N_BUCKETS // 2
    nf = np.maximum(n, 1).astype(np.float64)
    val = np.log(nf / max_exact) / math.log(MAX_DISTANCE / max_exact) * (N_BUCKETS - max_exact)
    in_window = (dist >= 0) & (dist < WINDOW)
    frac = val - np.floor(val)
    chk = in_window & (n > max_exact)
    assert np.all((frac[chk] > 1e-3) & (frac[chk] < 1 - 1e-3))
    large = np.minimum(max_exact + val.astype(np.int64), N_BUCKETS - 1)
    bucket = np.where(n < max_exact, n, large)
    later = np.where(in_window, bucket, -1)
    assert all(np.array_equal(later[r], np.roll(later[0], r)) for r in range(BLOCK))
    return jnp.asarray(np.broadcast_to(later[0:1], (8, 2 * BLOCK)).astype(np.int32))


def _bias_kernel(rb_ref, bucket_ref, o_ref):
    b = bucket_ref[...]
    hits = [b == k for k in range(N_BUCKETS)]
    col = lax.broadcasted_iota(jnp.int32, (BLOCK, 2 * BLOCK), 1)
    for h in range(SW_HEADS):
        row0 = jnp.full(b.shape, NEG_BIG, F32)
        for k in range(N_BUCKETS):
            row0 = jnp.where(hits[k], rb_ref[k, h], row0)
        tile = pltpu.roll(jnp.concatenate([row0] * (BLOCK // 8), axis=0), 0, 1,
                          stride=1, stride_axis=0)
        o_ref[1, h] = tile
        o_ref[0, h] = jnp.where(col >= BLOCK, tile, NEG_BIG)


def _bias_table(rel_bias):
    return pl.pallas_call(
        _bias_kernel,
        in_specs=[
            pl.BlockSpec(memory_space=pltpu.SMEM),
            pl.BlockSpec(memory_space=pltpu.VMEM),
        ],
        out_specs=pl.BlockSpec(memory_space=pltpu.VMEM),
        out_shape=jax.ShapeDtypeStruct((2, SW_HEADS, BLOCK, 2 * BLOCK), F32),
        name="t5_bias",
    )(rel_bias.astype(F32), _bucket_table())


def _sw_kernel(sink_ref, qa_ref, qb_ref, k0_ref, k1_ref, k2_ref, v0_ref, v1_ref, v2_ref, bias_ref,
               o_ref, p_ref, st_ref):
    g = pl.program_id(0)

    @pl.when(g == 0)
    def _():
        p_ref[...] = jnp.zeros_like(p_ref)
        st_ref[...] = jnp.zeros_like(st_ref)

    lane = lax.broadcasted_iota(jnp.int32, (BLOCK, LANES), 1)
    ones = jnp.ones((2 * BLOCK, LANES), BF16)
    scale = jnp.asarray(SCALE, BF16)
    pairs_per_kv = SW_HEADS // SW_KV_HEADS // 2
    dims = (((1,), (1,)), ((), ()))

    def swap_halves(t):
        return jnp.concatenate([t[:, HEAD_DIM:], t[:, :HEAD_DIM]], axis=1)

    def on_side(kv):
        return (lane >= HEAD_DIM) if kv % 2 else (lane < HEAD_DIM)

    def kv_cols(kv):
        return slice((kv // 2) * LANES, (kv // 2 + 1) * LANES)

    def stages(cur, q_ref, kp_ref, kc_ref, vp_ref, vc_ref, out_rows):
        old = 1 - cur
        k = jnp.concatenate([kp_ref[...], kc_ref[...]], axis=0)
        v = jnp.concatenate([vp_ref[...], vc_ref[...]], axis=0)
        logits, heads = [], []
        for kv in range(SW_KV_HEADS):
            side = kv % 2
            same, other, hs, ho = [], [], [], []
            for pp in range(pairs_per_kv):
                p = kv * pairs_per_kv + pp
                q_pair = q_ref[:, p * LANES:(p + 1) * LANES] * scale
                zero = jnp.zeros_like(q_pair)
                same.append(jnp.where(on_side(kv), q_pair, zero))
                other.append(jnp.where(on_side(kv), swap_halves(q_pair), zero))
                hs.append(2 * p + side)
                ho.append(2 * p + 1 - side)
            lhs = jnp.concatenate(same + other, axis=0)
            logits.append(lax.dot_general(lhs, k[:, kv_cols(kv)], dims,
                                          preferred_element_type=F32))
            heads.append(hs + ho)
        results = []
        for kv in range(SW_KV_HEADS):
            v_pair = v[:, kv_cols(kv)]
            half = pairs_per_kv * BLOCK
            r_same = jnp.dot(p_ref[old, kv, :half, :], jnp.concatenate([v_pair, ones], axis=1),
                             preferred_element_type=F32)
            r_other = jnp.dot(p_ref[old, kv, half:, :],
                              jnp.concatenate([swap_halves(v_pair), ones], axis=1),
                              preferred_element_type=F32)
            results.append((r_same, r_other))

        for kv in range(SW_KV_HEADS):
            maxes = []
            for i, h in enumerate(heads[kv]):
                lg = logits[kv][i * BLOCK:(i + 1) * BLOCK, :] + bias_ref[0, h]
                m = jnp.maximum(jnp.max(lg, axis=-1, keepdims=True), sink_ref[h])
                p_ref[cur, kv, i * BLOCK:(i + 1) * BLOCK, :] = jnp.exp(lg - m).astype(BF16)
                maxes.append(m)
            for pp in range(pairs_per_kv):
                h_same, h_other = heads[kv][pp], heads[kv][pairs_per_kv + pp]
                m = jnp.where(on_side(kv), maxes[pp], maxes[pairs_per_kv + pp])
                sink = jnp.where(on_side(kv), sink_ref[h_same], sink_ref[h_other])
                st_ref[cur, kv * pairs_per_kv + pp] = jnp.exp(sink - m)

        for kv in range(SW_KV_HEADS):
            r_same, r_other = results[kv]
            for pp in range(pairs_per_kv):
                p = kv * pairs_per_kv + pp
                rows = slice(pp * BLOCK, (pp + 1) * BLOCK)
                num = jnp.where(on_side(kv), r_same[rows, :LANES], r_other[rows, :LANES])
                den = jnp.where(on_side(kv), r_same[rows, LANES:], r_other[rows, LANES:])
                o_ref[out_rows, p * LANES:(p + 1) * LANES] = (
                    num / (den + st_ref[old, p])).astype(BF16)

    @pl.when(g >= 0)
    def _():
        stages(1, qa_ref, k0_ref, k1_ref, v0_ref, v1_ref, slice(0, BLOCK))

    @pl.when(g >= 0)
    def _():
        stages(0, qb_ref, k1_ref, k2_ref, v1_ref, v2_ref, slice(BLOCK, 2 * BLOCK))


def _sw_attention(proj, sinks, bias):
    kq = OFF_Q_SW // SW_WIDTH
    kk = OFF_K_SW // SW_KV_WIDTH
    kv = OFF_V_SW // SW_KV_WIDTH
    last = SEQ // BLOCK - 1
    blk = lambda n: jnp.clip(n, 0, last)
    q_spec = lambda off: pl.BlockSpec((BLOCK, SW_WIDTH), lambda g: (blk(2 * g + off), kq))
    k_spec = lambda off: pl.BlockSpec((BLOCK, SW_KV_WIDTH), lambda g: (blk(2 * g + off), kk))
    v_spec = lambda off: pl.BlockSpec((BLOCK, SW_KV_WIDTH), lambda g: (blk(2 * g + off), kv))
    return pl.pallas_call(
        _sw_kernel,
        grid=((last + 1) // 2 + 1,),
        in_specs=[
            pl.BlockSpec(memory_space=pltpu.SMEM),
            q_spec(-1), q_spec(0),
            k_spec(-2), k_spec(-1), k_spec(0),
            v_spec(-3), v_spec(-2), v_spec(-1),
            pl.BlockSpec((1, SW_HEADS, BLOCK, 2 * BLOCK), lambda g: (jnp.minimum(g, 1), 0, 0, 0)),
        ],
        out_specs=pl.BlockSpec((2 * BLOCK, SW_WIDTH), lambda g: (jnp.maximum(g - 1, 0), 0)),
        out_shape=jax.ShapeDtypeStruct((SEQ, SW_WIDTH), BF16),
        scratch_shapes=[
            pltpu.VMEM((2, SW_KV_HEADS, SW_HEADS // SW_KV_HEADS * BLOCK, 2 * BLOCK), BF16),
            pltpu.VMEM((2, SW_HEADS // 2, BLOCK, LANES), F32),
        ],
        compiler_params=pltpu.CompilerParams(
            dimension_semantics=("arbitrary",), vmem_limit_bytes=VMEM_LIMIT),
        name="sw_attention",
    )(sinks.astype(F32), proj, proj, proj, proj, proj, proj, proj, proj, bias)


def _rms(x, g):
    ms = jnp.mean(x * x, axis=-1, keepdims=True)
    return (x * lax.rsqrt(ms + RMS_EPS)) * g


def _silu(z):
    return z * (1.0 / (1.0 + jnp.exp(-z)))


def _out_kernel(osb_ref, osw_ref, zsb_ref, zswa_ref, zswb_ref, gsb_ref, gsw_ref, gpost_ref,
                w_hbm, x_ref, o_ref, w_ref, stage_ref, sem):
    @pl.when(pl.program_id(0) == 0)
    def _():
        _cast_weights(w_hbm, w_ref, stage_ref, sem)

    chunk = OUT_TM // OUT_CHUNKS
    ys = []
    for c in range(OUT_CHUNKS):
        rows = pl.ds(c * chunk, chunk)
        a_sb = _rms(osb_ref[rows, :].astype(F32), gsb_ref[...]) * zsb_ref[rows, :].astype(F32)
        g_sw = jnp.concatenate([zswa_ref[rows, :], zswb_ref[rows, :]], axis=1).astype(F32)
        a_sw = _rms(osw_ref[rows, :].astype(F32), gsw_ref[...]) * g_sw
        y = jnp.dot(a_sb.astype(BF16), w_ref[:SB_WIDTH, :], preferred_element_type=F32)
        ys.append(y + jnp.dot(a_sw.astype(BF16), w_ref[SB_WIDTH:, :],
                              preferred_element_type=F32))
    for c in range(OUT_CHUNKS):
        rows = pl.ds(c * chunk, chunk)
        o_ref[rows, :] = x_ref[rows, :] + _rms(ys[c], gpost_ref[...])


def _outproj(o_sb, o_sw, proj, gn_sb, gn_sw, norm_post, w_out, x2):
    half = SW_WIDTH // 2
    return pl.pallas_call(
        _out_kernel,
        grid=(SEQ // OUT_TM,),
        in_specs=[
            pl.BlockSpec((OUT_TM, SB_WIDTH), lambda i: (i, 0)),
            pl.BlockSpec((OUT_TM, SW_WIDTH), lambda i: (i, 0)),
            pl.BlockSpec((OUT_TM, SB_WIDTH), lambda i: (i, OFF_Z_SB // SB_WIDTH)),
            pl.BlockSpec((OUT_TM, half), lambda i: (i, OFF_Z_SW // half)),
            pl.BlockSpec((OUT_TM, half), lambda i: (i, OFF_Z_SW // half + 1)),
            pl.BlockSpec((1, SB_WIDTH), lambda i: (0, 0)),
            pl.BlockSpec((1, SW_WIDTH), lambda i: (0, 0)),
            pl.BlockSpec((1, D_MODEL), lambda i: (0, 0)),
            pl.BlockSpec(memory_space=pl.ANY),
            pl.BlockSpec((OUT_TM, D_MODEL), lambda i: (i, 0)),
        ],
        out_specs=pl.BlockSpec((OUT_TM, D_MODEL), lambda i: (i, 0)),
        out_shape=jax.ShapeDtypeStruct((SEQ, D_MODEL), F32),
        scratch_shapes=[
            pltpu.VMEM((D_MIX, D_MODEL), BF16),
            pltpu.VMEM((CAST_SLOTS, OUT_CAST_ROWS, D_MODEL), F32),
            pltpu.SemaphoreType.DMA((CAST_SLOTS,)),
        ],
        compiler_params=pltpu.CompilerParams(
            dimension_semantics=("arbitrary",), vmem_limit_bytes=VMEM_LIMIT),
        name="outproj",
    )(o_sb, o_sw, proj, proj, proj, gn_sb, gn_sw, norm_post, w_out, x2)


def kernel(x, w_in, w_out, norm_pre, norm_post, gn_sb, gn_sw, sinks, rel_bias):
    assert x.shape == (1, SEQ, D_MODEL) and w_in.shape == (1, D_MODEL, D_IN_PROJ)
    assert OFF_Z_SW % (SW_WIDTH // 2) == 0 and OFF_Q_SW % SW_WIDTH == 0
    x2 = x.reshape(SEQ, D_MODEL)
    proj = _inproj(x2, norm_pre.astype(F32), w_in[0].astype(F32))
    o_sb = _sb_attention(proj)
    bias = _bias_table(rel_bias)
    o_sw = _sw_attention(proj, sinks[0], bias)
    out = _outproj(o_sb, o_sw, proj, gn_sb.astype(F32), gn_sw.astype(F32), norm_post.astype(F32),
                   w_out[0].astype(F32), x2)
    return out.reshape(1, SEQ, D_MODEL)
```

```python
import math

import numpy as np
import jax
import jax.numpy as jnp
from jax import lax
from jax.experimental import pallas as pl
from jax.experimental.pallas import tpu as pltpu

D_MODEL = 2048
SEQ = 16384
HEAD_DIM = 64
SB_HEADS = 16
SW_HEADS = 16
SW_KV_HEADS = 4
SB_WIDTH = SB_HEADS * HEAD_DIM
SW_WIDTH = SW_HEADS * HEAD_DIM
SW_KV_WIDTH = SW_KV_HEADS * HEAD_DIM
D_MIX = SB_WIDTH + SW_WIDTH
D_IN_PROJ = 4 * SB_WIDTH + 2 * SW_WIDTH + 2 * SW_KV_WIDTH
WINDOW = 128
BLOCK = 128
N_BUCKETS = 32
MAX_DISTANCE = 128
RMS_EPS = 1e-6
SCALE = HEAD_DIM ** -0.5

OFF_Q_SB = 0
OFF_K_SB = SB_WIDTH
OFF_V_SB = 2 * SB_WIDTH
OFF_Z_SB = 3 * SB_WIDTH
OFF_Q_SW = 4 * SB_WIDTH
OFF_K_SW = OFF_Q_SW + SW_WIDTH
OFF_V_SW = OFF_K_SW + SW_KV_WIDTH
OFF_Z_SW = OFF_V_SW + SW_KV_WIDTH

LANES = 128
VMEM_LIMIT = 56 * 1024 * 1024

PROJ_TM = 512
PROJ_TN = 3328
CAST_ROWS = 16
OUT_CAST_ROWS = 32
CAST_SLOTS = 8
SB_T = 256
SB_H = SB_T // 2
OUT_TM = 512
OUT_CHUNKS = 4
NEG_BIG = -1e30
SB_STOP = -110.0

F32 = jnp.float32
BF16 = jnp.bfloat16


def _cast_weights(w_hbm, w_ref, stage_ref, sem):
    slots, rows = stage_ref.shape[0], stage_ref.shape[1]
    n_chunks = w_hbm.shape[0] // rows
    ahead = slots - 1

    def chunk_copy(c):
        slot = c % slots
        src = w_hbm.at[pl.ds(pl.multiple_of(c * rows, rows), rows), :]
        return pltpu.make_async_copy(src, stage_ref.at[slot], sem.at[slot])

    for c in range(ahead):
        chunk_copy(c).start()

    def body(c, _):
        @pl.when(c + ahead < n_chunks)
        def _():
            chunk_copy(c + ahead).start()

        chunk_copy(c).wait()
        w_ref[pl.ds(pl.multiple_of(c * rows, rows), rows), :] = stage_ref[c % slots].astype(BF16)
        return 0

    lax.fori_loop(0, n_chunks, body, 0)


def _inproj_kernel(x_ref, g_ref, w_hbm, o_ref, w_ref, stage_ref, sem):
    @pl.when(pl.program_id(0) == 0)
    def _():
        _cast_weights(w_hbm, w_ref, stage_ref, sem)

    x = x_ref[...]
    ms = jnp.mean(x * x, axis=-1, keepdims=True)
    xn = ((x * lax.rsqrt(ms + RMS_EPS)) * g_ref[...]).astype(BF16)
    gates = ((OFF_Z_SB, OFF_Z_SB + SB_WIDTH), (OFF_Z_SW, OFF_Z_SW + SW_WIDTH))
    for j in reversed(range(D_IN_PROJ // PROJ_TN)):
        lo, hi = j * PROJ_TN, (j + 1) * PROJ_TN
        res = jnp.dot(xn, w_ref[:, lo:hi], preferred_element_type=F32)
        cuts = sorted({lo, hi} | {min(max(c, lo), hi) for g in gates for c in g})
        for a, b in zip(cuts[:-1], cuts[1:]):
            part = res[:, a - lo:b - lo]
            if any(g0 <= a and b <= g1 for g0, g1 in gates):
                part = _silu(part)
            o_ref[:, a:b] = part.astype(BF16)


def _inproj(x2, g, w):
    return pl.pallas_call(
        _inproj_kernel,
        grid=(SEQ // PROJ_TM,),
        in_specs=[
            pl.BlockSpec((PROJ_TM, D_MODEL), lambda i: (i, 0)),
            pl.BlockSpec((1, D_MODEL), lambda i: (0, 0)),
            pl.BlockSpec(memory_space=pl.ANY),
        ],
        out_specs=pl.BlockSpec((PROJ_TM, D_IN_PROJ), lambda i: (i, 0)),
        out_shape=jax.ShapeDtypeStruct((SEQ, D_IN_PROJ), BF16),
        scratch_shapes=[
            pltpu.VMEM((D_MODEL, D_IN_PROJ), BF16),
            pltpu.VMEM((CAST_SLOTS, CAST_ROWS, D_IN_PROJ), F32),
            pltpu.SemaphoreType.DMA((CAST_SLOTS,)),
        ],
        compiler_params=pltpu.CompilerParams(
            dimension_semantics=("arbitrary",), vmem_limit_bytes=VMEM_LIMIT),
        name="inproj",
    )(x2, g, w)


def _sb_kernel(q_ref, k_ref, v_ref, tri_ref, o_ref,
               hi_ref, lbl_ref, lbr_ref, lbp_ref, sp0_ref, w_ref, carry_w_ref, go_ref,
               acc_ref, carry_ref):
    last = SEQ // SB_T - 1
    for ref in (hi_ref, lbl_ref, lbr_ref, lbp_ref, sp0_ref, w_ref, carry_w_ref):
        ref[...] = jnp.zeros_like(ref)

    lane = lax.broadcasted_iota(jnp.int32, (SB_T, LANES), 1)

    def stack_heads(tile):
        start = pl.multiple_of(tile * SB_T, SB_T)
        q = q_ref[pl.ds(start, SB_T), :] * jnp.asarray(SCALE, BF16)
        zero = jnp.zeros_like(q)
        return jnp.concatenate([jnp.where(lane < HEAD_DIM, q, zero),
                                jnp.where(lane >= HEAD_DIM, q, zero)], axis=0)

    tri = tri_ref[...]
    r_h = lax.broadcasted_iota(jnp.int32, (SB_H, SB_H), 0)
    c_h = lax.broadcasted_iota(jnp.int32, (SB_H, SB_H), 1)
    before_h = c_h < r_h
    zeros_h = jnp.zeros((SB_H, SB_H), BF16)

    def qk(qs, kb):
        start = pl.multiple_of(kb * SB_T, SB_T)
        kblk = k_ref[pl.ds(start, SB_T), :]
        return lax.dot_general(qs, kblk, (((1,), (1,)), ((), ())), preferred_element_type=F32)

    def softplus_parts(z):
        neg_abs = lax.bitcast_convert_type(
            lax.bitcast_convert_type(z, jnp.int32) | jnp.int32(-2 ** 31), F32)
        sp = jnp.maximum(z, 0.0) + jnp.log(1.0 + jnp.exp(neg_abs))
        return sp.astype(BF16), z - sp, sp[:, 0:1]

    def cut(z):
        return jnp.where(before_h, z, NEG_BIG)

    def quarters(a):
        left = a[:, :SB_H]
        right = jnp.concatenate([a[SB_H:SB_T, SB_H:], a[SB_T + SB_H:, SB_H:]], axis=0)
        return left, right

    def unquarter(left, right):
        right_full = jnp.concatenate([zeros_h, right[:SB_H], zeros_h, right[SB_H:]], axis=0)
        return jnp.concatenate([left, right_full], axis=1)

    def cumsum(hi):
        return jnp.dot(hi, tri, preferred_element_type=F32)

    def values(kb, w, v_valid=None):
        start = pl.multiple_of(kb * SB_T, SB_T)
        vblk = v_ref[pl.ds(start, SB_T), :]
        if v_valid is not None:
            vblk = jnp.where(v_valid, vblk, jnp.zeros_like(vblk))
        return jnp.dot(w, vblk, preferred_element_type=F32)

    def stages(cur, n, t):
        old = 1 - cur
        qs_n = stack_heads(n)
        z_d = qk(qs_n, n)
        z_p = qk(qs_n, jnp.maximum(n - 1, 0))
        cs_d = cumsum(hi_ref[old, 0])
        cs_p = cumsum(hi_ref[old, 1])
        acc_ref[...] = (values(t, w_ref[old, 0])
                        + values(jnp.maximum(t - 1, 0), w_ref[old, 1], v_valid=t > 0))
        carry_t = carry_w_ref[old]
        carry_ref[...] = carry_t
        go_ref[0] = (jnp.max(carry_t) >= SB_STOP).astype(jnp.int32)

        zl, zr = quarters(z_d)
        zl = jnp.concatenate([cut(zl[:SB_H]), zl[SB_H:SB_T],
                              cut(zl[SB_T:SB_T + SB_H]), zl[SB_T + SB_H:]], axis=0)
        zr = jnp.concatenate([cut(zr[:SB_H]), cut(zr[SB_H:])], axis=0)
        hi_l, lbl_ref[cur], sp0_ref[cur, 0] = softplus_parts(zl)
        hi_r, lbr_ref[cur], _ = softplus_parts(zr)
        hi_ref[cur, 0] = unquarter(hi_l, hi_r)
        hi_ref[cur, 1], lbp_ref[cur], sp0_ref[cur, 1] = softplus_parts(z_p)

        cs_l, cs_r = quarters(cs_d)
        w_l = jnp.exp(lbl_ref[old] + cs_l).astype(BF16)
        w_r = jnp.exp(lbr_ref[old] + cs_r).astype(BF16)
        carry_d = cs_d[:, 0:1] - sp0_ref[old, 0]
        w_ref[cur, 0] = unquarter(w_l, w_r)
        w_ref[cur, 1] = jnp.exp(lbp_ref[old] + cs_p + carry_d).astype(BF16)
        carry_w_ref[cur] = carry_d + (cs_p[:, 0:1] - sp0_ref[old, 1])

    def block(qs, kb):
        hi, log_beta, sp0 = softplus_parts(qk(qs, kb))
        cs = cumsum(hi)
        carry = carry_ref[...]
        w = jnp.exp(log_beta + cs + carry).astype(BF16)
        acc_ref[...] += values(kb, w)
        carry_ref[...] = carry + (cs[:, 0:1] - sp0)

    def cond(state):
        kb, go = state
        return jnp.logical_and(kb >= 0, go > 0)

    def two_steps(i, _):
        for half in range(2):
            s = 2 * i + half
            t = jnp.maximum(s - 2, 0)
            stages(half, jnp.minimum(s, last), t)

            def body(state, t=t):
                kb, _ = state
                block(stack_heads(t), kb)
                go = (jnp.max(carry_ref[...]) >= SB_STOP).astype(jnp.int32)
                return kb - 1, go

            lax.while_loop(cond, body, (t - 2, go_ref[0]))
            rows = pl.ds(pl.multiple_of(t * SB_T, SB_T), SB_T)
            o_ref[rows, :] = jnp.where(lane < HEAD_DIM, acc_ref[:SB_T, :],
                                       acc_ref[SB_T:, :]).astype(BF16)
        return 0

    lax.fori_loop(0, (last + 1 + 2) // 2, two_steps, 0)


def _sb_tri():
    j = np.arange(SB_T)[:, None]
    s = np.arange(SB_T)[None, :]
    return jnp.asarray(np.where(j > s, -1.0, 0.0).astype(np.float32), BF16)


def _sb_attention(proj):
    pair = lambda off: pl.BlockSpec((SEQ, LANES), lambda p: (0, off // LANES + p))
    return pl.pallas_call(
        _sb_kernel,
        grid=(SB_WIDTH // LANES,),
        in_specs=[pair(OFF_Q_SB), pair(OFF_K_SB), pair(OFF_V_SB),
                  pl.BlockSpec((SB_T, SB_T), lambda p: (0, 0))],
        out_specs=pl.BlockSpec((SEQ, LANES), lambda p: (0, p)),
        out_shape=jax.ShapeDtypeStruct((SEQ, SB_WIDTH), BF16),
        scratch_shapes=[
            pltpu.VMEM((2, 2, 2 * SB_T, SB_T), BF16),
            pltpu.VMEM((2, 2 * SB_T, SB_H), F32),
            pltpu.VMEM((2, SB_T, SB_H), F32),
            pltpu.VMEM((2, 2 * SB_T, SB_T), F32),
            pltpu.VMEM((2, 2, 2 * SB_T, 1), F32),
            pltpu.VMEM((2, 2, 2 * SB_T, SB_T), BF16),
            pltpu.VMEM((2, 2 * SB_T, 1), F32),
            pltpu.SMEM((1,), jnp.int32),
            pltpu.VMEM((2 * SB_T, LANES), F32),
            pltpu.VMEM((2 * SB_T, 1), F32),
        ],
        compiler_params=pltpu.CompilerParams(
            dimension_semantics=("arbitrary",), vmem_limit_bytes=VMEM_LIMIT),
        name="sb_attention",
    )(proj, proj, proj, _sb_tri())


def _bucket_table():
    qi = np.arange(BLOCK)[:, None]
    ci = np.arange(2 * BLOCK)[None, :]
    dist = qi + BLOCK - ci
    n = np.maximum(dist, 0)
    max_exact = N_BUCKETS // 2
    nf = np.maximum(n, 1).astype(np.float64)
    val = np.log(nf / max_exact) / math.log(MAX_DISTANCE / max_exact) * (N_BUCKETS - max_exact)
    in_window = (dist >= 0) & (dist < WINDOW)
    frac = val - np.floor(val)
    chk = in_window & (n > max_exact)
    assert np.all((frac[chk] > 1e-3) & (frac[chk] < 1 - 1e-3))
    large = np.minimum(max_exact + val.astype(np.int64), N_BUCKETS - 1)
    bucket = np.where(n < max_exact, n, large)
    later = np.where(in_window, bucket, -1)
    assert all(np.array_equal(later[r], np.roll(later[0], r)) for r in range(BLOCK))
    return jnp.asarray(np.broadcast_to(later[0:1], (8, 2 * BLOCK)).astype(np.int32))


def _bias_kernel(rb_ref, bucket_ref, o_ref):
    b = bucket_ref[...]
    hits = [b == k for k in range(N_BUCKETS)]
    col = lax.broadcasted_iota(jnp.int32, (BLOCK, 2 * BLOCK), 1)
    for h in range(SW_HEADS):
        row0 = jnp.full(b.shape, NEG_BIG, F32)
        for k in range(N_BUCKETS):
            row0 = jnp.where(hits[k], rb_ref[k, h], row0)
        tile = pltpu.roll(jnp.concatenate([row0] * (BLOCK // 8), axis=0), 0, 1,
                          stride=1, stride_axis=0)
        o_ref[1, h] = tile
        o_ref[0, h] = jnp.where(col >= BLOCK, tile, NEG_BIG)


def _bias_table(rel_bias):
    return pl.pallas_call(
        _bias_kernel,
        in_specs=[
            pl.BlockSpec(memory_space=pltpu.SMEM),
            pl.BlockSpec(memory_space=pltpu.VMEM),
        ],
        out_specs=pl.BlockSpec(memory_space=pltpu.VMEM),
        out_shape=jax.ShapeDtypeStruct((2, SW_HEADS, BLOCK, 2 * BLOCK), F32),
        name="t5_bias",
    )(rel_bias.astype(F32), _bucket_table())


def _sw_kernel(sink_ref, qa_ref, qb_ref, k0_ref, k1_ref, k2_ref, v0_ref, v1_ref, v2_ref, bias_ref,
               o_ref, p_ref, st_ref):
    g = pl.program_id(0)

    @pl.when(g == 0)
    def _():
        p_ref[...] = jnp.zeros_like(p_ref)
        st_ref[...] = jnp.zeros_like(st_ref)

    lane = lax.broadcasted_iota(jnp.int32, (BLOCK, LANES), 1)
    ones = jnp.ones((2 * BLOCK, LANES), BF16)
    scale = jnp.asarray(SCALE, BF16)
    pairs_per_kv = SW_HEADS // SW_KV_HEADS // 2
    dims = (((1,), (1,)), ((), ()))

    def swap_halves(t):
        return jnp.concatenate([t[:, HEAD_DIM:], t[:, :HEAD_DIM]], axis=1)

    def on_side(kv):
        return (lane >= HEAD_DIM) if kv % 2 else (lane < HEAD_DIM)

    def kv_cols(kv):
        return slice((kv // 2) * LANES, (kv // 2 + 1) * LANES)

    def stages(cur, q_ref, kp_ref, kc_ref, vp_ref, vc_ref, out_rows):
        old = 1 - cur
        k = jnp.concatenate([kp_ref[...], kc_ref[...]], axis=0)
        v = jnp.concatenate([vp_ref[...], vc_ref[...]], axis=0)
        logits, heads = [], []
        for kv in range(SW_KV_HEADS):
            side = kv % 2
            same, other, hs, ho = [], [], [], []
            for pp in range(pairs_per_kv):
                p = kv * pairs_per_kv + pp
                q_pair = q_ref[:, p * LANES:(p + 1) * LANES] * scale
                zero = jnp.zeros_like(q_pair)
                same.append(jnp.where(on_side(kv), q_pair, zero))
                other.append(jnp.where(on_side(kv), swap_halves(q_pair), zero))
                hs.append(2 * p + side)
                ho.append(2 * p + 1 - side)
            lhs = jnp.concatenate(same + other, axis=0)
            logits.append(lax.dot_general(lhs, k[:, kv_cols(kv)], dims,
                                          preferred_element_type=F32))
            heads.append(hs + ho)
        results = []
        for kv in range(SW_KV_HEADS):
            v_pair = v[:, kv_cols(kv)]
            half = pairs_per_kv * BLOCK
            r_same = jnp.dot(p_ref[old, kv, :half, :], jnp.concatenate([v_pair, ones], axis=1),
                             preferred_element_type=F32)
            r_other = jnp.dot(p_ref[old, kv, half:, :],
                              jnp.concatenate([swap_halves(v_pair), ones], axis=1),
                              preferred_element_type=F32)
            results.append((r_same, r_other))

        for kv in range(SW_KV_HEADS):
            maxes = []
            for i, h in enumerate(heads[kv]):
                lg = logits[kv][i * BLOCK:(i + 1) * BLOCK, :] + bias_ref[0, h]
                m = jnp.maximum(jnp.max(lg, axis=-1, keepdims=True), sink_ref[h])
                p_ref[cur, kv, i * BLOCK:(i + 1) * BLOCK, :] = jnp.exp(lg - m).astype(BF16)
                maxes.append(m)
            for pp in range(pairs_per_kv):
                h_same, h_other = heads[kv][pp], heads[kv][pairs_per_kv + pp]
                m = jnp.where(on_side(kv), maxes[pp], maxes[pairs_per_kv + pp])
                sink = jnp.where(on_side(kv), sink_ref[h_same], sink_ref[h_other])
                st_ref[cur, kv * pairs_per_kv + pp] = jnp.exp(sink - m)

        for kv in range(SW_KV_HEADS):
            r_same, r_other = results[kv]
            for pp in range(pairs_per_kv):
                p = kv * pairs_per_kv + pp
                rows = slice(pp * BLOCK, (pp + 1) * BLOCK)
                num = jnp.where(on_side(kv), r_same[rows, :LANES], r_other[rows, :LANES])
                den = jnp.where(on_side(kv), r_same[rows, LANES:], r_other[rows, LANES:])
                o_ref[out_rows, p * LANES:(p + 1) * LANES] = (
                    num / (den + st_ref[old, p])).astype(BF16)

    @pl.when(g >= 0)
    def _():
        stages(1, qa_ref, k0_ref, k1_ref, v0_ref, v1_ref, slice(0, BLOCK))

    @pl.when(g >= 0)
    def _():
        stages(0, qb_ref, k1_ref, k2_ref, v1_ref, v2_ref, slice(BLOCK, 2 * BLOCK))


def _sw_attention(proj, sinks, bias):
    kq = OFF_Q_SW // SW_WIDTH
    kk = OFF_K_SW // SW_KV_WIDTH
    kv = OFF_V_SW // SW_KV_WIDTH
    last = SEQ // BLOCK - 1
    blk = lambda n: jnp.clip(n, 0, last)
    q_spec = lambda off: pl.BlockSpec((BLOCK, SW_WIDTH), lambda g: (blk(2 * g + off), kq))
    k_spec = lambda off: pl.BlockSpec((BLOCK, SW_KV_WIDTH), lambda g: (blk(2 * g + off), kk))
    v_spec = lambda off: pl.BlockSpec((BLOCK, SW_KV_WIDTH), lambda g: (blk(2 * g + off), kv))
    return pl.pallas_call(
        _sw_kernel,
        grid=((last + 1) // 2 + 1,),
        in_specs=[
            pl.BlockSpec(memory_space=pltpu.SMEM),
            q_spec(-1), q_spec(0),
            k_spec(-2), k_spec(-1), k_spec(0),
            v_spec(-3), v_spec(-2), v_spec(-1),
            pl.BlockSpec((1, SW_HEADS, BLOCK, 2 * BLOCK), lambda g: (jnp.minimum(g, 1), 0, 0, 0)),
        ],
        out_specs=pl.BlockSpec((2 * BLOCK, SW_WIDTH), lambda g: (jnp.maximum(g - 1, 0), 0)),
        out_shape=jax.ShapeDtypeStruct((SEQ, SW_WIDTH), BF16),
        scratch_shapes=[
            pltpu.VMEM((2, SW_KV_HEADS, SW_HEADS // SW_KV_HEADS * BLOCK, 2 * BLOCK), BF16),
            pltpu.VMEM((2, SW_HEADS // 2, BLOCK, LANES), F32),
        ],
        compiler_params=pltpu.CompilerParams(
            dimension_semantics=("arbitrary",), vmem_limit_bytes=VMEM_LIMIT),
        name="sw_attention",
    )(sinks.astype(F32), proj, proj, proj, proj, proj, proj, proj, proj, bias)


def _rms(x, g):
    ms = jnp.mean(x * x, axis=-1, keepdims=True)
    return (x * lax.rsqrt(ms + RMS_EPS)) * g


def _silu(z):
    return z * (1.0 / (1.0 + jnp.exp(-z)))


def _out_kernel(osb_ref, osw_ref, zsb_ref, zswa_ref, zswb_ref, gsb_ref, gsw_ref, gpost_ref,
                w_hbm, x_ref, o_ref, w_ref, stage_ref, sem):
    @pl.when(pl.program_id(0) == 0)
    def _():
        _cast_weights(w_hbm, w_ref, stage_ref, sem)

    chunk = OUT_TM // OUT_CHUNKS
    ys = []
    for c in range(OUT_CHUNKS):
        rows = pl.ds(c * chunk, chunk)
        a_sb = _rms(osb_ref[rows, :].astype(F32), gsb_ref[...]) * zsb_ref[rows, :].astype(F32)
        g_sw = jnp.concatenate([zswa_ref[rows, :], zswb_ref[rows, :]], axis=1).astype(F32)
        a_sw = _rms(osw_ref[rows, :].astype(F32), gsw_ref[...]) * g_sw
        y = jnp.dot(a_sb.astype(BF16), w_ref[:SB_WIDTH, :], preferred_element_type=F32)
        ys.append(y + jnp.dot(a_sw.astype(BF16), w_ref[SB_WIDTH:, :],
                              preferred_element_type=F32))
    for c in range(OUT_CHUNKS):
        rows = pl.ds(c * chunk, chunk)
        o_ref[rows, :] = x_ref[rows, :] + _rms(ys[c], gpost_ref[...])


def _outproj(o_sb, o_sw, proj, gn_sb, gn_sw, norm_post, w_out, x2):
    half = SW_WIDTH // 2
    return pl.pallas_call(
        _out_kernel,
        grid=(SEQ // OUT_TM,),
        in_specs=[
            pl.BlockSpec((OUT_TM, SB_WIDTH), lambda i: (i, 0)),
            pl.BlockSpec((OUT_TM, SW_WIDTH), lambda i: (i, 0)),
            pl.BlockSpec((OUT_TM, SB_WIDTH), lambda i: (i, OFF_Z_SB // SB_WIDTH)),
            pl.BlockSpec((OUT_TM, half), lambda i: (i, OFF_Z_SW // half)),
            pl.BlockSpec((OUT_TM, half), lambda i: (i, OFF_Z_SW // half + 1)),
            pl.BlockSpec((1, SB_WIDTH), lambda i: (0, 0)),
            pl.BlockSpec((1, SW_WIDTH), lambda i: (0, 0)),
            pl.BlockSpec((1, D_MODEL), lambda i: (0, 0)),
            pl.BlockSpec(memory_space=pl.ANY),
            pl.BlockSpec((OUT_TM, D_MODEL), lambda i: (i, 0)),
        ],
        out_specs=pl.BlockSpec((OUT_TM, D_MODEL), lambda i: (i, 0)),
        out_shape=jax.ShapeDtypeStruct((SEQ, D_MODEL), F32),
        scratch_shapes=[
            pltpu.VMEM((D_MIX, D_MODEL), BF16),
            pltpu.VMEM((CAST_SLOTS, OUT_CAST_ROWS, D_MODEL), F32),
            pltpu.SemaphoreType.DMA((CAST_SLOTS,)),
        ],
        compiler_params=pltpu.CompilerParams(
            dimension_semantics=("arbitrary",), vmem_limit_bytes=VMEM_LIMIT),
        name="outproj",
    )(o_sb, o_sw, proj, proj, proj, gn_sb, gn_sw, norm_post, w_out, x2)


def kernel(x, w_in, w_out, norm_pre, norm_post, gn_sb, gn_sw, sinks, rel_bias):
    assert x.shape == (1, SEQ, D_MODEL) and w_in.shape == (1, D_MODEL, D_IN_PROJ)
    assert OFF_Z_SW % (SW_WIDTH // 2) == 0 and OFF_Q_SW % SW_WIDTH == 0
    x2 = x.reshape(SEQ, D_MODEL)
    proj = _inproj(x2, norm_pre.astype(F32), w_in[0].astype(F32))
    o_sb = _sb_attention(proj)
    bias = _bias_table(rel_bias)
    o_sw = _sw_attention(proj, sinks[0], bias)
    out = _outproj(o_sb, o_sw, proj, gn_sb.astype(F32), gn_sw.astype(F32), norm_post.astype(F32),
                   w_out[0].astype(F32), x2)
    return out.reshape(1, SEQ, D_MODEL)
```

```python
import math

import numpy as np
import jax
import jax.numpy as jnp
from jax import lax
from jax.experimental import pallas as pl
from jax.experimental.pallas import tpu as pltpu

D_MODEL = 2048
SEQ = 16384
HEAD_DIM = 64
SB_HEADS = 16
SW_HEADS = 16
SW_KV_HEADS = 4
SB_WIDTH = SB_HEADS * HEAD_DIM
SW_WIDTH = SW_HEADS * HEAD_DIM
SW_KV_WIDTH = SW_KV_HEADS * HEAD_DIM
D_MIX = SB_WIDTH + SW_WIDTH
D_IN_PROJ = 4 * SB_WIDTH + 2 * SW_WIDTH + 2 * SW_KV_WIDTH
WINDOW = 128
BLOCK = 128
N_BUCKETS = 32
MAX_DISTANCE = 128
RMS_EPS = 1e-6
SCALE = HEAD_DIM ** -0.5

OFF_Q_SB = 0
OFF_K_SB = SB_WIDTH
OFF_V_SB = 2 * SB_WIDTH
OFF_Z_SB = 3 * SB_WIDTH
OFF_Q_SW = 4 * SB_WIDTH
OFF_K_SW = OFF_Q_SW + SW_WIDTH
OFF_V_SW = OFF_K_SW + SW_KV_WIDTH
OFF_Z_SW = OFF_V_SW + SW_KV_WIDTH

LANES = 128
VMEM_LIMIT = 56 * 1024 * 1024

PROJ_TM = 512
PROJ_TN = 3328
CAST_ROWS = 16
OUT_CAST_ROWS = 32
CAST_SLOTS = 8
SB_T = 256
SB_H = SB_T // 2
OUT_TM = 512
OUT_CHUNKS = 2
NEG_BIG = -1e30
SB_STOP = -110.0

F32 = jnp.float32
BF16 = jnp.bfloat16


def _cast_weights(w_hbm, w_ref, stage_ref, sem):
    slots, rows = stage_ref.shape[0], stage_ref.shape[1]
    n_chunks = w_hbm.shape[0] // rows
    ahead = slots - 1

    def chunk_copy(c):
        slot = c % slots
        src = w_hbm.at[pl.ds(pl.multiple_of(c * rows, rows), rows), :]
        return pltpu.make_async_copy(src, stage_ref.at[slot], sem.at[slot])

    for c in range(ahead):
        chunk_copy(c).start()

    def body(c, _):
        @pl.when(c + ahead < n_chunks)
        def _():
            chunk_copy(c + ahead).start()

        chunk_copy(c).wait()
        w_ref[pl.ds(pl.multiple_of(c * rows, rows), rows), :] = stage_ref[c % slots].astype(BF16)
        return 0

    lax.fori_loop(0, n_chunks, body, 0)


def _inproj_kernel(x_ref, g_ref, w_hbm, o_ref, w_ref, stage_ref, sem):
    @pl.when(pl.program_id(0) == 0)
    def _():
        _cast_weights(w_hbm, w_ref, stage_ref, sem)

    x = x_ref[...]
    ms = jnp.mean(x * x, axis=-1, keepdims=True)
    xn = ((x * lax.rsqrt(ms + RMS_EPS)) * g_ref[...]).astype(BF16)
    gates = ((OFF_Z_SB, OFF_Z_SB + SB_WIDTH), (OFF_Z_SW, OFF_Z_SW + SW_WIDTH))
    for j in reversed(range(D_IN_PROJ // PROJ_TN)):
        lo, hi = j * PROJ_TN, (j + 1) * PROJ_TN
        res = jnp.dot(xn, w_ref[:, lo:hi], preferred_element_type=F32)
        cuts = sorted({lo, hi} | {min(max(c, lo), hi) for g in gates for c in g})
        for a, b in zip(cuts[:-1], cuts[1:]):
            part = res[:, a - lo:b - lo]
            if any(g0 <= a and b <= g1 for g0, g1 in gates):
                part = _silu(part)
            o_ref[:, a:b] = part.astype(BF16)


def _inproj(x2, g, w):
    return pl.pallas_call(
        _inproj_kernel,
        grid=(SEQ // PROJ_TM,),
        in_specs=[
            pl.BlockSpec((PROJ_TM, D_MODEL), lambda i: (i, 0)),
            pl.BlockSpec((1, D_MODEL), lambda i: (0, 0)),
            pl.BlockSpec(memory_space=pl.ANY),
        ],
        out_specs=pl.BlockSpec((PROJ_TM, D_IN_PROJ), lambda i: (i, 0)),
        out_shape=jax.ShapeDtypeStruct((SEQ, D_IN_PROJ), BF16),
        scratch_shapes=[
            pltpu.VMEM((D_MODEL, D_IN_PROJ), BF16),
            pltpu.VMEM((CAST_SLOTS, CAST_ROWS, D_IN_PROJ), F32),
            pltpu.SemaphoreType.DMA((CAST_SLOTS,)),
        ],
        compiler_params=pltpu.CompilerParams(
            dimension_semantics=("arbitrary",), vmem_limit_bytes=VMEM_LIMIT),
        name="inproj",
    )(x2, g, w)


def _sb_kernel(q_ref, k_ref, v_ref, tri_ref, o_ref,
               hi_ref, lbl_ref, lbr_ref, lbp_ref, sp0_ref, w_ref, carry_w_ref, go_ref,
               acc_ref, carry_ref):
    last = SEQ // SB_T - 1
    for ref in (hi_ref, lbl_ref, lbr_ref, lbp_ref, sp0_ref, w_ref, carry_w_ref):
        ref[...] = jnp.zeros_like(ref)

    lane = lax.broadcasted_iota(jnp.int32, (SB_T, LANES), 1)

    def stack_heads(tile):
        start = pl.multiple_of(tile * SB_T, SB_T)
        q = q_ref[pl.ds(start, SB_T), :] * jnp.asarray(SCALE, BF16)
        zero = jnp.zeros_like(q)
        return jnp.concatenate([jnp.where(lane < HEAD_DIM, q, zero),
                                jnp.where(lane >= HEAD_DIM, q, zero)], axis=0)

    tri = tri_ref[...]
    r_h = lax.broadcasted_iota(jnp.int32, (SB_H, SB_H), 0)
    c_h = lax.broadcasted_iota(jnp.int32, (SB_H, SB_H), 1)
    before_h = c_h < r_h
    zeros_h = jnp.zeros((SB_H, SB_H), BF16)

    def qk(qs, kb):
        start = pl.multiple_of(kb * SB_T, SB_T)
        kblk = k_ref[pl.ds(start, SB_T), :]
        return lax.dot_general(qs, kblk, (((1,), (1,)), ((), ())), preferred_element_type=F32)

    def softplus_parts(z):
        neg_abs = lax.bitcast_convert_type(
            lax.bitcast_convert_type(z, jnp.int32) | jnp.int32(-2 ** 31), F32)
        sp = jnp.maximum(z, 0.0) + jnp.log(1.0 + jnp.exp(neg_abs))
        return sp.astype(BF16), z - sp, sp[:, 0:1]

    def cut(z):
        return jnp.where(before_h, z, NEG_BIG)

    def quarters(a):
        left = a[:, :SB_H]
        right = jnp.concatenate([a[SB_H:SB_T, SB_H:], a[SB_T + SB_H:, SB_H:]], axis=0)
        return left, right

    def unquarter(left, right):
        right_full = jnp.concatenate([zeros_h, right[:SB_H], zeros_h, right[SB_H:]], axis=0)
        return jnp.concatenate([left, right_full], axis=1)

    def cumsum(hi):
        return jnp.dot(hi, tri, preferred_element_type=F32)

    def values(kb, w, v_valid=None):
        start = pl.multiple_of(kb * SB_T, SB_T)
        vblk = v_ref[pl.ds(start, SB_T), :]
        if v_valid is not None:
            vblk = jnp.where(v_valid, vblk, jnp.zeros_like(vblk))
        return jnp.dot(w, vblk, preferred_element_type=F32)

    def stages(cur, n, t):
        old = 1 - cur
        qs_n = stack_heads(n)
        z_d = qk(qs_n, n)
        z_p = qk(qs_n, jnp.maximum(n - 1, 0))
        cs_d = cumsum(hi_ref[old, 0])
        cs_p = cumsum(hi_ref[old, 1])
        acc_ref[...] = (values(t, w_ref[old, 0])
                        + values(jnp.maximum(t - 1, 0), w_ref[old, 1], v_valid=t > 0))
        carry_t = carry_w_ref[old]
        carry_ref[...] = carry_t
        go_ref[0] = (jnp.max(carry_t) >= SB_STOP).astype(jnp.int32)

        zl, zr = quarters(z_d)
        zl = jnp.concatenate([cut(zl[:SB_H]), zl[SB_H:SB_T],
                              cut(zl[SB_T:SB_T + SB_H]), zl[SB_T + SB_H:]], axis=0)
        zr = jnp.concatenate([cut(zr[:SB_H]), cut(zr[SB_H:])], axis=0)
        hi_l, lbl_ref[cur], sp0_ref[cur, 0] = softplus_parts(zl)
        hi_r, lbr_ref[cur], _ = softplus_parts(zr)
        hi_ref[cur, 0] = unquarter(hi_l, hi_r)
        hi_ref[cur, 1], lbp_ref[cur], sp0_ref[cur, 1] = softplus_parts(z_p)

        cs_l, cs_r = quarters(cs_d)
        w_l = jnp.exp(lbl_ref[old] + cs_l).astype(BF16)
        w_r = jnp.exp(lbr_ref[old] + cs_r).astype(BF16)
        carry_d = cs_d[:, 0:1] - sp0_ref[old, 0]
        w_ref[cur, 0] = unquarter(w_l, w_r)
        w_ref[cur, 1] = jnp.exp(lbp_ref[old] + cs_p + carry_d).astype(BF16)
        carry_w_ref[cur] = carry_d + (cs_p[:, 0:1] - sp0_ref[old, 1])

    def block(qs, kb):
        hi, log_beta, sp0 = softplus_parts(qk(qs, kb))
        cs = cumsum(hi)
        carry = carry_ref[...]
        w = jnp.exp(log_beta + cs + carry).astype(BF16)
        acc_ref[...] += values(kb, w)
        carry_ref[...] = carry + (cs[:, 0:1] - sp0)

    def cond(state):
        kb, go = state
        return jnp.logical_and(kb >= 0, go > 0)

    def two_steps(i, _):
        for half in range(2):
            s = 2 * i + half
            t = jnp.maximum(s - 2, 0)
            stages(half, jnp.minimum(s, last), t)

            def body(state, t=t):
                kb, _ = state
                block(stack_heads(t), kb)
                go = (jnp.max(carry_ref[...]) >= SB_STOP).astype(jnp.int32)
                return kb - 1, go

            lax.while_loop(cond, body, (t - 2, go_ref[0]))
            rows = pl.ds(pl.multiple_of(t * SB_T, SB_T), SB_T)
            o_ref[rows, :] = jnp.where(lane < HEAD_DIM, acc_ref[:SB_T, :],
                                       acc_ref[SB_T:, :]).astype(BF16)
        return 0

    lax.fori_loop(0, (last + 1 + 2) // 2, two_steps, 0)


def _sb_tri():
    j = np.arange(SB_T)[:, None]
    s = np.arange(SB_T)[None, :]
    return jnp.asarray(np.where(j > s, -1.0, 0.0).astype(np.float32), BF16)


def _sb_attention(proj):
    pair = lambda off: pl.BlockSpec((SEQ, LANES), lambda p: (0, off // LANES + p))
    return pl.pallas_call(
        _sb_kernel,
        grid=(SB_WIDTH // LANES,),
        in_specs=[pair(OFF_Q_SB), pair(OFF_K_SB), pair(OFF_V_SB),
                  pl.BlockSpec((SB_T, SB_T), lambda p: (0, 0))],
        out_specs=pl.BlockSpec((SEQ, LANES), lambda p: (0, p)),
        out_shape=jax.ShapeDtypeStruct((SEQ, SB_WIDTH), BF16),
        scratch_shapes=[
            pltpu.VMEM((2, 2, 2 * SB_T, SB_T), BF16),
            pltpu.VMEM((2, 2 * SB_T, SB_H), F32),
            pltpu.VMEM((2, SB_T, SB_H), F32),
            pltpu.VMEM((2, 2 * SB_T, SB_T), F32),
            pltpu.VMEM((2, 2, 2 * SB_T, 1), F32),
            pltpu.VMEM((2, 2, 2 * SB_T, SB_T), BF16),
            pltpu.VMEM((2, 2 * SB_T, 1), F32),
            pltpu.SMEM((1,), jnp.int32),
            pltpu.VMEM((2 * SB_T, LANES), F32),
            pltpu.VMEM((2 * SB_T, 1), F32),
        ],
        compiler_params=pltpu.CompilerParams(
            dimension_semantics=("arbitrary",), vmem_limit_bytes=VMEM_LIMIT),
        name="sb_attention",
    )(proj, proj, proj, _sb_tri())


def _bucket_table():
    qi = np.arange(BLOCK)[:, None]
    ci = np.arange(2 * BLOCK)[None, :]
    dist = qi + BLOCK - ci
    n = np.maximum(dist, 0)
    max_exact = N_BUCKETS // 2
    nf = np.maximum(n, 1).astype(np.float64)
    val = np.log(nf / max_exact) / math.log(MAX_DISTANCE / max_exact) * (N_BUCKETS - max_exact)
    in_window = (dist >= 0) & (dist < WINDOW)
    frac = val - np.floor(val)
    chk = in_window & (n > max_exact)
    assert np.all((frac[chk] > 1e-3) & (frac[chk] < 1 - 1e-3))
    large = np.minimum(max_exact + val.astype(np.int64), N_BUCKETS - 1)
    bucket = np.where(n < max_exact, n, large)
    later = np.where(in_window, bucket, -1)
    assert all(np.array_equal(later[r], np.roll(later[0], r)) for r in range(BLOCK))
    return jnp.asarray(np.broadcast_to(later[0:1], (8, 2 * BLOCK)).astype(np.int32))


def _bias_kernel(rb_ref, bucket_ref, o_ref):
    b = bucket_ref[...]
    hits = [b == k for k in range(N_BUCKETS)]
    col = lax.broadcasted_iota(jnp.int32, (BLOCK, 2 * BLOCK), 1)
    for h in range(SW_HEADS):
        row0 = jnp.full(b.shape, NEG_BIG, F32)
        for k in range(N_BUCKETS):
            row0 = jnp.where(hits[k], rb_ref[k, h], row0)
        tile = pltpu.roll(jnp.concatenate([row0] * (BLOCK // 8), axis=0), 0, 1,
                          stride=1, stride_axis=0)
        o_ref[1, h] = tile
        o_ref[0, h] = jnp.where(col >= BLOCK, tile, NEG_BIG)


def _bias_table(rel_bias):
    return pl.pallas_call(
        _bias_kernel,
        in_specs=[
            pl.BlockSpec(memory_space=pltpu.SMEM),
            pl.BlockSpec(memory_space=pltpu.VMEM),
        ],
        out_specs=pl.BlockSpec(memory_space=pltpu.VMEM),
        out_shape=jax.ShapeDtypeStruct((2, SW_HEADS, BLOCK, 2 * BLOCK), F32),
        name="t5_bias",
    )(rel_bias.astype(F32), _bucket_table())


def _sw_kernel(sink_ref, qa_ref, qb_ref, k0_ref, k1_ref, k2_ref, v0_ref, v1_ref, v2_ref, bias_ref,
               o_ref, p_ref, st_ref):
    g = pl.program_id(0)

    @pl.when(g == 0)
    def _():
        p_ref[...] = jnp.zeros_like(p_ref)
        st_ref[...] = jnp.zeros_like(st_ref)

    lane = lax.broadcasted_iota(jnp.int32, (BLOCK, LANES), 1)
    ones = jnp.ones((2 * BLOCK, LANES), BF16)
    scale = jnp.asarray(SCALE, BF16)
    pairs_per_kv = SW_HEADS // SW_KV_HEADS // 2
    dims = (((1,), (1,)), ((), ()))

    def swap_halves(t):
        return jnp.concatenate([t[:, HEAD_DIM:], t[:, :HEAD_DIM]], axis=1)

    def on_side(kv):
        return (lane >= HEAD_DIM) if kv % 2 else (lane < HEAD_DIM)

    def kv_cols(kv):
        return slice((kv // 2) * LANES, (kv // 2 + 1) * LANES)

    def stages(cur, q_ref, kp_ref, kc_ref, vp_ref, vc_ref, out_rows):
        old = 1 - cur
        k = jnp.concatenate([kp_ref[...], kc_ref[...]], axis=0)
        v = jnp.concatenate([vp_ref[...], vc_ref[...]], axis=0)
        logits, heads = [], []
        for kv in range(SW_KV_HEADS):
            side = kv % 2
            same, other, hs, ho = [], [], [], []
            for pp in range(pairs_per_kv):
                p = kv * pairs_per_kv + pp
                q_pair = q_ref[:, p * LANES:(p + 1) * LANES] * scale
                zero = jnp.zeros_like(q_pair)
                same.append(jnp.where(on_side(kv), q_pair, zero))
                other.append(jnp.where(on_side(kv), swap_halves(q_pair), zero))
                hs.append(2 * p + side)
                ho.append(2 * p + 1 - side)
            lhs = jnp.concatenate(same + other, axis=0)
            logits.append(lax.dot_general(lhs, k[:, kv_cols(kv)], dims,
                                          preferred_element_type=F32))
            heads.append(hs + ho)
        results = []
        for kv in range(SW_KV_HEADS):
            v_pair = v[:, kv_cols(kv)]
            half = pairs_per_kv * BLOCK
            r_same = jnp.dot(p_ref[old, kv, :half, :], jnp.concatenate([v_pair, ones], axis=1),
                             preferred_element_type=F32)
            r_other = jnp.dot(p_ref[old, kv, half:, :],
                              jnp.concatenate([swap_halves(v_pair), ones], axis=1),
                              preferred_element_type=F32)
            results.append((r_same, r_other))

        for kv in range(SW_KV_HEADS):
            maxes = []
            for i, h in enumerate(heads[kv]):
                lg = logits[kv][i * BLOCK:(i + 1) * BLOCK, :] + bias_ref[0, h]
                m = jnp.maximum(jnp.max(lg, axis=-1, keepdims=True), sink_ref[h])
                p_ref[cur, kv, i * BLOCK:(i + 1) * BLOCK, :] = jnp.exp(lg - m).astype(BF16)
                maxes.append(m)
            for pp in range(pairs_per_kv):
                h_same, h_other = heads[kv][pp], heads[kv][pairs_per_kv + pp]
                m = jnp.where(on_side(kv), maxes[pp], maxes[pairs_per_kv + pp])
                sink = jnp.where(on_side(kv), sink_ref[h_same], sink_ref[h_other])
                st_ref[cur, kv * pairs_per_kv + pp] = jnp.exp(sink - m)

        for kv in range(SW_KV_HEADS):
            r_same, r_other = results[kv]
            for pp in range(pairs_per_kv):
                p = kv * pairs_per_kv + pp
                rows = slice(pp * BLOCK, (pp + 1) * BLOCK)
                num = jnp.where(on_side(kv), r_same[rows, :LANES], r_other[rows, :LANES])
                den = jnp.where(on_side(kv), r_same[rows, LANES:], r_other[rows, LANES:])
                o_ref[out_rows, p * LANES:(p + 1) * LANES] = (
                    num / (den + st_ref[old, p])).astype(BF16)

    @pl.when(g >= 0)
    def _():
        stages(1, qa_ref, k0_ref, k1_ref, v0_ref, v1_ref, slice(0, BLOCK))

    @pl.when(g >= 0)
    def _():
        stages(0, qb_ref, k1_ref, k2_ref, v1_ref, v2_ref, slice(BLOCK, 2 * BLOCK))


def _sw_attention(proj, sinks, bias):
    kq = OFF_Q_SW // SW_WIDTH
    kk = OFF_K_SW // SW_KV_WIDTH
    kv = OFF_V_SW // SW_KV_WIDTH
    last = SEQ // BLOCK - 1
    blk = lambda n: jnp.clip(n, 0, last)
    q_spec = lambda off: pl.BlockSpec((BLOCK, SW_WIDTH), lambda g: (blk(2 * g + off), kq))
    k_spec = lambda off: pl.BlockSpec((BLOCK, SW_KV_WIDTH), lambda g: (blk(2 * g + off), kk))
    v_spec = lambda off: pl.BlockSpec((BLOCK, SW_KV_WIDTH), lambda g: (blk(2 * g + off), kv))
    return pl.pallas_call(
        _sw_kernel,
        grid=((last + 1) // 2 + 1,),
        in_specs=[
            pl.BlockSpec(memory_space=pltpu.SMEM),
            q_spec(-1), q_spec(0),
            k_spec(-2), k_spec(-1), k_spec(0),
            v_spec(-3), v_spec(-2), v_spec(-1),
            pl.BlockSpec((1, SW_HEADS, BLOCK, 2 * BLOCK), lambda g: (jnp.minimum(g, 1), 0, 0, 0)),
        ],
        out_specs=pl.BlockSpec((2 * BLOCK, SW_WIDTH), lambda g: (jnp.maximum(g - 1, 0), 0)),
        out_shape=jax.ShapeDtypeStruct((SEQ, SW_WIDTH), BF16),
        scratch_shapes=[
            pltpu.VMEM((2, SW_KV_HEADS, SW_HEADS // SW_KV_HEADS * BLOCK, 2 * BLOCK), BF16),
            pltpu.VMEM((2, SW_HEADS // 2, BLOCK, LANES), F32),
        ],
        compiler_params=pltpu.CompilerParams(
            dimension_semantics=("arbitrary",), vmem_limit_bytes=VMEM_LIMIT),
        name="sw_attention",
    )(sinks.astype(F32), proj, proj, proj, proj, proj, proj, proj, proj, bias)


def _rms(x, g):
    ms = jnp.mean(x * x, axis=-1, keepdims=True)
    return (x * lax.rsqrt(ms + RMS_EPS)) * g


def _silu(z):
    return z * (1.0 / (1.0 + jnp.exp(-z)))


def _out_kernel(osb_ref, osw_ref, zsb_ref, zswa_ref, zswb_ref, gsb_ref, gsw_ref, gpost_ref,
                w_hbm, x_ref, o_ref, w_ref, stage_ref, sem):
    @pl.when(pl.program_id(0) == 0)
    def _():
        _cast_weights(w_hbm, w_ref, stage_ref, sem)

    chunk = OUT_TM // OUT_CHUNKS
    ys = []
    for c in range(OUT_CHUNKS):
        rows = pl.ds(c * chunk, chunk)
        a_sb = _rms(osb_ref[rows, :].astype(F32), gsb_ref[...]) * zsb_ref[rows, :].astype(F32)
        g_sw = jnp.concatenate([zswa_ref[rows, :], zswb_ref[rows, :]], axis=1).astype(F32)
        a_sw = _rms(osw_ref[rows, :].astype(F32), gsw_ref[...]) * g_sw
        y = jnp.dot(a_sb.astype(BF16), w_ref[:SB_WIDTH, :], preferred_element_type=F32)
        ys.append(y + jnp.dot(a_sw.astype(BF16), w_ref[SB_WIDTH:, :],
                              preferred_element_type=F32))
    for c in range(OUT_CHUNKS):
        rows = pl.ds(c * chunk, chunk)
        o_ref[rows, :] = x_ref[rows, :] + _rms(ys[c], gpost_ref[...])


def _outproj(o_sb, o_sw, proj, gn_sb, gn_sw, norm_post, w_out, x2):
    half = SW_WIDTH // 2
    return pl.pallas_call(
        _out_kernel,
        grid=(SEQ // OUT_TM,),
        in_specs=[
            pl.BlockSpec((OUT_TM, SB_WIDTH), lambda i: (i, 0)),
            pl.BlockSpec((OUT_TM, SW_WIDTH), lambda i: (i, 0)),
            pl.BlockSpec((OUT_TM, SB_WIDTH), lambda i: (i, OFF_Z_SB // SB_WIDTH)),
            pl.BlockSpec((OUT_TM, half), lambda i: (i, OFF_Z_SW // half)),
            pl.BlockSpec((OUT_TM, half), lambda i: (i, OFF_Z_SW // half + 1)),
            pl.BlockSpec((1, SB_WIDTH), lambda i: (0, 0)),
            pl.BlockSpec((1, SW_WIDTH), lambda i: (0, 0)),
            pl.BlockSpec((1, D_MODEL), lambda i: (0, 0)),
            pl.BlockSpec(memory_space=pl.ANY),
            pl.BlockSpec((OUT_TM, D_MODEL), lambda i: (i, 0)),
        ],
        out_specs=pl.BlockSpec((OUT_TM, D_MODEL), lambda i: (i, 0)),
        out_shape=jax.ShapeDtypeStruct((SEQ, D_MODEL), F32),
        scratch_shapes=[
            pltpu.VMEM((D_MIX, D_MODEL), BF16),
            pltpu.VMEM((CAST_SLOTS, OUT_CAST_ROWS, D_MODEL), F32),
            pltpu.SemaphoreType.DMA((CAST_SLOTS,)),
        ],
        compiler_params=pltpu.CompilerParams(
            dimension_semantics=("arbitrary",), vmem_limit_bytes=VMEM_LIMIT),
        name="outproj",
    )(o_sb, o_sw, proj, proj, proj, gn_sb, gn_sw, norm_post, w_out, x2)


def kernel(x, w_in, w_out, norm_pre, norm_post, gn_sb, gn_sw, sinks, rel_bias):
    assert x.shape == (1, SEQ, D_MODEL) and w_in.shape == (1, D_MODEL, D_IN_PROJ)
    assert OFF_Z_SW % (SW_WIDTH // 2) == 0 and OFF_Q_SW % SW_WIDTH == 0
    x2 = x.reshape(SEQ, D_MODEL)
    proj = _inproj(x2, norm_pre.astype(F32), w_in[0].astype(F32))
    o_sb = _sb_attention(proj)
    bias = _bias_table(rel_bias)
    o_sw = _sw_attention(proj, sinks[0], bias)
    out = _outproj(o_sb, o_sw, proj, gn_sb.astype(F32), gn_sw.astype(F32), norm_post.astype(F32),
                   w_out[0].astype(F32), x2)
    return out.reshape(1, SEQ, D_MODEL)
```

```python
import math

import numpy as np
import jax
import jax.numpy as jnp
from jax import lax
from jax.experimental import pallas as pl
from jax.experimental.pallas import tpu as pltpu

D_MODEL = 2048
SEQ = 16384
HEAD_DIM = 64
SB_HEADS = 16
SW_HEADS = 16
SW_KV_HEADS = 4
SB_WIDTH = SB_HEADS * HEAD_DIM
SW_WIDTH = SW_HEADS * HEAD_DIM
SW_KV_WIDTH = SW_KV_HEADS * HEAD_DIM
D_MIX = SB_WIDTH + SW_WIDTH
D_IN_PROJ = 4 * SB_WIDTH + 2 * SW_WIDTH + 2 * SW_KV_WIDTH
WINDOW = 128
BLOCK = 128
N_BUCKETS = 32
MAX_DISTANCE = 128
RMS_EPS = 1e-6
SCALE = HEAD_DIM ** -0.5

OFF_Q_SB = 0
OFF_K_SB = SB_WIDTH
OFF_V_SB = 2 * SB_WIDTH
OFF_Z_SB = 3 * SB_WIDTH
OFF_Q_SW = 4 * SB_WIDTH
OFF_K_SW = OFF_Q_SW + SW_WIDTH
OFF_V_SW = OFF_K_SW + SW_KV_WIDTH
OFF_Z_SW = OFF_V_SW + SW_KV_WIDTH

LANES = 128
VMEM_LIMIT = 56 * 1024 * 1024

PROJ_TM = 512
PROJ_TN = 3328
CAST_ROWS = 16
OUT_CAST_ROWS = 32
CAST_SLOTS = 8
SB_T = 256
SB_H = SB_T // 2
OUT_TM = 512
OUT_CHUNKS = 2
NEG_BIG = -1e30
SB_STOP = -110.0

F32 = jnp.float32
BF16 = jnp.bfloat16


def _cast_weights(w_hbm, w_ref, stage_ref, sem):
    slots, rows = stage_ref.shape[0], stage_ref.shape[1]
    n_chunks = w_hbm.shape[0] // rows
    ahead = slots - 1

    def chunk_copy(c):
        slot = c % slots
        src = w_hbm.at[pl.ds(pl.multiple_of(c * rows, rows), rows), :]
        return pltpu.make_async_copy(src, stage_ref.at[slot], sem.at[slot])

    for c in range(ahead):
        chunk_copy(c).start()

    def body(c, _):
        @pl.when(c + ahead < n_chunks)
        def _():
            chunk_copy(c + ahead).start()

        chunk_copy(c).wait()
        w_ref[pl.ds(pl.multiple_of(c * rows, rows), rows), :] = stage_ref[c % slots].astype(BF16)
        return 0

    lax.fori_loop(0, n_chunks, body, 0)


def _inproj_kernel(x_ref, g_ref, w_hbm, o_ref, w_ref, stage_ref, sem):
    @pl.when(pl.program_id(0) == 0)
    def _():
        _cast_weights(w_hbm, w_ref, stage_ref, sem)

    x = x_ref[...]
    ms = jnp.mean(x * x, axis=-1, keepdims=True)
    xn = ((x * lax.rsqrt(ms + RMS_EPS)) * g_ref[...]).astype(BF16)
    gates = ((OFF_Z_SB, OFF_Z_SB + SB_WIDTH), (OFF_Z_SW, OFF_Z_SW + SW_WIDTH))
    for j in reversed(range(D_IN_PROJ // PROJ_TN)):
        lo, hi = j * PROJ_TN, (j + 1) * PROJ_TN
        res = jnp.dot(xn, w_ref[:, lo:hi], preferred_element_type=F32)
        cuts = sorted({lo, hi} | {min(max(c, lo), hi) for g in gates for c in g})
        for a, b in zip(cuts[:-1], cuts[1:]):
            part = res[:, a - lo:b - lo]
            if any(g0 <= a and b <= g1 for g0, g1 in gates):
                part = _silu(part)
            o_ref[:, a:b] = part.astype(BF16)


def _inproj(x2, g, w):
    return pl.pallas_call(
        _inproj_kernel,
        grid=(SEQ // PROJ_TM,),
        in_specs=[
            pl.BlockSpec((PROJ_TM, D_MODEL), lambda i: (i, 0)),
            pl.BlockSpec((1, D_MODEL), lambda i: (0, 0)),
            pl.BlockSpec(memory_space=pl.ANY),
        ],
        out_specs=pl.BlockSpec((PROJ_TM, D_IN_PROJ), lambda i: (i, 0)),
        out_shape=jax.ShapeDtypeStruct((SEQ, D_IN_PROJ), BF16),
        scratch_shapes=[
            pltpu.VMEM((D_MODEL, D_IN_PROJ), BF16),
            pltpu.VMEM((CAST_SLOTS, CAST_ROWS, D_IN_PROJ), F32),
            pltpu.SemaphoreType.DMA((CAST_SLOTS,)),
        ],
        compiler_params=pltpu.CompilerParams(
            dimension_semantics=("arbitrary",), vmem_limit_bytes=VMEM_LIMIT),
        name="inproj",
    )(x2, g, w)


def _sb_kernel(q_ref, k_ref, v_ref, tri_ref, o_ref,
               hi_ref, lbl_ref, lbr_ref, lbp_ref, sp0_ref, w_ref, carry_w_ref, go_ref,
               acc_ref, carry_ref):
    last = SEQ // SB_T - 1
    for ref in (hi_ref, lbl_ref, lbr_ref, lbp_ref, sp0_ref, w_ref, carry_w_ref):
        ref[...] = jnp.zeros_like(ref)

    lane = lax.broadcasted_iota(jnp.int32, (SB_T, LANES), 1)

    def stack_heads(tile):
        start = pl.multiple_of(tile * SB_T, SB_T)
        q = q_ref[pl.ds(start, SB_T), :] * jnp.asarray(SCALE, BF16)
        zero = jnp.zeros_like(q)
        return jnp.concatenate([jnp.where(lane < HEAD_DIM, q, zero),
                                jnp.where(lane >= HEAD_DIM, q, zero)], axis=0)

    tri = tri_ref[...]
    r_h = lax.broadcasted_iota(jnp.int32, (SB_H, SB_H), 0)
    c_h = lax.broadcasted_iota(jnp.int32, (SB_H, SB_H), 1)
    cut_add = jnp.where(c_h < r_h, 0.0, NEG_BIG).astype(F32)
    zeros_h = jnp.zeros((SB_H, SB_H), BF16)

    def qk(qs, kb):
        start = pl.multiple_of(kb * SB_T, SB_T)
        kblk = k_ref[pl.ds(start, SB_T), :]
        return lax.dot_general(qs, kblk, (((1,), (1,)), ((), ())), preferred_element_type=F32)

    def softplus_parts(z):
        neg_abs = lax.bitcast_convert_type(
            lax.bitcast_convert_type(z, jnp.int32) | jnp.int32(-2 ** 31), F32)
        sp = jnp.maximum(z, 0.0) + jnp.log(1.0 + jnp.exp(neg_abs))
        return sp.astype(BF16), z - sp, sp[:, 0:1]

    def cut(z):
        return z + cut_add

    def col_max(c):
        return jnp.max(jnp.max(c.reshape(c.shape[0] // 8, 8, 1), axis=0))

    def quarters(a):
        left = a[:, :SB_H]
        right = jnp.concatenate([a[SB_H:SB_T, SB_H:], a[SB_T + SB_H:, SB_H:]], axis=0)
        return left, right

    def unquarter(left, right):
        right_full = jnp.concatenate([zeros_h, right[:SB_H], zeros_h, right[SB_H:]], axis=0)
        return jnp.concatenate([left, right_full], axis=1)

    def cumsum(hi):
        return jnp.dot(hi, tri, preferred_element_type=F32)

    def values(kb, w, v_valid=None):
        start = pl.multiple_of(kb * SB_T, SB_T)
        vblk = v_ref[pl.ds(start, SB_T), :]
        if v_valid is not None:
            vblk = jnp.where(v_valid, vblk, jnp.zeros_like(vblk))
        return jnp.dot(w, vblk, preferred_element_type=F32)

    def stages(cur, n, t):
        old = 1 - cur
        qs_n = stack_heads(n)
        z_d = qk(qs_n, n)
        z_p = qk(qs_n, jnp.maximum(n - 1, 0))
        cs_d = cumsum(hi_ref[old, 0])
        cs_p = cumsum(hi_ref[old, 1])
        acc_ref[...] = (values(t, w_ref[old, 0])
                        + values(jnp.maximum(t - 1, 0), w_ref[old, 1], v_valid=t > 0))
        carry_t = carry_w_ref[old]
        carry_ref[...] = carry_t
        go_ref[0] = (col_max(carry_t) >= SB_STOP).astype(jnp.int32)

        zl, zr = quarters(z_d)
        zl = jnp.concatenate([cut(zl[:SB_H]), zl[SB_H:SB_T],
                              cut(zl[SB_T:SB_T + SB_H]), zl[SB_T + SB_H:]], axis=0)
        zr = jnp.concatenate([cut(zr[:SB_H]), cut(zr[SB_H:])], axis=0)
        hi_l, lbl_ref[cur], sp0_ref[cur, 0] = softplus_parts(zl)
        hi_r, lbr_ref[cur], _ = softplus_parts(zr)
        hi_ref[cur, 0] = unquarter(hi_l, hi_r)
        hi_ref[cur, 1], lbp_ref[cur], sp0_ref[cur, 1] = softplus_parts(z_p)

        cs_l, cs_r = quarters(cs_d)
        w_l = jnp.exp(lbl_ref[old] + cs_l).astype(BF16)
        w_r = jnp.exp(lbr_ref[old] + cs_r).astype(BF16)
        carry_d = cs_d[:, 0:1] - sp0_ref[old, 0]
        w_ref[cur, 0] = unquarter(w_l, w_r)
        w_ref[cur, 1] = jnp.exp(lbp_ref[old] + cs_p + carry_d).astype(BF16)
        carry_w_ref[cur] = carry_d + (cs_p[:, 0:1] - sp0_ref[old, 1])

    def block(qs, kb):
        hi, log_beta, sp0 = softplus_parts(qk(qs, kb))
        cs = cumsum(hi)
        carry = carry_ref[...]
        w = jnp.exp(log_beta + cs + carry).astype(BF16)
        acc_ref[...] += values(kb, w)
        carry_ref[...] = carry + (cs[:, 0:1] - sp0)

    def cond(state):
        kb, go = state
        return jnp.logical_and(kb >= 0, go > 0)

    def two_steps(i, _):
        for half in range(2):
            s = 2 * i + half
            t = jnp.maximum(s - 2, 0)
            stages(half, jnp.minimum(s, last), t)

            def body(state, t=t):
                kb, _ = state
                block(stack_heads(t), kb)
                go = (jnp.max(carry_ref[...]) >= SB_STOP).astype(jnp.int32)
                return kb - 1, go

            lax.while_loop(cond, body, (t - 2, go_ref[0]))
            rows = pl.ds(pl.multiple_of(t * SB_T, SB_T), SB_T)
            o_ref[rows, :] = jnp.where(lane < HEAD_DIM, acc_ref[:SB_T, :],
                                       acc_ref[SB_T:, :]).astype(BF16)
        return 0

    lax.fori_loop(0, (last + 1 + 2) // 2, two_steps, 0)


def _sb_tri():
    j = np.arange(SB_T)[:, None]
    s = np.arange(SB_T)[None, :]
    return jnp.asarray(np.where(j > s, -1.0, 0.0).astype(np.float32), BF16)


def _sb_attention(proj):
    pair = lambda off: pl.BlockSpec((SEQ, LANES), lambda p: (0, off // LANES + p))
    return pl.pallas_call(
        _sb_kernel,
        grid=(SB_WIDTH // LANES,),
        in_specs=[pair(OFF_Q_SB), pair(OFF_K_SB), pair(OFF_V_SB),
                  pl.BlockSpec((SB_T, SB_T), lambda p: (0, 0))],
        out_specs=pl.BlockSpec((SEQ, LANES), lambda p: (0, p)),
        out_shape=jax.ShapeDtypeStruct((SEQ, SB_WIDTH), BF16),
        scratch_shapes=[
            pltpu.VMEM((2, 2, 2 * SB_T, SB_T), BF16),
            pltpu.VMEM((2, 2 * SB_T, SB_H), F32),
            pltpu.VMEM((2, SB_T, SB_H), F32),
            pltpu.VMEM((2, 2 * SB_T, SB_T), F32),
            pltpu.VMEM((2, 2, 2 * SB_T, 1), F32),
            pltpu.VMEM((2, 2, 2 * SB_T, SB_T), BF16),
            pltpu.VMEM((2, 2 * SB_T, 1), F32),
            pltpu.SMEM((1,), jnp.int32),
            pltpu.VMEM((2 * SB_T, LANES), F32),
            pltpu.VMEM((2 * SB_T, 1), F32),
        ],
        compiler_params=pltpu.CompilerParams(
            dimension_semantics=("arbitrary",), vmem_limit_bytes=VMEM_LIMIT),
        name="sb_attention",
    )(proj, proj, proj, _sb_tri())


def _bucket_table():
    qi = np.arange(BLOCK)[:, None]
    ci = np.arange(2 * BLOCK)[None, :]
    dist = qi + BLOCK - ci
    n = np.maximum(dist, 0)
    max_exact = N_BUCKETS // 2
    nf = np.maximum(n, 1).astype(np.float64)
    val = np.log(nf / max_exact) / math.log(MAX_DISTANCE / max_exact) * (N_BUCKETS - max_exact)
    in_window = (dist >= 0) & (dist < WINDOW)
    frac = val - np.floor(val)
    chk = in_window & (n > max_exact)
    assert np.all((frac[chk] > 1e-3) & (frac[chk] < 1 - 1e-3))
    large = np.minimum(max_exact + val.astype(np.int64), N_BUCKETS - 1)
    bucket = np.where(n < max_exact, n, large)
    later = np.where(in_window, bucket, -1)
    assert all(np.array_equal(later[r], np.roll(later[0], r)) for r in range(BLOCK))
    return jnp.asarray(np.broadcast_to(later[0:1], (8, 2 * BLOCK)).astype(np.int32))


def _bias_kernel(rb_ref, bucket_ref, o_ref):
    b = bucket_ref[...]
    hits = [b == k for k in range(N_BUCKETS)]
    col = lax.broadcasted_iota(jnp.int32, (BLOCK, 2 * BLOCK), 1)
    for h in range(SW_HEADS):
        row0 = jnp.full(b.shape, NEG_BIG, F32)
        for k in range(N_BUCKETS):
            row0 = jnp.where(hits[k], rb_ref[k, h], row0)
        tile = pltpu.roll(jnp.concatenate([row0] * (BLOCK // 8), axis=0), 0, 1,
                          stride=1, stride_axis=0)
        o_ref[1, h] = tile
        o_ref[0, h] = jnp.where(col >= BLOCK, tile, NEG_BIG)


def _bias_table(rel_bias):
    return pl.pallas_call(
        _bias_kernel,
        in_specs=[
            pl.BlockSpec(memory_space=pltpu.SMEM),
            pl.BlockSpec(memory_space=pltpu.VMEM),
        ],
        out_specs=pl.BlockSpec(memory_space=pltpu.VMEM),
        out_shape=jax.ShapeDtypeStruct((2, SW_HEADS, BLOCK, 2 * BLOCK), F32),
        name="t5_bias",
    )(rel_bias.astype(F32), _bucket_table())


def _sw_kernel(sink_ref, qa_ref, qb_ref, k0_ref, k1_ref, k2_ref, v0_ref, v1_ref, v2_ref, bias_ref,
               o_ref, p_ref, st_ref):
    g = pl.program_id(0)

    @pl.when(g == 0)
    def _():
        p_ref[...] = jnp.zeros_like(p_ref)
        st_ref[...] = jnp.zeros_like(st_ref)

    lane = lax.broadcasted_iota(jnp.int32, (BLOCK, LANES), 1)
    ones = jnp.ones((2 * BLOCK, LANES), BF16)
    scale = jnp.asarray(SCALE, BF16)
    pairs_per_kv = SW_HEADS // SW_KV_HEADS // 2
    dims = (((1,), (1,)), ((), ()))

    def swap_halves(t):
        return jnp.concatenate([t[:, HEAD_DIM:], t[:, :HEAD_DIM]], axis=1)

    def on_side(kv):
        return (lane >= HEAD_DIM) if kv % 2 else (lane < HEAD_DIM)

    def kv_cols(kv):
        return slice((kv // 2) * LANES, (kv // 2 + 1) * LANES)

    def stages(cur, q_ref, kp_ref, kc_ref, vp_ref, vc_ref, out_rows):
        old = 1 - cur
        k = jnp.concatenate([kp_ref[...], kc_ref[...]], axis=0)
        v = jnp.concatenate([vp_ref[...], vc_ref[...]], axis=0)
        logits, heads = [], []
        for kv in range(SW_KV_HEADS):
            side = kv % 2
            same, other, hs, ho = [], [], [], []
            for pp in range(pairs_per_kv):
                p = kv * pairs_per_kv + pp
                q_pair = q_ref[:, p * LANES:(p + 1) * LANES] * scale
                zero = jnp.zeros_like(q_pair)
                same.append(jnp.where(on_side(kv), q_pair, zero))
                other.append(jnp.where(on_side(kv), swap_halves(q_pair), zero))
                hs.append(2 * p + side)
                ho.append(2 * p + 1 - side)
            lhs = jnp.concatenate(same + other, axis=0)
            logits.append(lax.dot_general(lhs, k[:, kv_cols(kv)], dims,
                                          preferred_element_type=F32))
            heads.append(hs + ho)
        results = []
        for kv in range(SW_KV_HEADS):
            v_pair = v[:, kv_cols(kv)]
            half = pairs_per_kv * BLOCK
            r_same = jnp.dot(p_ref[old, kv, :half, :], jnp.concatenate([v_pair, ones], axis=1),
                             preferred_element_type=F32)
            r_other = jnp.dot(p_ref[old, kv, half:, :],
                              jnp.concatenate([swap_halves(v_pair), ones], axis=1),
                              preferred_element_type=F32)
            results.append((r_same, r_other))

        for kv in range(SW_KV_HEADS):
            maxes = []
            for i, h in enumerate(heads[kv]):
                lg = logits[kv][i * BLOCK:(i + 1) * BLOCK, :] + bias_ref[0, h]
                m = jnp.maximum(jnp.max(lg, axis=-1, keepdims=True), sink_ref[h])
                p_ref[cur, kv, i * BLOCK:(i + 1) * BLOCK, :] = jnp.exp(lg - m).astype(BF16)
                maxes.append(m)
            for pp in range(pairs_per_kv):
                h_same, h_other = heads[kv][pp], heads[kv][pairs_per_kv + pp]
                m = jnp.where(on_side(kv), maxes[pp], maxes[pairs_per_kv + pp])
                sink = jnp.where(on_side(kv), sink_ref[h_same], sink_ref[h_other])
                st_ref[cur, kv * pairs_per_kv + pp] = jnp.exp(sink - m)

        for kv in range(SW_KV_HEADS):
            r_same, r_other = results[kv]
            for pp in range(pairs_per_kv):
                p = kv * pairs_per_kv + pp
                rows = slice(pp * BLOCK, (pp + 1) * BLOCK)
                num = jnp.where(on_side(kv), r_same[rows, :LANES], r_other[rows, :LANES])
                den = jnp.where(on_side(kv), r_same[rows, LANES:], r_other[rows, LANES:])
                o_ref[out_rows, p * LANES:(p + 1) * LANES] = (
                    num / (den + st_ref[old, p])).astype(BF16)

    @pl.when(g >= 0)
    def _():
        stages(1, qa_ref, k0_ref, k1_ref, v0_ref, v1_ref, slice(0, BLOCK))

    @pl.when(g >= 0)
    def _():
        stages(0, qb_ref, k1_ref, k2_ref, v1_ref, v2_ref, slice(BLOCK, 2 * BLOCK))


def _sw_attention(proj, sinks, bias):
    kq = OFF_Q_SW // SW_WIDTH
    kk = OFF_K_SW // SW_KV_WIDTH
    kv = OFF_V_SW // SW_KV_WIDTH
    last = SEQ // BLOCK - 1
    blk = lambda n: jnp.clip(n, 0, last)
    q_spec = lambda off: pl.BlockSpec((BLOCK, SW_WIDTH), lambda g: (blk(2 * g + off), kq))
    k_spec = lambda off: pl.BlockSpec((BLOCK, SW_KV_WIDTH), lambda g: (blk(2 * g + off), kk))
    v_spec = lambda off: pl.BlockSpec((BLOCK, SW_KV_WIDTH), lambda g: (blk(2 * g + off), kv))
    return pl.pallas_call(
        _sw_kernel,
        grid=((last + 1) // 2 + 1,),
        in_specs=[
            pl.BlockSpec(memory_space=pltpu.SMEM),
            q_spec(-1), q_spec(0),
            k_spec(-2), k_spec(-1), k_spec(0),
            v_spec(-3), v_spec(-2), v_spec(-1),
            pl.BlockSpec((1, SW_HEADS, BLOCK, 2 * BLOCK), lambda g: (jnp.minimum(g, 1), 0, 0, 0)),
        ],
        out_specs=pl.BlockSpec((2 * BLOCK, SW_WIDTH), lambda g: (jnp.maximum(g - 1, 0), 0)),
        out_shape=jax.ShapeDtypeStruct((SEQ, SW_WIDTH), BF16),
        scratch_shapes=[
            pltpu.VMEM((2, SW_KV_HEADS, SW_HEADS // SW_KV_HEADS * BLOCK, 2 * BLOCK), BF16),
            pltpu.VMEM((2, SW_HEADS // 2, BLOCK, LANES), F32),
        ],
        compiler_params=pltpu.CompilerParams(
            dimension_semantics=("arbitrary",), vmem_limit_bytes=VMEM_LIMIT),
        name="sw_attention",
    )(sinks.astype(F32), proj, proj, proj, proj, proj, proj, proj, proj, bias)


def _rms(x, g):
    ms = jnp.mean(x * x, axis=-1, keepdims=True)
    return (x * lax.rsqrt(ms + RMS_EPS)) * g


def _silu(z):
    return z * (1.0 / (1.0 + jnp.exp(-z)))


def _out_kernel(osb_ref, osw_ref, zsb_ref, zswa_ref, zswb_ref, gsb_ref, gsw_ref, gpost_ref,
                w_hbm, x_ref, o_ref, w_ref, stage_ref, sem):
    @pl.when(pl.program_id(0) == 0)
    def _():
        _cast_weights(w_hbm, w_ref, stage_ref, sem)

    chunk = OUT_TM // OUT_CHUNKS
    ys = []
    for c in range(OUT_CHUNKS):
        rows = pl.ds(c * chunk, chunk)
        a_sb = _rms(osb_ref[rows, :].astype(F32), gsb_ref[...]) * zsb_ref[rows, :].astype(F32)
        g_sw = jnp.concatenate([zswa_ref[rows, :], zswb_ref[rows, :]], axis=1).astype(F32)
        a_sw = _rms(osw_ref[rows, :].astype(F32), gsw_ref[...]) * g_sw
        y = jnp.dot(a_sb.astype(BF16), w_ref[:SB_WIDTH, :], preferred_element_type=F32)
        ys.append(y + jnp.dot(a_sw.astype(BF16), w_ref[SB_WIDTH:, :],
                              preferred_element_type=F32))
    for c in range(OUT_CHUNKS):
        rows = pl.ds(c * chunk, chunk)
        o_ref[rows, :] = x_ref[rows, :] + _rms(ys[c], gpost_ref[...])


def _outproj(o_sb, o_sw, proj, gn_sb, gn_sw, norm_post, w_out, x2):
    half = SW_WIDTH // 2
    return pl.pallas_call(
        _out_kernel,
        grid=(SEQ // OUT_TM,),
        in_specs=[
            pl.BlockSpec((OUT_TM, SB_WIDTH), lambda i: (i, 0)),
            pl.BlockSpec((OUT_TM, SW_WIDTH), lambda i: (i, 0)),
            pl.BlockSpec((OUT_TM, SB_WIDTH), lambda i: (i, OFF_Z_SB // SB_WIDTH)),
            pl.BlockSpec((OUT_TM, half), lambda i: (i, OFF_Z_SW // half)),
            pl.BlockSpec((OUT_TM, half), lambda i: (i, OFF_Z_SW // half + 1)),
            pl.BlockSpec((1, SB_WIDTH), lambda i: (0, 0)),
            pl.BlockSpec((1, SW_WIDTH), lambda i: (0, 0)),
            pl.BlockSpec((1, D_MODEL), lambda i: (0, 0)),
            pl.BlockSpec(memory_space=pl.ANY),
            pl.BlockSpec((OUT_TM, D_MODEL), lambda i: (i, 0)),
        ],
        out_specs=pl.BlockSpec((OUT_TM, D_MODEL), lambda i: (i, 0)),
        out_shape=jax.ShapeDtypeStruct((SEQ, D_MODEL), F32),
        scratch_shapes=[
            pltpu.VMEM((D_MIX, D_MODEL), BF16),
            pltpu.VMEM((CAST_SLOTS, OUT_CAST_ROWS, D_MODEL), F32),
            pltpu.SemaphoreType.DMA((CAST_SLOTS,)),
        ],
        compiler_params=pltpu.CompilerParams(
            dimension_semantics=("arbitrary",), vmem_limit_bytes=VMEM_LIMIT),
        name="outproj",
    )(o_sb, o_sw, proj, proj, proj, gn_sb, gn_sw, norm_post, w_out, x2)


def kernel(x, w_in, w_out, norm_pre, norm_post, gn_sb, gn_sw, sinks, rel_bias):
    assert x.shape == (1, SEQ, D_MODEL) and w_in.shape == (1, D_MODEL, D_IN_PROJ)
    assert OFF_Z_SW % (SW_WIDTH // 2) == 0 and OFF_Q_SW % SW_WIDTH == 0
    x2 = x.reshape(SEQ, D_MODEL)
    proj = _inproj(x2, norm_pre.astype(F32), w_in[0].astype(F32))
    o_sb = _sb_attention(proj)
    bias = _bias_table(rel_bias)
    o_sw = _sw_attention(proj, sinks[0], bias)
    out = _outproj(o_sb, o_sw, proj, gn_sb.astype(F32), gn_sw.astype(F32), norm_post.astype(F32),
                   w_out[0].astype(F32), x2)
    return out.reshape(1, SEQ, D_MODEL)
```

```python
import math

import numpy as np
import jax
import jax.numpy as jnp
from jax import lax
from jax.experimental import pallas as pl
from jax.experimental.pallas import tpu as pltpu

D_MODEL = 2048
SEQ = 16384
HEAD_DIM = 64
SB_HEADS = 16
SW_HEADS = 16
SW_KV_HEADS = 4
SB_WIDTH = SB_HEADS * HEAD_DIM
SW_WIDTH = SW_HEADS * HEAD_DIM
SW_KV_WIDTH = SW_KV_HEADS * HEAD_DIM
D_MIX = SB_WIDTH + SW_WIDTH
D_IN_PROJ = 4 * SB_WIDTH + 2 * SW_WIDTH + 2 * SW_KV_WIDTH
WINDOW = 128
BLOCK = 128
N_BUCKETS = 32
MAX_DISTANCE = 128
RMS_EPS = 1e-6
SCALE = HEAD_DIM ** -0.5

OFF_Q_SB = 0
OFF_K_SB = SB_WIDTH
OFF_V_SB = 2 * SB_WIDTH
OFF_Z_SB = 3 * SB_WIDTH
OFF_Q_SW = 4 * SB_WIDTH
OFF_K_SW = OFF_Q_SW + SW_WIDTH
OFF_V_SW = OFF_K_SW + SW_KV_WIDTH
OFF_Z_SW = OFF_V_SW + SW_KV_WIDTH

LANES = 128
VMEM_LIMIT = 56 * 1024 * 1024

PROJ_TM = 512
PROJ_TN = 3328
CAST_ROWS = 16
OUT_CAST_ROWS = 32
CAST_SLOTS = 8
SB_T = 256
SB_H = SB_T // 2
OUT_TM = 512
OUT_CHUNKS = 2
NEG_BIG = -1e30
SB_STOP = -110.0

F32 = jnp.float32
BF16 = jnp.bfloat16


def _cast_weights(w_hbm, w_ref, stage_ref, sem):
    slots, rows = stage_ref.shape[0], stage_ref.shape[1]
    n_chunks = w_hbm.shape[0] // rows
    ahead = slots - 1

    def chunk_copy(c):
        slot = c % slots
        src = w_hbm.at[pl.ds(pl.multiple_of(c * rows, rows), rows), :]
        return pltpu.make_async_copy(src, stage_ref.at[slot], sem.at[slot])

    for c in range(ahead):
        chunk_copy(c).start()

    def body(c, _):
        @pl.when(c + ahead < n_chunks)
        def _():
            chunk_copy(c + ahead).start()

        chunk_copy(c).wait()
        w_ref[pl.ds(pl.multiple_of(c * rows, rows), rows), :] = stage_ref[c % slots].astype(BF16)
        return 0

    lax.fori_loop(0, n_chunks, body, 0)


def _inproj_kernel(x_ref, g_ref, w_hbm, o_ref, w_ref, stage_ref, sem):
    @pl.when(pl.program_id(0) == 0)
    def _():
        _cast_weights(w_hbm, w_ref, stage_ref, sem)

    x = x_ref[...]
    ms = jnp.mean(x * x, axis=-1, keepdims=True)
    xn = ((x * lax.rsqrt(ms + RMS_EPS)) * g_ref[...]).astype(BF16)
    gates = ((OFF_Z_SB, OFF_Z_SB + SB_WIDTH), (OFF_Z_SW, OFF_Z_SW + SW_WIDTH))
    queries = ((OFF_Q_SB, OFF_Q_SB + SB_WIDTH), (OFF_Q_SW, OFF_Q_SW + SW_WIDTH))
    inside = lambda a, b, ranges: any(r0 <= a and b <= r1 for r0, r1 in ranges)
    for j in reversed(range(D_IN_PROJ // PROJ_TN)):
        lo, hi = j * PROJ_TN, (j + 1) * PROJ_TN
        res = jnp.dot(xn, w_ref[:, lo:hi], preferred_element_type=F32)
        cuts = sorted({lo, hi} | {min(max(c, lo), hi) for r in gates + queries for c in r})
        for a, b in zip(cuts[:-1], cuts[1:]):
            part = res[:, a - lo:b - lo]
            if inside(a, b, gates):
                part = _silu(part)
            elif inside(a, b, queries):
                part = part * SCALE
            o_ref[:, a:b] = part.astype(BF16)


def _inproj(x2, g, w):
    return pl.pallas_call(
        _inproj_kernel,
        grid=(SEQ // PROJ_TM,),
        in_specs=[
            pl.BlockSpec((PROJ_TM, D_MODEL), lambda i: (i, 0)),
            pl.BlockSpec((1, D_MODEL), lambda i: (0, 0)),
            pl.BlockSpec(memory_space=pl.ANY),
        ],
        out_specs=pl.BlockSpec((PROJ_TM, D_IN_PROJ), lambda i: (i, 0)),
        out_shape=jax.ShapeDtypeStruct((SEQ, D_IN_PROJ), BF16),
        scratch_shapes=[
            pltpu.VMEM((D_MODEL, D_IN_PROJ), BF16),
            pltpu.VMEM((CAST_SLOTS, CAST_ROWS, D_IN_PROJ), F32),
            pltpu.SemaphoreType.DMA((CAST_SLOTS,)),
        ],
        compiler_params=pltpu.CompilerParams(
            dimension_semantics=("arbitrary",), vmem_limit_bytes=VMEM_LIMIT),
        name="inproj",
    )(x2, g, w)


def _sb_kernel(q_ref, k_ref, v_ref, tri_ref, o_ref,
               hi_ref, lbl_ref, lbr_ref, lbp_ref, sp0_ref, w_ref, carry_w_ref, go_ref,
               acc_ref, carry_ref):
    last = SEQ // SB_T - 1
    for ref in (hi_ref, lbl_ref, lbr_ref, lbp_ref, sp0_ref, w_ref, carry_w_ref):
        ref[...] = jnp.zeros_like(ref)

    lane = lax.broadcasted_iota(jnp.int32, (SB_T, LANES), 1)

    def stack_heads(tile):
        start = pl.multiple_of(tile * SB_T, SB_T)
        q = q_ref[pl.ds(start, SB_T), :]
        zero = jnp.zeros_like(q)
        return jnp.concatenate([jnp.where(lane < HEAD_DIM, q, zero),
                                jnp.where(lane >= HEAD_DIM, q, zero)], axis=0)

    tri = tri_ref[...]
    r_h = lax.broadcasted_iota(jnp.int32, (SB_H, SB_H), 0)
    c_h = lax.broadcasted_iota(jnp.int32, (SB_H, SB_H), 1)
    cut_add = jnp.where(c_h < r_h, 0.0, NEG_BIG).astype(F32)
    zeros_h = jnp.zeros((SB_H, SB_H), BF16)

    def qk(qs, kb):
        start = pl.multiple_of(kb * SB_T, SB_T)
        kblk = k_ref[pl.ds(start, SB_T), :]
        return lax.dot_general(qs, kblk, (((1,), (1,)), ((), ())), preferred_element_type=F32)

    def softplus_parts(z):
        neg_abs = lax.bitcast_convert_type(
            lax.bitcast_convert_type(z, jnp.int32) | jnp.int32(-2 ** 31), F32)
        sp = jnp.maximum(z, 0.0) + jnp.log(1.0 + jnp.exp(neg_abs))
        return sp.astype(BF16), z - sp, sp[:, 0:1]

    def cut(z):
        return z + cut_add

    def col_max(c):
        return jnp.max(jnp.max(c.reshape(c.shape[0] // 8, 8, 1), axis=0))

    def quarters(a):
        left = a[:, :SB_H]
        right = jnp.concatenate([a[SB_H:SB_T, SB_H:], a[SB_T + SB_H:, SB_H:]], axis=0)
        return left, right

    def unquarter(left, right):
        right_full = jnp.concatenate([zeros_h, right[:SB_H], zeros_h, right[SB_H:]], axis=0)
        return jnp.concatenate([left, right_full], axis=1)

    def cumsum(hi):
        return jnp.dot(hi, tri, preferred_element_type=F32)

    def values(kb, w, v_valid=None):
        start = pl.multiple_of(kb * SB_T, SB_T)
        vblk = v_ref[pl.ds(start, SB_T), :]
        if v_valid is not None:
            vblk = jnp.where(v_valid, vblk, jnp.zeros_like(vblk))
        return jnp.dot(w, vblk, preferred_element_type=F32)

    def stages(cur, n, t):
        old = 1 - cur
        qs_n = stack_heads(n)
        z_d = qk(qs_n, n)
        z_p = qk(qs_n, jnp.maximum(n - 1, 0))
        cs_d = cumsum(hi_ref[old, 0])
        cs_p = cumsum(hi_ref[old, 1])
        acc_ref[...] = (values(t, w_ref[old, 0])
                        + values(jnp.maximum(t - 1, 0), w_ref[old, 1], v_valid=t > 0))
        carry_t = carry_w_ref[old]
        carry_ref[...] = carry_t
        go_ref[0] = (col_max(carry_t) >= SB_STOP).astype(jnp.int32)

        zl, zr = quarters(z_d)
        zl = jnp.concatenate([cut(zl[:SB_H]), zl[SB_H:SB_T],
                              cut(zl[SB_T:SB_T + SB_H]), zl[SB_T + SB_H:]], axis=0)
        zr = jnp.concatenate([cut(zr[:SB_H]), cut(zr[SB_H:])], axis=0)
        hi_l, lbl_ref[cur], sp0_ref[cur, 0] = softplus_parts(zl)
        hi_r, lbr_ref[cur], _ = softplus_parts(zr)
        hi_ref[cur, 0] = unquarter(hi_l, hi_r)
        hi_ref[cur, 1], lbp_ref[cur], sp0_ref[cur, 1] = softplus_parts(z_p)

        cs_l, cs_r = quarters(cs_d)
        w_l = jnp.exp(lbl_ref[old] + cs_l).astype(BF16)
        w_r = jnp.exp(lbr_ref[old] + cs_r).astype(BF16)
        carry_d = cs_d[:, 0:1] - sp0_ref[old, 0]
        w_ref[cur, 0] = unquarter(w_l, w_r)
        w_ref[cur, 1] = jnp.exp(lbp_ref[old] + cs_p + carry_d).astype(BF16)
        carry_w_ref[cur] = carry_d + (cs_p[:, 0:1] - sp0_ref[old, 1])

    def block(qs, kb):
        hi, log_beta, sp0 = softplus_parts(qk(qs, kb))
        cs = cumsum(hi)
        carry = carry_ref[...]
        w = jnp.exp(log_beta + cs + carry).astype(BF16)
        acc_ref[...] += values(kb, w)
        carry_ref[...] = carry + (cs[:, 0:1] - sp0)

    def cond(state):
        kb, go = state
        return jnp.logical_and(kb >= 0, go > 0)

    def two_steps(i, _):
        for half in range(2):
            s = 2 * i + half
            t = jnp.maximum(s - 2, 0)
            stages(half, jnp.minimum(s, last), t)

            def body(state, t=t):
                kb, _ = state
                block(stack_heads(t), kb)
                go = (jnp.max(carry_ref[...]) >= SB_STOP).astype(jnp.int32)
                return kb - 1, go

            lax.while_loop(cond, body, (t - 2, go_ref[0]))
            rows = pl.ds(pl.multiple_of(t * SB_T, SB_T), SB_T)
            o_ref[rows, :] = jnp.where(lane < HEAD_DIM, acc_ref[:SB_T, :],
                                       acc_ref[SB_T:, :]).astype(BF16)
        return 0

    lax.fori_loop(0, (last + 1 + 2) // 2, two_steps, 0)


def _sb_tri():
    j = np.arange(SB_T)[:, None]
    s = np.arange(SB_T)[None, :]
    return jnp.asarray(np.where(j > s, -1.0, 0.0).astype(np.float32), BF16)


def _sb_attention(proj):
    pair = lambda off: pl.BlockSpec((SEQ, LANES), lambda p: (0, off // LANES + p))
    return pl.pallas_call(
        _sb_kernel,
        grid=(SB_WIDTH // LANES,),
        in_specs=[pair(OFF_Q_SB), pair(OFF_K_SB), pair(OFF_V_SB),
                  pl.BlockSpec((SB_T, SB_T), lambda p: (0, 0))],
        out_specs=pl.BlockSpec((SEQ, LANES), lambda p: (0, p)),
        out_shape=jax.ShapeDtypeStruct((SEQ, SB_WIDTH), BF16),
        scratch_shapes=[
            pltpu.VMEM((2, 2, 2 * SB_T, SB_T), BF16),
            pltpu.VMEM((2, 2 * SB_T, SB_H), F32),
            pltpu.VMEM((2, SB_T, SB_H), F32),
            pltpu.VMEM((2, 2 * SB_T, SB_T), F32),
            pltpu.VMEM((2, 2, 2 * SB_T, 1), F32),
            pltpu.VMEM((2, 2, 2 * SB_T, SB_T), BF16),
            pltpu.VMEM((2, 2 * SB_T, 1), F32),
            pltpu.SMEM((1,), jnp.int32),
            pltpu.VMEM((2 * SB_T, LANES), F32),
            pltpu.VMEM((2 * SB_T, 1), F32),
        ],
        compiler_params=pltpu.CompilerParams(
            dimension_semantics=("arbitrary",), vmem_limit_bytes=VMEM_LIMIT),
        name="sb_attention",
    )(proj, proj, proj, _sb_tri())


def _bucket_table():
    qi = np.arange(BLOCK)[:, None]
    ci = np.arange(2 * BLOCK)[None, :]
    dist = qi + BLOCK - ci
    n = np.maximum(dist, 0)
    max_exact = N_BUCKETS // 2
    nf = np.maximum(n, 1).astype(np.float64)
    val = np.log(nf / max_exact) / math.log(MAX_DISTANCE / max_exact) * (N_BUCKETS - max_exact)
    in_window = (dist >= 0) & (dist < WINDOW)
    frac = val - np.floor(val)
    chk = in_window & (n > max_exact)
    assert np.all((frac[chk] > 1e-3) & (frac[chk] < 1 - 1e-3))
    large = np.minimum(max_exact + val.astype(np.int64), N_BUCKETS - 1)
    bucket = np.where(n < max_exact, n, large)
    later = np.where(in_window, bucket, -1)
    assert all(np.array_equal(later[r], np.roll(later[0], r)) for r in range(BLOCK))
    return jnp.asarray(np.broadcast_to(later[0:1], (8, 2 * BLOCK)).astype(np.int32))


def _bias_kernel(rb_ref, bucket_ref, o_ref):
    b = bucket_ref[...]
    hits = [b == k for k in range(N_BUCKETS)]
    col = lax.broadcasted_iota(jnp.int32, (BLOCK, 2 * BLOCK), 1)
    for h in range(SW_HEADS):
        row0 = jnp.full(b.shape, NEG_BIG, F32)
        for k in range(N_BUCKETS):
            row0 = jnp.where(hits[k], rb_ref[k, h], row0)
        tile = pltpu.roll(jnp.concatenate([row0] * (BLOCK // 8), axis=0), 0, 1,
                          stride=1, stride_axis=0)
        o_ref[1, h] = tile
        o_ref[0, h] = jnp.where(col >= BLOCK, tile, NEG_BIG)


def _bias_table(rel_bias):
    return pl.pallas_call(
        _bias_kernel,
        in_specs=[
            pl.BlockSpec(memory_space=pltpu.SMEM),
            pl.BlockSpec(memory_space=pltpu.VMEM),
        ],
        out_specs=pl.BlockSpec(memory_space=pltpu.VMEM),
        out_shape=jax.ShapeDtypeStruct((2, SW_HEADS, BLOCK, 2 * BLOCK), F32),
        name="t5_bias",
    )(rel_bias.astype(F32), _bucket_table())


def _sw_kernel(sink_ref, qa_ref, qb_ref, k0_ref, k1_ref, k2_ref, v0_ref, v1_ref, v2_ref, bias_ref,
               o_ref, p_ref, st_ref):
    g = pl.program_id(0)

    @pl.when(g == 0)
    def _():
        p_ref[...] = jnp.zeros_like(p_ref)
        st_ref[...] = jnp.zeros_like(st_ref)

    lane = lax.broadcasted_iota(jnp.int32, (BLOCK, LANES), 1)
    ones = jnp.ones((2 * BLOCK, LANES), BF16)
    pairs_per_kv = SW_HEADS // SW_KV_HEADS // 2
    dims = (((1,), (1,)), ((), ()))

    def swap_halves(t):
        return jnp.concatenate([t[:, HEAD_DIM:], t[:, :HEAD_DIM]], axis=1)

    def on_side(kv):
        return (lane >= HEAD_DIM) if kv % 2 else (lane < HEAD_DIM)

    def kv_cols(kv):
        return slice((kv // 2) * LANES, (kv // 2 + 1) * LANES)

    def stages(cur, q_ref, kp_ref, kc_ref, vp_ref, vc_ref, out_rows):
        old = 1 - cur
        k = jnp.concatenate([kp_ref[...], kc_ref[...]], axis=0)
        v = jnp.concatenate([vp_ref[...], vc_ref[...]], axis=0)
        logits, heads = [], []
        for kv in range(SW_KV_HEADS):
            side = kv % 2
            same, other, hs, ho = [], [], [], []
            for pp in range(pairs_per_kv):
                p = kv * pairs_per_kv + pp
                q_pair = q_ref[:, p * LANES:(p + 1) * LANES]
                zero = jnp.zeros_like(q_pair)
                same.append(jnp.where(on_side(kv), q_pair, zero))
                other.append(jnp.where(on_side(kv), swap_halves(q_pair), zero))
                hs.append(2 * p + side)
                ho.append(2 * p + 1 - side)
            lhs = jnp.concatenate(same + other, axis=0)
            logits.append(lax.dot_general(lhs, k[:, kv_cols(kv)], dims,
                                          preferred_element_type=F32))
            heads.append(hs + ho)
        results = []
        for kv in range(SW_KV_HEADS):
            v_pair = v[:, kv_cols(kv)]
            half = pairs_per_kv * BLOCK
            r_same = jnp.dot(p_ref[old, kv, :half, :], jnp.concatenate([v_pair, ones], axis=1),
                             preferred_element_type=F32)
            r_other = jnp.dot(p_ref[old, kv, half:, :],
                              jnp.concatenate([swap_halves(v_pair), ones], axis=1),
                              preferred_element_type=F32)
            results.append((r_same, r_other))

        for kv in range(SW_KV_HEADS):
            maxes = []
            for i, h in enumerate(heads[kv]):
                lg = logits[kv][i * BLOCK:(i + 1) * BLOCK, :] + bias_ref[0, h]
                m = jnp.maximum(jnp.max(lg, axis=-1, keepdims=True), sink_ref[h])
                p_ref[cur, kv, i * BLOCK:(i + 1) * BLOCK, :] = jnp.exp(lg - m).astype(BF16)
                maxes.append(m)
            for pp in range(pairs_per_kv):
                h_same, h_other = heads[kv][pp], heads[kv][pairs_per_kv + pp]
                m = jnp.where(on_side(kv), maxes[pp], maxes[pairs_per_kv + pp])
                sink = jnp.where(on_side(kv), sink_ref[h_same], sink_ref[h_other])
                st_ref[cur, kv * pairs_per_kv + pp] = jnp.exp(sink - m)

        for kv in range(SW_KV_HEADS):
            r_same, r_other = results[kv]
            for pp in range(pairs_per_kv):
                p = kv * pairs_per_kv + pp
                rows = slice(pp * BLOCK, (pp + 1) * BLOCK)
                num = jnp.where(on_side(kv), r_same[rows, :LANES], r_other[rows, :LANES])
                den = jnp.where(on_side(kv), r_same[rows, LANES:], r_other[rows, LANES:])
                o_ref[out_rows, p * LANES:(p + 1) * LANES] = (
                    num / (den + st_ref[old, p])).astype(BF16)

    @pl.when(g >= 0)
    def _():
        stages(1, qa_ref, k0_ref, k1_ref, v0_ref, v1_ref, slice(0, BLOCK))

    @pl.when(g >= 0)
    def _():
        stages(0, qb_ref, k1_ref, k2_ref, v1_ref, v2_ref, slice(BLOCK, 2 * BLOCK))


def _sw_attention(proj, sinks, bias):
    kq = OFF_Q_SW // SW_WIDTH
    kk = OFF_K_SW // SW_KV_WIDTH
    kv = OFF_V_SW // SW_KV_WIDTH
    last = SEQ // BLOCK - 1
    blk = lambda n: jnp.clip(n, 0, last)
    q_spec = lambda off: pl.BlockSpec((BLOCK, SW_WIDTH), lambda g: (blk(2 * g + off), kq))
    k_spec = lambda off: pl.BlockSpec((BLOCK, SW_KV_WIDTH), lambda g: (blk(2 * g + off), kk))
    v_spec = lambda off: pl.BlockSpec((BLOCK, SW_KV_WIDTH), lambda g: (blk(2 * g + off), kv))
    return pl.pallas_call(
        _sw_kernel,
        grid=((last + 1) // 2 + 1,),
        in_specs=[
            pl.BlockSpec(memory_space=pltpu.SMEM),
            q_spec(-1), q_spec(0),
            k_spec(-2), k_spec(-1), k_spec(0),
            v_spec(-3), v_spec(-2), v_spec(-1),
            pl.BlockSpec((1, SW_HEADS, BLOCK, 2 * BLOCK), lambda g: (jnp.minimum(g, 1), 0, 0, 0)),
        ],
        out_specs=pl.BlockSpec((2 * BLOCK, SW_WIDTH), lambda g: (jnp.maximum(g - 1, 0), 0)),
        out_shape=jax.ShapeDtypeStruct((SEQ, SW_WIDTH), BF16),
        scratch_shapes=[
            pltpu.VMEM((2, SW_KV_HEADS, SW_HEADS // SW_KV_HEADS * BLOCK, 2 * BLOCK), BF16),
            pltpu.VMEM((2, SW_HEADS // 2, BLOCK, LANES), F32),
        ],
        compiler_params=pltpu.CompilerParams(
            dimension_semantics=("arbitrary",), vmem_limit_bytes=VMEM_LIMIT),
        name="sw_attention",
    )(sinks.astype(F32), proj, proj, proj, proj, proj, proj, proj, proj, bias)


def _rms(x, g):
    ms = jnp.mean(x * x, axis=-1, keepdims=True)
    return (x * lax.rsqrt(ms + RMS_EPS)) * g


def _silu(z):
    return z * (1.0 / (1.0 + jnp.exp(-z)))


def _out_kernel(osb_ref, osw_ref, zsb_ref, zswa_ref, zswb_ref, gsb_ref, gsw_ref, gpost_ref,
                w_hbm, x_ref, o_ref, w_ref, stage_ref, sem):
    @pl.when(pl.program_id(0) == 0)
    def _():
        _cast_weights(w_hbm, w_ref, stage_ref, sem)

    chunk = OUT_TM // OUT_CHUNKS
    ys = []
    for c in range(OUT_CHUNKS):
        rows = pl.ds(c * chunk, chunk)
        a_sb = _rms(osb_ref[rows, :].astype(F32), gsb_ref[...]) * zsb_ref[rows, :].astype(F32)
        g_sw = jnp.concatenate([zswa_ref[rows, :], zswb_ref[rows, :]], axis=1).astype(F32)
        a_sw = _rms(osw_ref[rows, :].astype(F32), gsw_ref[...]) * g_sw
        y = jnp.dot(a_sb.astype(BF16), w_ref[:SB_WIDTH, :], preferred_element_type=F32)
        ys.append(y + jnp.dot(a_sw.astype(BF16), w_ref[SB_WIDTH:, :],
                              preferred_element_type=F32))
    for c in range(OUT_CHUNKS):
        rows = pl.ds(c * chunk, chunk)
        o_ref[rows, :] = x_ref[rows, :] + _rms(ys[c], gpost_ref[...])


def _outproj(o_sb, o_sw, proj, gn_sb, gn_sw, norm_post, w_out, x2):
    half = SW_WIDTH // 2
    return pl.pallas_call(
        _out_kernel,
        grid=(SEQ // OUT_TM,),
        in_specs=[
            pl.BlockSpec((OUT_TM, SB_WIDTH), lambda i: (i, 0)),
            pl.BlockSpec((OUT_TM, SW_WIDTH), lambda i: (i, 0)),
            pl.BlockSpec((OUT_TM, SB_WIDTH), lambda i: (i, OFF_Z_SB // SB_WIDTH)),
            pl.BlockSpec((OUT_TM, half), lambda i: (i, OFF_Z_SW // half)),
            pl.BlockSpec((OUT_TM, half), lambda i: (i, OFF_Z_SW // half + 1)),
            pl.BlockSpec((1, SB_WIDTH), lambda i: (0, 0)),
            pl.BlockSpec((1, SW_WIDTH), lambda i: (0, 0)),
            pl.BlockSpec((1, D_MODEL), lambda i: (0, 0)),
            pl.BlockSpec(memory_space=pl.ANY),
            pl.BlockSpec((OUT_TM, D_MODEL), lambda i: (i, 0)),
        ],
        out_specs=pl.BlockSpec((OUT_TM, D_MODEL), lambda i: (i, 0)),
        out_shape=jax.ShapeDtypeStruct((SEQ, D_MODEL), F32),
        scratch_shapes=[
            pltpu.VMEM((D_MIX, D_MODEL), BF16),
            pltpu.VMEM((CAST_SLOTS, OUT_CAST_ROWS, D_MODEL), F32),
            pltpu.SemaphoreType.DMA((CAST_SLOTS,)),
        ],
        compiler_params=pltpu.CompilerParams(
            dimension_semantics=("arbitrary",), vmem_limit_bytes=VMEM_LIMIT),
        name="outproj",
    )(o_sb, o_sw, proj, proj, proj, gn_sb, gn_sw, norm_post, w_out, x2)


def kernel(x, w_in, w_out, norm_pre, norm_post, gn_sb, gn_sw, sinks, rel_bias):
    assert x.shape == (1, SEQ, D_MODEL) and w_in.shape == (1, D_MODEL, D_IN_PROJ)
    assert OFF_Z_SW % (SW_WIDTH // 2) == 0 and OFF_Q_SW % SW_WIDTH == 0
    x2 = x.reshape(SEQ, D_MODEL)
    proj = _inproj(x2, norm_pre.astype(F32), w_in[0].astype(F32))
    o_sb = _sb_attention(proj)
    bias = _bias_table(rel_bias)
    o_sw = _sw_attention(proj, sinks[0], bias)
    out = _outproj(o_sb, o_sw, proj, gn_sb.astype(F32), gn_sw.astype(F32), norm_post.astype(F32),
                   w_out[0].astype(F32), x2)
    return out.reshape(1, SEQ, D_MODEL)
```

```python
import math

import numpy as np
import jax
import jax.numpy as jnp
from jax import lax
from jax.experimental import pallas as pl
from jax.experimental.pallas import tpu as pltpu

D_MODEL = 2048
SEQ = 16384
HEAD_DIM = 64
SB_HEADS = 16
SW_HEADS = 16
SW_KV_HEADS = 4
SB_WIDTH = SB_HEADS * HEAD_DIM
SW_WIDTH = SW_HEADS * HEAD_DIM
SW_KV_WIDTH = SW_KV_HEADS * HEAD_DIM
D_MIX = SB_WIDTH + SW_WIDTH
D_IN_PROJ = 4 * SB_WIDTH + 2 * SW_WIDTH + 2 * SW_KV_WIDTH
WINDOW = 128
BLOCK = 128
N_BUCKETS = 32
MAX_DISTANCE = 128
RMS_EPS = 1e-6
SCALE = HEAD_DIM ** -0.5

OFF_Q_SB = 0
OFF_K_SB = SB_WIDTH
OFF_V_SB = 2 * SB_WIDTH
OFF_Z_SB = 3 * SB_WIDTH
OFF_Q_SW = 4 * SB_WIDTH
OFF_K_SW = OFF_Q_SW + SW_WIDTH
OFF_V_SW = OFF_K_SW + SW_KV_WIDTH
OFF_Z_SW = OFF_V_SW + SW_KV_WIDTH

LANES = 128
VMEM_LIMIT = 56 * 1024 * 1024

PROJ_TM = 512
PROJ_TN = 3328
CAST_ROWS = 16
OUT_CAST_ROWS = 32
CAST_SLOTS = 8
SB_T = 256
SB_H = SB_T // 2
SW_STEPS = 4
OUT_TM = 512
OUT_CHUNKS = 2
NEG_BIG = -1e30
SB_STOP = -110.0

F32 = jnp.float32
BF16 = jnp.bfloat16


def _cast_weights(w_hbm, w_ref, stage_ref, sem):
    slots, rows = stage_ref.shape[0], stage_ref.shape[1]
    n_chunks = w_hbm.shape[0] // rows
    ahead = slots - 1

    def chunk_copy(c):
        slot = c % slots
        src = w_hbm.at[pl.ds(pl.multiple_of(c * rows, rows), rows), :]
        return pltpu.make_async_copy(src, stage_ref.at[slot], sem.at[slot])

    for c in range(ahead):
        chunk_copy(c).start()

    def body(c, _):
        @pl.when(c + ahead < n_chunks)
        def _():
            chunk_copy(c + ahead).start()

        chunk_copy(c).wait()
        w_ref[pl.ds(pl.multiple_of(c * rows, rows), rows), :] = stage_ref[c % slots].astype(BF16)
        return 0

    lax.fori_loop(0, n_chunks, body, 0)


def _inproj_kernel(x_ref, g_ref, w_hbm, o_ref, w_ref, stage_ref, sem):
    @pl.when(pl.program_id(0) == 0)
    def _():
        _cast_weights(w_hbm, w_ref, stage_ref, sem)

    x = x_ref[...]
    ms = jnp.mean(x * x, axis=-1, keepdims=True)
    xn = ((x * lax.rsqrt(ms + RMS_EPS)) * g_ref[...]).astype(BF16)
    gates = ((OFF_Z_SB, OFF_Z_SB + SB_WIDTH), (OFF_Z_SW, OFF_Z_SW + SW_WIDTH))
    for j in reversed(range(D_IN_PROJ // PROJ_TN)):
        lo, hi = j * PROJ_TN, (j + 1) * PROJ_TN
        res = jnp.dot(xn, w_ref[:, lo:hi], preferred_element_type=F32)
        cuts = sorted({lo, hi} | {min(max(c, lo), hi) for g in gates for c in g})
        for a, b in zip(cuts[:-1], cuts[1:]):
            part = res[:, a - lo:b - lo]
            if any(g0 <= a and b <= g1 for g0, g1 in gates):
                part = _silu(part)
            o_ref[:, a:b] = part.astype(BF16)


def _inproj(x2, g, w):
    return pl.pallas_call(
        _inproj_kernel,
        grid=(SEQ // PROJ_TM,),
        in_specs=[
            pl.BlockSpec((PROJ_TM, D_MODEL), lambda i: (i, 0)),
            pl.BlockSpec((1, D_MODEL), lambda i: (0, 0)),
            pl.BlockSpec(memory_space=pl.ANY),
        ],
        out_specs=pl.BlockSpec((PROJ_TM, D_IN_PROJ), lambda i: (i, 0)),
        out_shape=jax.ShapeDtypeStruct((SEQ, D_IN_PROJ), BF16),
        scratch_shapes=[
            pltpu.VMEM((D_MODEL, D_IN_PROJ), BF16),
            pltpu.VMEM((CAST_SLOTS, CAST_ROWS, D_IN_PROJ), F32),
            pltpu.SemaphoreType.DMA((CAST_SLOTS,)),
        ],
        compiler_params=pltpu.CompilerParams(
            dimension_semantics=("arbitrary",), vmem_limit_bytes=VMEM_LIMIT),
        name="inproj",
    )(x2, g, w)


def _sb_kernel(q_ref, k_ref, v_ref, tri_ref, o_ref,
               hi_ref, lbl_ref, lbr_ref, lbp_ref, sp0_ref, w_ref, carry_w_ref, go_ref,
               acc_ref, carry_ref):
    last = SEQ // SB_T - 1
    for ref in (hi_ref, lbl_ref, lbr_ref, lbp_ref, sp0_ref, w_ref, carry_w_ref):
        ref[...] = jnp.zeros_like(ref)

    lane = lax.broadcasted_iota(jnp.int32, (SB_T, LANES), 1)

    def stack_heads(tile):
        start = pl.multiple_of(tile * SB_T, SB_T)
        q = q_ref[pl.ds(start, SB_T), :] * jnp.asarray(SCALE, BF16)
        zero = jnp.zeros_like(q)
        return jnp.concatenate([jnp.where(lane < HEAD_DIM, q, zero),
                                jnp.where(lane >= HEAD_DIM, q, zero)], axis=0)

    tri = tri_ref[...]
    r_h = lax.broadcasted_iota(jnp.int32, (SB_H, SB_H), 0)
    c_h = lax.broadcasted_iota(jnp.int32, (SB_H, SB_H), 1)
    cut_add = jnp.where(c_h < r_h, 0.0, NEG_BIG).astype(F32)
    zeros_h = jnp.zeros((SB_H, SB_H), BF16)

    def qk(qs, kb):
        start = pl.multiple_of(kb * SB_T, SB_T)
        kblk = k_ref[pl.ds(start, SB_T), :]
        return lax.dot_general(qs, kblk, (((1,), (1,)), ((), ())), preferred_element_type=F32)

    def softplus_parts(z):
        neg_abs = lax.bitcast_convert_type(
            lax.bitcast_convert_type(z, jnp.int32) | jnp.int32(-2 ** 31), F32)
        sp = jnp.maximum(z, 0.0) + jnp.log(1.0 + jnp.exp(neg_abs))
        return sp.astype(BF16), z - sp, sp[:, 0:1]

    def cut(z):
        return z + cut_add

    def col_max(c):
        return jnp.max(jnp.max(c.reshape(c.shape[0] // 8, 8, 1), axis=0))

    def quarters(a):
        left = a[:, :SB_H]
        right = jnp.concatenate([a[SB_H:SB_T, SB_H:], a[SB_T + SB_H:, SB_H:]], axis=0)
        return left, right

    def unquarter(left, right):
        right_full = jnp.concatenate([zeros_h, right[:SB_H], zeros_h, right[SB_H:]], axis=0)
        return jnp.concatenate([left, right_full], axis=1)

    def cumsum(hi):
        return jnp.dot(hi, tri, preferred_element_type=F32)

    def values(kb, w, v_valid=None):
        start = pl.multiple_of(kb * SB_T, SB_T)
        vblk = v_ref[pl.ds(start, SB_T), :]
        if v_valid is not None:
            vblk = jnp.where(v_valid, vblk, jnp.zeros_like(vblk))
        return jnp.dot(w, vblk, preferred_element_type=F32)

    def stages(cur, n, t):
        old = 1 - cur
        qs_n = stack_heads(n)
        z_d = qk(qs_n, n)
        z_p = qk(qs_n, jnp.maximum(n - 1, 0))
        cs_d = cumsum(hi_ref[old, 0])
        cs_p = cumsum(hi_ref[old, 1])
        acc_ref[...] = (values(t, w_ref[old, 0])
                        + values(jnp.maximum(t - 1, 0), w_ref[old, 1], v_valid=t > 0))
        carry_t = carry_w_ref[old]
        carry_ref[...] = carry_t
        go_ref[0] = (col_max(carry_t) >= SB_STOP).astype(jnp.int32)

        zl, zr = quarters(z_d)
        zl = jnp.concatenate([cut(zl[:SB_H]), zl[SB_H:SB_T],
                              cut(zl[SB_T:SB_T + SB_H]), zl[SB_T + SB_H:]], axis=0)
        zr = jnp.concatenate([cut(zr[:SB_H]), cut(zr[SB_H:])], axis=0)
        hi_l, lbl_ref[cur], sp0_ref[cur, 0] = softplus_parts(zl)
        hi_r, lbr_ref[cur], _ = softplus_parts(zr)
        hi_ref[cur, 0] = unquarter(hi_l, hi_r)
        hi_ref[cur, 1], lbp_ref[cur], sp0_ref[cur, 1] = softplus_parts(z_p)

        cs_l, cs_r = quarters(cs_d)
        w_l = jnp.exp(lbl_ref[old] + cs_l).astype(BF16)
        w_r = jnp.exp(lbr_ref[old] + cs_r).astype(BF16)
        carry_d = cs_d[:, 0:1] - sp0_ref[old, 0]
        w_ref[cur, 0] = unquarter(w_l, w_r)
        w_ref[cur, 1] = jnp.exp(lbp_ref[old] + cs_p + carry_d).astype(BF16)
        carry_w_ref[cur] = carry_d + (cs_p[:, 0:1] - sp0_ref[old, 1])

    def block(qs, kb):
        hi, log_beta, sp0 = softplus_parts(qk(qs, kb))
        cs = cumsum(hi)
        carry = carry_ref[...]
        w = jnp.exp(log_beta + cs + carry).astype(BF16)
        acc_ref[...] += values(kb, w)
        carry_ref[...] = carry + (cs[:, 0:1] - sp0)

    def cond(state):
        kb, go = state
        return jnp.logical_and(kb >= 0, go > 0)

    def two_steps(i, _):
        for half in range(2):
            s = 2 * i + half
            t = jnp.maximum(s - 2, 0)
            stages(half, jnp.minimum(s, last), t)

            def body(state, t=t):
                kb, _ = state
                block(stack_heads(t), kb)
                go = (jnp.max(carry_ref[...]) >= SB_STOP).astype(jnp.int32)
                return kb - 1, go

            lax.while_loop(cond, body, (t - 2, go_ref[0]))
            rows = pl.ds(pl.multiple_of(t * SB_T, SB_T), SB_T)
            o_ref[rows, :] = jnp.where(lane < HEAD_DIM, acc_ref[:SB_T, :],
                                       acc_ref[SB_T:, :]).astype(BF16)
        return 0

    lax.fori_loop(0, (last + 1 + 2) // 2, two_steps, 0)


def _sb_tri():
    j = np.arange(SB_T)[:, None]
    s = np.arange(SB_T)[None, :]
    return jnp.asarray(np.where(j > s, -1.0, 0.0).astype(np.float32), BF16)


def _sb_attention(proj):
    pair = lambda off: pl.BlockSpec((SEQ, LANES), lambda p: (0, off // LANES + p))
    return pl.pallas_call(
        _sb_kernel,
        grid=(SB_WIDTH // LANES,),
        in_specs=[pair(OFF_Q_SB), pair(OFF_K_SB), pair(OFF_V_SB),
                  pl.BlockSpec((SB_T, SB_T), lambda p: (0, 0))],
        out_specs=pl.BlockSpec((SEQ, LANES), lambda p: (0, p)),
        out_shape=jax.ShapeDtypeStruct((SEQ, SB_WIDTH), BF16),
        scratch_shapes=[
            pltpu.VMEM((2, 2, 2 * SB_T, SB_T), BF16),
            pltpu.VMEM((2, 2 * SB_T, SB_H), F32),
            pltpu.VMEM((2, SB_T, SB_H), F32),
            pltpu.VMEM((2, 2 * SB_T, SB_T), F32),
            pltpu.VMEM((2, 2, 2 * SB_T, 1), F32),
            pltpu.VMEM((2, 2, 2 * SB_T, SB_T), BF16),
            pltpu.VMEM((2, 2 * SB_T, 1), F32),
            pltpu.SMEM((1,), jnp.int32),
            pltpu.VMEM((2 * SB_T, LANES), F32),
            pltpu.VMEM((2 * SB_T, 1), F32),
        ],
        compiler_params=pltpu.CompilerParams(
            dimension_semantics=("arbitrary",), vmem_limit_bytes=VMEM_LIMIT),
        name="sb_attention",
    )(proj, proj, proj, _sb_tri())


def _bucket_table():
    qi = np.arange(BLOCK)[:, None]
    ci = np.arange(2 * BLOCK)[None, :]
    dist = qi + BLOCK - ci
    n = np.maximum(dist, 0)
    max_exact = N_BUCKETS // 2
    nf = np.maximum(n, 1).astype(np.float64)
    val = np.log(nf / max_exact) / math.log(MAX_DISTANCE / max_exact) * (N_BUCKETS - max_exact)
    in_window = (dist >= 0) & (dist < WINDOW)
    frac = val - np.floor(val)
    chk = in_window & (n > max_exact)
    assert np.all((frac[chk] > 1e-3) & (frac[chk] < 1 - 1e-3))
    large = np.minimum(max_exact + val.astype(np.int64), N_BUCKETS - 1)
    bucket = np.where(n < max_exact, n, large)
    later = np.where(in_window, bucket, -1)
    assert all(np.array_equal(later[r], np.roll(later[0], r)) for r in range(BLOCK))
    return jnp.asarray(np.broadcast_to(later[0:1], (8, 2 * BLOCK)).astype(np.int32))


def _bias_kernel(rb_ref, bucket_ref, o_ref):
    b = bucket_ref[...]
    hits = [b == k for k in range(N_BUCKETS)]
    col = lax.broadcasted_iota(jnp.int32, (BLOCK, 2 * BLOCK), 1)
    for h in range(SW_HEADS):
        row0 = jnp.full(b.shape, NEG_BIG, F32)
        for k in range(N_BUCKETS):
            row0 = jnp.where(hits[k], rb_ref[k, h], row0)
        tile = pltpu.roll(jnp.concatenate([row0] * (BLOCK // 8), axis=0), 0, 1,
                          stride=1, stride_axis=0)
        o_ref[1, h] = tile
        o_ref[0, h] = jnp.where(col >= BLOCK, tile, NEG_BIG)


def _bias_table(rel_bias):
    return pl.pallas_call(
        _bias_kernel,
        in_specs=[
            pl.BlockSpec(memory_space=pltpu.SMEM),
            pl.BlockSpec(memory_space=pltpu.VMEM),
        ],
        out_specs=pl.BlockSpec(memory_space=pltpu.VMEM),
        out_shape=jax.ShapeDtypeStruct((2, SW_HEADS, BLOCK, 2 * BLOCK), F32),
        name="t5_bias",
    )(rel_bias.astype(F32), _bucket_table())


def _sw_kernel(sink_ref, *refs):
    n = SW_STEPS
    q_refs, k_refs, v_refs = refs[:n], refs[n:2 * n + 1], refs[2 * n + 1:3 * n + 2]
    bias_ref, o_ref, p_ref, st_ref = refs[3 * n + 2:]
    g = pl.program_id(0)

    @pl.when(g == 0)
    def _():
        p_ref[...] = jnp.zeros_like(p_ref)
        st_ref[...] = jnp.zeros_like(st_ref)

    lane = lax.broadcasted_iota(jnp.int32, (BLOCK, LANES), 1)
    ones = jnp.ones((2 * BLOCK, LANES), BF16)
    scale = jnp.asarray(SCALE, BF16)
    pairs_per_kv = SW_HEADS // SW_KV_HEADS // 2
    dims = (((1,), (1,)), ((), ()))

    def swap_halves(t):
        return jnp.concatenate([t[:, HEAD_DIM:], t[:, :HEAD_DIM]], axis=1)

    def on_side(kv):
        return (lane >= HEAD_DIM) if kv % 2 else (lane < HEAD_DIM)

    def kv_cols(kv):
        return slice((kv // 2) * LANES, (kv // 2 + 1) * LANES)

    def stages(cur, q_ref, kp_ref, kc_ref, vp_ref, vc_ref, out_rows):
        old = 1 - cur
        k = jnp.concatenate([kp_ref[...], kc_ref[...]], axis=0)
        v = jnp.concatenate([vp_ref[...], vc_ref[...]], axis=0)
        logits, heads = [], []
        for kv in range(SW_KV_HEADS):
            side = kv % 2
            same, other, hs, ho = [], [], [], []
            for pp in range(pairs_per_kv):
                p = kv * pairs_per_kv + pp
                q_pair = q_ref[:, p * LANES:(p + 1) * LANES] * scale
                zero = jnp.zeros_like(q_pair)
                same.append(jnp.where(on_side(kv), q_pair, zero))
                other.append(jnp.where(on_side(kv), swap_halves(q_pair), zero))
                hs.append(2 * p + side)
                ho.append(2 * p + 1 - side)
            lhs = jnp.concatenate(same + other, axis=0)
            logits.append(lax.dot_general(lhs, k[:, kv_cols(kv)], dims,
                                          preferred_element_type=F32))
            heads.append(hs + ho)
        results = []
        for kv in range(SW_KV_HEADS):
            v_pair = v[:, kv_cols(kv)]
            half = pairs_per_kv * BLOCK
            r_same = jnp.dot(p_ref[old, kv, :half, :], jnp.concatenate([v_pair, ones], axis=1),
                             preferred_element_type=F32)
            r_other = jnp.dot(p_ref[old, kv, half:, :],
                              jnp.concatenate([swap_halves(v_pair), ones], axis=1),
                              preferred_element_type=F32)
            results.append((r_same, r_other))

        for kv in range(SW_KV_HEADS):
            maxes = []
            for i, h in enumerate(heads[kv]):
                lg = logits[kv][i * BLOCK:(i + 1) * BLOCK, :] + bias_ref[0, h]
                m = jnp.maximum(jnp.max(lg, axis=-1, keepdims=True), sink_ref[h])
                p_ref[cur, kv, i * BLOCK:(i + 1) * BLOCK, :] = jnp.exp(lg - m).astype(BF16)
                maxes.append(m)
            for pp in range(pairs_per_kv):
                h_same, h_other = heads[kv][pp], heads[kv][pairs_per_kv + pp]
                m = jnp.where(on_side(kv), maxes[pp], maxes[pairs_per_kv + pp])
                sink = jnp.where(on_side(kv), sink_ref[h_same], sink_ref[h_other])
                st_ref[cur, kv * pairs_per_kv + pp] = jnp.exp(sink - m)

        for kv in range(SW_KV_HEADS):
            r_same, r_other = results[kv]
            for pp in range(pairs_per_kv):
                p = kv * pairs_per_kv + pp
                rows = slice(pp * BLOCK, (pp + 1) * BLOCK)
                num = jnp.where(on_side(kv), r_same[rows, :LANES], r_other[rows, :LANES])
                den = jnp.where(on_side(kv), r_same[rows, LANES:], r_other[rows, LANES:])
                o_ref[out_rows, p * LANES:(p + 1) * LANES] = (
                    num / (den + st_ref[old, p])).astype(BF16)

    for i in range(n):
        @pl.when(g >= 0)
        def _(i=i):
            stages((i + 1) % 2, q_refs[i], k_refs[i], k_refs[i + 1], v_refs[i], v_refs[i + 1],
                   slice(i * BLOCK, (i + 1) * BLOCK))


def _sw_attention(proj, sinks, bias):
    kq = OFF_Q_SW // SW_WIDTH
    kk = OFF_K_SW // SW_KV_WIDTH
    kv = OFF_V_SW // SW_KV_WIDTH
    last = SEQ // BLOCK - 1
    n = SW_STEPS
    assert n % 2 == 0 and (last + 1) % n == 0
    spec = lambda width, col, off: pl.BlockSpec(
        (BLOCK, width), lambda g: (jnp.clip(n * g + off, 0, last), col))
    return pl.pallas_call(
        _sw_kernel,
        grid=((last + 1) // n + 1,),
        in_specs=(
            [pl.BlockSpec(memory_space=pltpu.SMEM)]
            + [spec(SW_WIDTH, kq, off) for off in range(-(n - 1), 1)]
            + [spec(SW_KV_WIDTH, kk, off) for off in range(-n, 1)]
            + [spec(SW_KV_WIDTH, kv, off) for off in range(-n - 1, 0)]
            + [pl.BlockSpec((1, SW_HEADS, BLOCK, 2 * BLOCK),
                            lambda g: (jnp.minimum(g, 1), 0, 0, 0))]),
        out_specs=pl.BlockSpec((n * BLOCK, SW_WIDTH), lambda g: (jnp.maximum(g - 1, 0), 0)),
        out_shape=jax.ShapeDtypeStruct((SEQ, SW_WIDTH), BF16),
        scratch_shapes=[
            pltpu.VMEM((2, SW_KV_HEADS, SW_HEADS // SW_KV_HEADS * BLOCK, 2 * BLOCK), BF16),
            pltpu.VMEM((2, SW_HEADS // 2, BLOCK, LANES), F32),
        ],
        compiler_params=pltpu.CompilerParams(
            dimension_semantics=("arbitrary",), vmem_limit_bytes=VMEM_LIMIT),
        name="sw_attention",
    )(sinks.astype(F32), *([proj] * (3 * n + 2)), bias)


def _rms(x, g):
    ms = jnp.mean(x * x, axis=-1, keepdims=True)
    return (x * lax.rsqrt(ms + RMS_EPS)) * g


def _silu(z):
    return z * (1.0 / (1.0 + jnp.exp(-z)))


def _out_kernel(osb_ref, osw_ref, zsb_ref, zswa_ref, zswb_ref, gsb_ref, gsw_ref, gpost_ref,
                w_hbm, x_ref, o_ref, w_ref, stage_ref, sem):
    @pl.when(pl.program_id(0) == 0)
    def _():
        _cast_weights(w_hbm, w_ref, stage_ref, sem)

    chunk = OUT_TM // OUT_CHUNKS
    ys = []
    for c in range(OUT_CHUNKS):
        rows = pl.ds(c * chunk, chunk)
        a_sb = _rms(osb_ref[rows, :].astype(F32), gsb_ref[...]) * zsb_ref[rows, :].astype(F32)
        g_sw = jnp.concatenate([zswa_ref[rows, :], zswb_ref[rows, :]], axis=1).astype(F32)
        a_sw = _rms(osw_ref[rows, :].astype(F32), gsw_ref[...]) * g_sw
        y = jnp.dot(a_sb.astype(BF16), w_ref[:SB_WIDTH, :], preferred_element_type=F32)
        ys.append(y + jnp.dot(a_sw.astype(BF16), w_ref[SB_WIDTH:, :],
                              preferred_element_type=F32))
    for c in range(OUT_CHUNKS):
        rows = pl.ds(c * chunk, chunk)
        o_ref[rows, :] = x_ref[rows, :] + _rms(ys[c], gpost_ref[...])


def _outproj(o_sb, o_sw, proj, gn_sb, gn_sw, norm_post, w_out, x2):
    half = SW_WIDTH // 2
    return pl.pallas_call(
        _out_kernel,
        grid=(SEQ // OUT_TM,),
        in_specs=[
            pl.BlockSpec((OUT_TM, SB_WIDTH), lambda i: (i, 0)),
            pl.BlockSpec((OUT_TM, SW_WIDTH), lambda i: (i, 0)),
            pl.BlockSpec((OUT_TM, SB_WIDTH), lambda i: (i, OFF_Z_SB // SB_WIDTH)),
            pl.BlockSpec((OUT_TM, half), lambda i: (i, OFF_Z_SW // half)),
            pl.BlockSpec((OUT_TM, half), lambda i: (i, OFF_Z_SW // half + 1)),
            pl.BlockSpec((1, SB_WIDTH), lambda i: (0, 0)),
            pl.BlockSpec((1, SW_WIDTH), lambda i: (0, 0)),
            pl.BlockSpec((1, D_MODEL), lambda i: (0, 0)),
            pl.BlockSpec(memory_space=pl.ANY),
            pl.BlockSpec((OUT_TM, D_MODEL), lambda i: (i, 0)),
        ],
        out_specs=pl.BlockSpec((OUT_TM, D_MODEL), lambda i: (i, 0)),
        out_shape=jax.ShapeDtypeStruct((SEQ, D_MODEL), F32),
        scratch_shapes=[
            pltpu.VMEM((D_MIX, D_MODEL), BF16),
            pltpu.VMEM((CAST_SLOTS, OUT_CAST_ROWS, D_MODEL), F32),
            pltpu.SemaphoreType.DMA((CAST_SLOTS,)),
        ],
        compiler_params=pltpu.CompilerParams(
            dimension_semantics=("arbitrary",), vmem_limit_bytes=VMEM_LIMIT),
        name="outproj",
    )(o_sb, o_sw, proj, proj, proj, gn_sb, gn_sw, norm_post, w_out, x2)


def kernel(x, w_in, w_out, norm_pre, norm_post, gn_sb, gn_sw, sinks, rel_bias):
    assert x.shape == (1, SEQ, D_MODEL) and w_in.shape == (1, D_MODEL, D_IN_PROJ)
    assert OFF_Z_SW % (SW_WIDTH // 2) == 0 and OFF_Q_SW % SW_WIDTH == 0
    x2 = x.reshape(SEQ, D_MODEL)
    proj = _inproj(x2, norm_pre.astype(F32), w_in[0].astype(F32))
    o_sb = _sb_attention(proj)
    bias = _bias_table(rel_bias)
    o_sw = _sw_attention(proj, sinks[0], bias)
    out = _outproj(o_sb, o_sw, proj, gn_sb.astype(F32), gn_sw.astype(F32), norm_post.astype(F32),
                   w_out[0].astype(F32), x2)
    return out.reshape(1, SEQ, D_MODEL)
```

```python
import math

import numpy as np
import jax
import jax.numpy as jnp
from jax import lax
from jax.experimental import pallas as pl
from jax.experimental.pallas import tpu as pltpu

D_MODEL = 2048
SEQ = 16384
HEAD_DIM = 64
SB_HEADS = 16
SW_HEADS = 16
SW_KV_HEADS = 4
SB_WIDTH = SB_HEADS * HEAD_DIM
SW_WIDTH = SW_HEADS * HEAD_DIM
SW_KV_WIDTH = SW_KV_HEADS * HEAD_DIM
D_MIX = SB_WIDTH + SW_WIDTH
D_IN_PROJ = 4 * SB_WIDTH + 2 * SW_WIDTH + 2 * SW_KV_WIDTH
WINDOW = 128
BLOCK = 128
N_BUCKETS = 32
MAX_DISTANCE = 128
RMS_EPS = 1e-6
SCALE = HEAD_DIM ** -0.5

OFF_Q_SB = 0
OFF_K_SB = SB_WIDTH
OFF_V_SB = 2 * SB_WIDTH
OFF_Z_SB = 3 * SB_WIDTH
OFF_Q_SW = 4 * SB_WIDTH
OFF_K_SW = OFF_Q_SW + SW_WIDTH
OFF_V_SW = OFF_K_SW + SW_KV_WIDTH
OFF_Z_SW = OFF_V_SW + SW_KV_WIDTH

LANES = 128
VMEM_LIMIT = 56 * 1024 * 1024

PROJ_TM = 512
PROJ_TN = 3328
CAST_ROWS = 16
OUT_CAST_ROWS = 32
CAST_SLOTS = 8
SB_T = 256
SB_H = SB_T // 2
OUT_TM = 512
OUT_CHUNKS = 2
NEG_BIG = -1e30
SB_STOP = -110.0

F32 = jnp.float32
BF16 = jnp.bfloat16


def _cast_weights(w_hbm, w_ref, stage_ref, sem):
    slots, rows = stage_ref.shape[0], stage_ref.shape[1]
    n_chunks = w_hbm.shape[0] // rows
    ahead = slots - 1

    def chunk_copy(c):
        slot = c % slots
        src = w_hbm.at[pl.ds(pl.multiple_of(c * rows, rows), rows), :]
        return pltpu.make_async_copy(src, stage_ref.at[slot], sem.at[slot])

    for c in range(ahead):
        chunk_copy(c).start()

    def body(c, _):
        @pl.when(c + ahead < n_chunks)
        def _():
            chunk_copy(c + ahead).start()

        chunk_copy(c).wait()
        w_ref[pl.ds(pl.multiple_of(c * rows, rows), rows), :] = stage_ref[c % slots].astype(BF16)
        return 0

    lax.fori_loop(0, n_chunks, body, 0)


def _inproj_kernel(x_ref, g_ref, w_hbm, o_ref, w_ref, stage_ref, sem):
    @pl.when(pl.program_id(0) == 0)
    def _():
        _cast_weights(w_hbm, w_ref, stage_ref, sem)

    x = x_ref[...]
    ms = jnp.mean(x * x, axis=-1, keepdims=True)
    xn = ((x * lax.rsqrt(ms + RMS_EPS)) * g_ref[...]).astype(BF16)
    gates = ((OFF_Z_SB, OFF_Z_SB + SB_WIDTH), (OFF_Z_SW, OFF_Z_SW + SW_WIDTH))
    for j in reversed(range(D_IN_PROJ // PROJ_TN)):
        lo, hi = j * PROJ_TN, (j + 1) * PROJ_TN
        res = jnp.dot(xn, w_ref[:, lo:hi], preferred_element_type=F32)
        cuts = sorted({lo, hi} | {min(max(c, lo), hi) for g in gates for c in g})
        for a, b in zip(cuts[:-1], cuts[1:]):
            part = res[:, a - lo:b - lo]
            if any(g0 <= a and b <= g1 for g0, g1 in gates):
                part = _silu(part)
            o_ref[:, a:b] = part.astype(BF16)


def _inproj(x2, g, w):
    return pl.pallas_call(
        _inproj_kernel,
        grid=(SEQ // PROJ_TM,),
        in_specs=[
            pl.BlockSpec((PROJ_TM, D_MODEL), lambda i: (i, 0)),
            pl.BlockSpec((1, D_MODEL), lambda i: (0, 0)),
            pl.BlockSpec(memory_space=pl.ANY),
        ],
        out_specs=pl.BlockSpec((PROJ_TM, D_IN_PROJ), lambda i: (i, 0)),
        out_shape=jax.ShapeDtypeStruct((SEQ, D_IN_PROJ), BF16),
        scratch_shapes=[
            pltpu.VMEM((D_MODEL, D_IN_PROJ), BF16),
            pltpu.VMEM((CAST_SLOTS, CAST_ROWS, D_IN_PROJ), F32),
            pltpu.SemaphoreType.DMA((CAST_SLOTS,)),
        ],
        compiler_params=pltpu.CompilerParams(
            dimension_semantics=("arbitrary",), vmem_limit_bytes=VMEM_LIMIT),
        name="inproj",
    )(x2, g, w)


def _sb_kernel(q_ref, k_ref, v_ref, tri_ref, o_ref,
               hi_ref, lbl_ref, lbr_ref, lbp_ref, sp0_ref, w_ref, carry_w_ref, go_ref,
               acc_ref, carry_ref):
    last = SEQ // SB_T - 1
    for ref in (hi_ref, lbl_ref, lbr_ref, lbp_ref, sp0_ref, w_ref, carry_w_ref):
        ref[...] = jnp.zeros_like(ref)

    lane = lax.broadcasted_iota(jnp.int32, (SB_T, LANES), 1)

    def stack_heads(tile):
        start = pl.multiple_of(tile * SB_T, SB_T)
        q = q_ref[pl.ds(start, SB_T), :] * jnp.asarray(SCALE, BF16)
        zero = jnp.zeros_like(q)
        return jnp.concatenate([jnp.where(lane < HEAD_DIM, q, zero),
                                jnp.where(lane >= HEAD_DIM, q, zero)], axis=0)

    tri = tri_ref[...]
    r_h = lax.broadcasted_iota(jnp.int32, (SB_H, SB_H), 0)
    c_h = lax.broadcasted_iota(jnp.int32, (SB_H, SB_H), 1)
    cut_add = jnp.where(c_h < r_h, 0.0, NEG_BIG).astype(F32)
    zeros_h = jnp.zeros((SB_H, SB_H), BF16)

    def qk(qs, kb):
        start = pl.multiple_of(kb * SB_T, SB_T)
        kblk = k_ref[pl.ds(start, SB_T), :]
        return lax.dot_general(qs, kblk, (((1,), (1,)), ((), ())), preferred_element_type=F32)

    def softplus_parts(z):
        neg_abs = lax.bitcast_convert_type(
            lax.bitcast_convert_type(z, jnp.int32) | jnp.int32(-2 ** 31), F32)
        sp = jnp.maximum(z, 0.0) + jnp.log(1.0 + jnp.exp(neg_abs))
        return sp.astype(BF16), z - sp, sp[:, 0:1]

    def cut(z):
        return z + cut_add

    def col_max(c):
        return jnp.max(jnp.max(c.reshape(c.shape[0] // 8, 8, 1), axis=0))

    def quarters(a):
        left = a[:, :SB_H]
        right = jnp.concatenate([a[SB_H:SB_T, SB_H:], a[SB_T + SB_H:, SB_H:]], axis=0)
        return left, right

    def unquarter(left, right):
        right_full = jnp.concatenate([zeros_h, right[:SB_H], zeros_h, right[SB_H:]], axis=0)
        return jnp.concatenate([left, right_full], axis=1)

    def cumsum(hi):
        return jnp.dot(hi, tri, preferred_element_type=F32)

    def values(kb, w, v_valid=None):
        start = pl.multiple_of(kb * SB_T, SB_T)
        vblk = v_ref[pl.ds(start, SB_T), :]
        if v_valid is not None:
            vblk = jnp.where(v_valid, vblk, jnp.zeros_like(vblk))
        return jnp.dot(w, vblk, preferred_element_type=F32)

    def near_values(old, t):
        return (values(t, w_ref[old, 0])
                + values(jnp.maximum(t - 1, 0), w_ref[old, 1], v_valid=t > 0))

    def write_out(t, acc):
        rows = pl.ds(pl.multiple_of(t * SB_T, SB_T), SB_T)
        o_ref[rows, :] = jnp.where(lane < HEAD_DIM, acc[:SB_T], acc[SB_T:]).astype(BF16)

    def stages(cur, n, t):
        old = 1 - cur
        qs_n = stack_heads(n)
        z_d = qk(qs_n, n)
        z_p = qk(qs_n, jnp.maximum(n - 1, 0))
        cs_d = cumsum(hi_ref[old, 0])
        cs_p = cumsum(hi_ref[old, 1])
        write_out(t, near_values(old, t))
        go_ref[0] = (col_max(carry_w_ref[old]) >= SB_STOP).astype(jnp.int32)

        zl, zr = quarters(z_d)
        zl = jnp.concatenate([cut(zl[:SB_H]), zl[SB_H:SB_T],
                              cut(zl[SB_T:SB_T + SB_H]), zl[SB_T + SB_H:]], axis=0)
        zr = jnp.concatenate([cut(zr[:SB_H]), cut(zr[SB_H:])], axis=0)
        hi_l, lbl_ref[cur], sp0_ref[cur, 0] = softplus_parts(zl)
        hi_r, lbr_ref[cur], _ = softplus_parts(zr)
        hi_ref[cur, 0] = unquarter(hi_l, hi_r)
        hi_ref[cur, 1], lbp_ref[cur], sp0_ref[cur, 1] = softplus_parts(z_p)

        cs_l, cs_r = quarters(cs_d)
        w_l = jnp.exp(lbl_ref[old] + cs_l).astype(BF16)
        w_r = jnp.exp(lbr_ref[old] + cs_r).astype(BF16)
        carry_d = cs_d[:, 0:1] - sp0_ref[old, 0]
        w_ref[cur, 0] = unquarter(w_l, w_r)
        w_ref[cur, 1] = jnp.exp(lbp_ref[old] + cs_p + carry_d).astype(BF16)
        carry_w_ref[cur] = carry_d + (cs_p[:, 0:1] - sp0_ref[old, 1])

    def block(qs, kb):
        hi, log_beta, sp0 = softplus_parts(qk(qs, kb))
        cs = cumsum(hi)
        carry = carry_ref[...]
        w = jnp.exp(log_beta + cs + carry).astype(BF16)
        acc_ref[...] += values(kb, w)
        carry_ref[...] = carry + (cs[:, 0:1] - sp0)

    def cond(state):
        kb, go = state
        return jnp.logical_and(kb >= 0, go > 0)

    def two_steps(i, _):
        for half in range(2):
            s = 2 * i + half
            t = jnp.maximum(s - 2, 0)
            stages(half, jnp.minimum(s, last), t)

            @pl.when(jnp.logical_and(go_ref[0] > 0, t >= 2))
            def _(t=t, old=1 - half):
                acc_ref[...] = near_values(old, t)
                carry_ref[...] = carry_w_ref[old]

                def body(state):
                    kb, _ = state
                    block(stack_heads(t), kb)
                    go = (jnp.max(carry_ref[...]) >= SB_STOP).astype(jnp.int32)
                    return kb - 1, go

                lax.while_loop(cond, body, (t - 2, jnp.int32(1)))
                write_out(t, acc_ref[...])
        return 0

    lax.fori_loop(0, (last + 1 + 2) // 2, two_steps, 0)


def _sb_tri():
    j = np.arange(SB_T)[:, None]
    s = np.arange(SB_T)[None, :]
    return jnp.asarray(np.where(j > s, -1.0, 0.0).astype(np.float32), BF16)


def _sb_attention(proj):
    pair = lambda off: pl.BlockSpec((SEQ, LANES), lambda p: (0, off // LANES + p))
    return pl.pallas_call(
        _sb_kernel,
        grid=(SB_WIDTH // LANES,),
        in_specs=[pair(OFF_Q_SB), pair(OFF_K_SB), pair(OFF_V_SB),
                  pl.BlockSpec((SB_T, SB_T), lambda p: (0, 0))],
        out_specs=pl.BlockSpec((SEQ, LANES), lambda p: (0, p)),
        out_shape=jax.ShapeDtypeStruct((SEQ, SB_WIDTH), BF16),
        scratch_shapes=[
            pltpu.VMEM((2, 2, 2 * SB_T, SB_T), BF16),
            pltpu.VMEM((2, 2 * SB_T, SB_H), F32),
            pltpu.VMEM((2, SB_T, SB_H), F32),
            pltpu.VMEM((2, 2 * SB_T, SB_T), F32),
            pltpu.VMEM((2, 2, 2 * SB_T, 1), F32),
            pltpu.VMEM((2, 2, 2 * SB_T, SB_T), BF16),
            pltpu.VMEM((2, 2 * SB_T, 1), F32),
            pltpu.SMEM((1,), jnp.int32),
            pltpu.VMEM((2 * SB_T, LANES), F32),
            pltpu.VMEM((2 * SB_T, 1), F32),
        ],
        compiler_params=pltpu.CompilerParams(
            dimension_semantics=("arbitrary",), vmem_limit_bytes=VMEM_LIMIT),
        name="sb_attention",
    )(proj, proj, proj, _sb_tri())


def _bucket_table():
    qi = np.arange(BLOCK)[:, None]
    ci = np.arange(2 * BLOCK)[None, :]
    dist = qi + BLOCK - ci
    n = np.maximum(dist, 0)
    max_exact = N_BUCKETS // 2
    nf = np.maximum(n, 1).astype(np.float64)
    val = np.log(nf / max_exact) / math.log(MAX_DISTANCE / max_exact) * (N_BUCKETS - max_exact)
    in_window = (dist >= 0) & (dist < WINDOW)
    frac = val - np.floor(val)
    chk = in_window & (n > max_exact)
    assert np.all((frac[chk] > 1e-3) & (frac[chk] < 1 - 1e-3))
    large = np.minimum(max_exact + val.astype(np.int64), N_BUCKETS - 1)
    bucket = np.where(n < max_exact, n, large)
    later = np.where(in_window, bucket, -1)
    assert all(np.array_equal(later[r], np.roll(later[0], r)) for r in range(BLOCK))
    return jnp.asarray(np.broadcast_to(later[0:1], (8, 2 * BLOCK)).astype(np.int32))


def _bias_kernel(rb_ref, bucket_ref, o_ref):
    b = bucket_ref[...]
    hits = [b == k for k in range(N_BUCKETS)]
    col = lax.broadcasted_iota(jnp.int32, (BLOCK, 2 * BLOCK), 1)
    for h in range(SW_HEADS):
        row0 = jnp.full(b.shape, NEG_BIG, F32)
        for k in range(N_BUCKETS):
            row0 = jnp.where(hits[k], rb_ref[k, h], row0)
        tile = pltpu.roll(jnp.concatenate([row0] * (BLOCK // 8), axis=0), 0, 1,
                          stride=1, stride_axis=0)
        o_ref[1, h] = tile
        o_ref[0, h] = jnp.where(col >= BLOCK, tile, NEG_BIG)


def _bias_table(rel_bias):
    return pl.pallas_call(
        _bias_kernel,
        in_specs=[
            pl.BlockSpec(memory_space=pltpu.SMEM),
            pl.BlockSpec(memory_space=pltpu.VMEM),
        ],
        out_specs=pl.BlockSpec(memory_space=pltpu.VMEM),
        out_shape=jax.ShapeDtypeStruct((2, SW_HEADS, BLOCK, 2 * BLOCK), F32),
        name="t5_bias",
    )(rel_bias.astype(F32), _bucket_table())


def _sw_kernel(sink_ref, qa_ref, qb_ref, k0_ref, k1_ref, k2_ref, v0_ref, v1_ref, v2_ref, bias_ref,
               o_ref, p_ref, st_ref):
    g = pl.program_id(0)

    @pl.when(g == 0)
    def _():
        p_ref[...] = jnp.zeros_like(p_ref)
        st_ref[...] = jnp.zeros_like(st_ref)

    lane = lax.broadcasted_iota(jnp.int32, (BLOCK, LANES), 1)
    ones = jnp.ones((2 * BLOCK, LANES), BF16)
    scale = jnp.asarray(SCALE, BF16)
    pairs_per_kv = SW_HEADS // SW_KV_HEADS // 2
    dims = (((1,), (1,)), ((), ()))

    def swap_halves(t):
        return jnp.concatenate([t[:, HEAD_DIM:], t[:, :HEAD_DIM]], axis=1)

    def on_side(kv):
        return (lane >= HEAD_DIM) if kv % 2 else (lane < HEAD_DIM)

    def kv_cols(kv):
        return slice((kv // 2) * LANES, (kv // 2 + 1) * LANES)

    def stages(cur, q_ref, kp_ref, kc_ref, vp_ref, vc_ref, out_rows):
        old = 1 - cur
        k = jnp.concatenate([kp_ref[...], kc_ref[...]], axis=0)
        v = jnp.concatenate([vp_ref[...], vc_ref[...]], axis=0)
        logits, heads = [], []
        for kv in range(SW_KV_HEADS):
            side = kv % 2
            same, other, hs, ho = [], [], [], []
            for pp in range(pairs_per_kv):
                p = kv * pairs_per_kv + pp
                q_pair = q_ref[:, p * LANES:(p + 1) * LANES] * scale
                zero = jnp.zeros_like(q_pair)
                same.append(jnp.where(on_side(kv), q_pair, zero))
                other.append(jnp.where(on_side(kv), swap_halves(q_pair), zero))
                hs.append(2 * p + side)
                ho.append(2 * p + 1 - side)
            lhs = jnp.concatenate(same + other, axis=0)
            logits.append(lax.dot_general(lhs, k[:, kv_cols(kv)], dims,
                                          preferred_element_type=F32))
            heads.append(hs + ho)
        results = []
        for kv in range(SW_KV_HEADS):
            v_pair = v[:, kv_cols(kv)]
            half = pairs_per_kv * BLOCK
            r_same = jnp.dot(p_ref[old, kv, :half, :], jnp.concatenate([v_pair, ones], axis=1),
                             preferred_element_type=F32)
            r_other = jnp.dot(p_ref[old, kv, half:, :],
                              jnp.concatenate([swap_halves(v_pair), ones], axis=1),
                              preferred_element_type=F32)
            results.append((r_same, r_other))

        for kv in range(SW_KV_HEADS):
            maxes = []
            for i, h in enumerate(heads[kv]):
                lg = logits[kv][i * BLOCK:(i + 1) * BLOCK, :] + bias_ref[0, h]
                m = jnp.maximum(jnp.max(lg, axis=-1, keepdims=True), sink_ref[h])
                p_ref[cur, kv, i * BLOCK:(i + 1) * BLOCK, :] = jnp.exp(lg - m).astype(BF16)
                maxes.append(m)
            for pp in range(pairs_per_kv):
                h_same, h_other = heads[kv][pp], heads[kv][pairs_per_kv + pp]
                m = jnp.where(on_side(kv), maxes[pp], maxes[pairs_per_kv + pp])
                sink = jnp.where(on_side(kv), sink_ref[h_same], sink_ref[h_other])
                st_ref[cur, kv * pairs_per_kv + pp] = jnp.exp(sink - m)

        for kv in range(SW_KV_HEADS):
            r_same, r_other = results[kv]
            for pp in range(pairs_per_kv):
                p = kv * pairs_per_kv + pp
                rows = slice(pp * BLOCK, (pp + 1) * BLOCK)
                num = jnp.where(on_side(kv), r_same[rows, :LANES], r_other[rows, :LANES])
                den = jnp.where(on_side(kv), r_same[rows, LANES:], r_other[rows, LANES:])
                o_ref[out_rows, p * LANES:(p + 1) * LANES] = (
                    num / (den + st_ref[old, p])).astype(BF16)

    @pl.when(g >= 0)
    def _():
        stages(1, qa_ref, k0_ref, k1_ref, v0_ref, v1_ref, slice(0, BLOCK))

    @pl.when(g >= 0)
    def _():
        stages(0, qb_ref, k1_ref, k2_ref, v1_ref, v2_ref, slice(BLOCK, 2 * BLOCK))


def _sw_attention(proj, sinks, bias):
    kq = OFF_Q_SW // SW_WIDTH
    kk = OFF_K_SW // SW_KV_WIDTH
    kv = OFF_V_SW // SW_KV_WIDTH
    last = SEQ // BLOCK - 1
    blk = lambda n: jnp.clip(n, 0, last)
    q_spec = lambda off: pl.BlockSpec((BLOCK, SW_WIDTH), lambda g: (blk(2 * g + off), kq))
    k_spec = lambda off: pl.BlockSpec((BLOCK, SW_KV_WIDTH), lambda g: (blk(2 * g + off), kk))
    v_spec = lambda off: pl.BlockSpec((BLOCK, SW_KV_WIDTH), lambda g: (blk(2 * g + off), kv))
    return pl.pallas_call(
        _sw_kernel,
        grid=((last + 1) // 2 + 1,),
        in_specs=[
            pl.BlockSpec(memory_space=pltpu.SMEM),
            q_spec(-1), q_spec(0),
            k_spec(-2), k_spec(-1), k_spec(0),
            v_spec(-3), v_spec(-2), v_spec(-1),
            pl.BlockSpec((1, SW_HEADS, BLOCK, 2 * BLOCK), lambda g: (jnp.minimum(g, 1), 0, 0, 0)),
        ],
        out_specs=pl.BlockSpec((2 * BLOCK, SW_WIDTH), lambda g: (jnp.maximum(g - 1, 0), 0)),
        out_shape=jax.ShapeDtypeStruct((SEQ, SW_WIDTH), BF16),
        scratch_shapes=[
            pltpu.VMEM((2, SW_KV_HEADS, SW_HEADS // SW_KV_HEADS * BLOCK, 2 * BLOCK), BF16),
            pltpu.VMEM((2, SW_HEADS // 2, BLOCK, LANES), F32),
        ],
        compiler_params=pltpu.CompilerParams(
            dimension_semantics=("arbitrary",), vmem_limit_bytes=VMEM_LIMIT),
        name="sw_attention",
    )(sinks.astype(F32), proj, proj, proj, proj, proj, proj, proj, proj, bias)


def _rms(x, g):
    ms = jnp.mean(x * x, axis=-1, keepdims=True)
    return (x * lax.rsqrt(ms + RMS_EPS)) * g


def _silu(z):
    return z * (1.0 / (1.0 + jnp.exp(-z)))


def _out_kernel(osb_ref, osw_ref, zsb_ref, zswa_ref, zswb_ref, gsb_ref, gsw_ref, gpost_ref,
                w_hbm, x_ref, o_ref, w_ref, stage_ref, sem):
    @pl.when(pl.program_id(0) == 0)
    def _():
        _cast_weights(w_hbm, w_ref, stage_ref, sem)

    chunk = OUT_TM // OUT_CHUNKS
    ys = []
    for c in range(OUT_CHUNKS):
        rows = pl.ds(c * chunk, chunk)
        a_sb = _rms(osb_ref[rows, :].astype(F32), gsb_ref[...]) * zsb_ref[rows, :].astype(F32)
        g_sw = jnp.concatenate([zswa_ref[rows, :], zswb_ref[rows, :]], axis=1).astype(F32)
        a_sw = _rms(osw_ref[rows, :].astype(F32), gsw_ref[...]) * g_sw
        y = jnp.dot(a_sb.astype(BF16), w_ref[:SB_WIDTH, :], preferred_element_type=F32)
        ys.append(y + jnp.dot(a_sw.astype(BF16), w_ref[SB_WIDTH:, :],
                              preferred_element_type=F32))
    for c in range(OUT_CHUNKS):
        rows = pl.ds(c * chunk, chunk)
        o_ref[rows, :] = x_ref[rows, :] + _rms(ys[c], gpost_ref[...])


def _outproj(o_sb, o_sw, proj, gn_sb, gn_sw, norm_post, w_out, x2):
    half = SW_WIDTH // 2
    return pl.pallas_call(
        _out_kernel,
        grid=(SEQ // OUT_TM,),
        in_specs=[
            pl.BlockSpec((OUT_TM, SB_WIDTH), lambda i: (i, 0)),
            pl.BlockSpec((OUT_TM, SW_WIDTH), lambda i: (i, 0)),
            pl.BlockSpec((OUT_TM, SB_WIDTH), lambda i: (i, OFF_Z_SB // SB_WIDTH)),
            pl.BlockSpec((OUT_TM, half), lambda i: (i, OFF_Z_SW // half)),
            pl.BlockSpec((OUT_TM, half), lambda i: (i, OFF_Z_SW // half + 1)),
            pl.BlockSpec((1, SB_WIDTH), lambda i: (0, 0)),
            pl.BlockSpec((1, SW_WIDTH), lambda i: (0, 0)),
            pl.BlockSpec((1, D_MODEL), lambda i: (0, 0)),
            pl.BlockSpec(memory_space=pl.ANY),
            pl.BlockSpec((OUT_TM, D_MODEL), lambda i: (i, 0)),
        ],
        out_specs=pl.BlockSpec((OUT_TM, D_MODEL), lambda i: (i, 0)),
        out_shape=jax.ShapeDtypeStruct((SEQ, D_MODEL), F32),
        scratch_shapes=[
            pltpu.VMEM((D_MIX, D_MODEL), BF16),
            pltpu.VMEM((CAST_SLOTS, OUT_CAST_ROWS, D_MODEL), F32),
            pltpu.SemaphoreType.DMA((CAST_SLOTS,)),
        ],
        compiler_params=pltpu.CompilerParams(
            dimension_semantics=("arbitrary",), vmem_limit_bytes=VMEM_LIMIT),
        name="outproj",
    )(o_sb, o_sw, proj, proj, proj, gn_sb, gn_sw, norm_post, w_out, x2)


def kernel(x, w_in, w_out, norm_pre, norm_post, gn_sb, gn_sw, sinks, rel_bias):
    assert x.shape == (1, SEQ, D_MODEL) and w_in.shape == (1, D_MODEL, D_IN_PROJ)
    assert OFF_Z_SW % (SW_WIDTH // 2) == 0 and OFF_Q_SW % SW_WIDTH == 0
    x2 = x.reshape(SEQ, D_MODEL)
    proj = _inproj(x2, norm_pre.astype(F32), w_in[0].astype(F32))
    o_sb = _sb_attention(proj)
    bias = _bias_table(rel_bias)
    o_sw = _sw_attention(proj, sinks[0], bias)
    out = _outproj(o_sb, o_sw, proj, gn_sb.astype(F32), gn_sw.astype(F32), norm_post.astype(F32),
                   w_out[0].astype(F32), x2)
    return out.reshape(1, SEQ, D_MODEL)
```

```python
import math

import numpy as np
import jax
import jax.numpy as jnp
from jax import lax
from jax.experimental import pallas as pl
from jax.experimental.pallas import tpu as pltpu

D_MODEL = 2048
SEQ = 16384
HEAD_DIM = 64
SB_HEADS = 16
SW_HEADS = 16
SW_KV_HEADS = 4
SB_WIDTH = SB_HEADS * HEAD_DIM
SW_WIDTH = SW_HEADS * HEAD_DIM
SW_KV_WIDTH = SW_KV_HEADS * HEAD_DIM
D_MIX = SB_WIDTH + SW_WIDTH
D_IN_PROJ = 4 * SB_WIDTH + 2 * SW_WIDTH + 2 * SW_KV_WIDTH
WINDOW = 128
BLOCK = 128
N_BUCKETS = 32
MAX_DISTANCE = 128
RMS_EPS = 1e-6
SCALE = HEAD_DIM ** -0.5

OFF_Q_SB = 0
OFF_K_SB = SB_WIDTH
OFF_V_SB = 2 * SB_WIDTH
OFF_Z_SB = 3 * SB_WIDTH
OFF_Q_SW = 4 * SB_WIDTH
OFF_K_SW = OFF_Q_SW + SW_WIDTH
OFF_V_SW = OFF_K_SW + SW_KV_WIDTH
OFF_Z_SW = OFF_V_SW + SW_KV_WIDTH

LANES = 128
VMEM_LIMIT = 56 * 1024 * 1024

PROJ_TM = 512
PROJ_TN = 3328
CAST_ROWS = 16
OUT_CAST_ROWS = 32
CAST_SLOTS = 8
SB_T = 256
SB_H = SB_T // 2
OUT_TM = 512
OUT_CHUNKS = 2
NEG_BIG = -1e30
SB_STOP = -110.0

F32 = jnp.float32
BF16 = jnp.bfloat16


def _cast_weights(w_hbm, w_ref, stage_ref, sem):
    slots, rows = stage_ref.shape[0], stage_ref.shape[1]
    n_chunks = w_hbm.shape[0] // rows
    ahead = slots - 1

    def chunk_copy(c):
        slot = c % slots
        src = w_hbm.at[pl.ds(pl.multiple_of(c * rows, rows), rows), :]
        return pltpu.make_async_copy(src, stage_ref.at[slot], sem.at[slot])

    for c in range(ahead):
        chunk_copy(c).start()

    def body(c, _):
        @pl.when(c + ahead < n_chunks)
        def _():
            chunk_copy(c + ahead).start()

        chunk_copy(c).wait()
        w_ref[pl.ds(pl.multiple_of(c * rows, rows), rows), :] = stage_ref[c % slots].astype(BF16)
        return 0

    lax.fori_loop(0, n_chunks, body, 0)


def _inproj_kernel(x_ref, g_ref, w_hbm, o_ref, w_ref, stage_ref, sem):
    @pl.when(pl.program_id(0) == 0)
    def _():
        _cast_weights(w_hbm, w_ref, stage_ref, sem)

    x = x_ref[...]
    ms = jnp.mean(x * x, axis=-1, keepdims=True)
    xn = ((x * lax.rsqrt(ms + RMS_EPS)) * g_ref[...]).astype(BF16)
    gates = ((OFF_Z_SB, OFF_Z_SB + SB_WIDTH), (OFF_Z_SW, OFF_Z_SW + SW_WIDTH))
    for j in reversed(range(D_IN_PROJ // PROJ_TN)):
        lo, hi = j * PROJ_TN, (j + 1) * PROJ_TN
        res = jnp.dot(xn, w_ref[:, lo:hi], preferred_element_type=F32)
        cuts = sorted({lo, hi} | {min(max(c, lo), hi) for g in gates for c in g})
        for a, b in zip(cuts[:-1], cuts[1:]):
            part = res[:, a - lo:b - lo]
            if any(g0 <= a and b <= g1 for g0, g1 in gates):
                part = _silu(part)
            o_ref[:, a:b] = part.astype(BF16)


def _inproj(x2, g, w):
    return pl.pallas_call(
        _inproj_kernel,
        grid=(SEQ // PROJ_TM,),
        in_specs=[
            pl.BlockSpec((PROJ_TM, D_MODEL), lambda i: (i, 0)),
            pl.BlockSpec((1, D_MODEL), lambda i: (0, 0)),
            pl.BlockSpec(memory_space=pl.ANY),
        ],
        out_specs=pl.BlockSpec((PROJ_TM, D_IN_PROJ), lambda i: (i, 0)),
        out_shape=jax.ShapeDtypeStruct((SEQ, D_IN_PROJ), BF16),
        scratch_shapes=[
            pltpu.VMEM((D_MODEL, D_IN_PROJ), BF16),
            pltpu.VMEM((CAST_SLOTS, CAST_ROWS, D_IN_PROJ), F32),
            pltpu.SemaphoreType.DMA((CAST_SLOTS,)),
        ],
        compiler_params=pltpu.CompilerParams(
            dimension_semantics=("arbitrary",), vmem_limit_bytes=VMEM_LIMIT),
        name="inproj",
    )(x2, g, w)


def _sb_kernel(q_ref, k_ref, v_ref, tri_ref, o_ref,
               hi_ref, lbl_ref, lbr_ref, lbp_ref, sp0_ref, w_ref, carry_w_ref, go_ref,
               acc_ref, carry_ref):
    last = SEQ // SB_T - 1
    for ref in (hi_ref, lbl_ref, lbr_ref, lbp_ref, sp0_ref, w_ref, carry_w_ref):
        ref[...] = jnp.zeros_like(ref)

    lane = lax.broadcasted_iota(jnp.int32, (SB_T, LANES), 1)

    def stack_heads(tile):
        start = pl.multiple_of(tile * SB_T, SB_T)
        q = q_ref[pl.ds(start, SB_T), :] * jnp.asarray(SCALE, BF16)
        zero = jnp.zeros_like(q)
        return jnp.concatenate([jnp.where(lane < HEAD_DIM, q, zero),
                                jnp.where(lane >= HEAD_DIM, q, zero)], axis=0)

    tri = tri_ref[...]
    r_h = lax.broadcasted_iota(jnp.int32, (SB_H, SB_H), 0)
    c_h = lax.broadcasted_iota(jnp.int32, (SB_H, SB_H), 1)
    cut_add = jnp.where(c_h < r_h, 0.0, NEG_BIG).astype(F32)
    zeros_h = jnp.zeros((SB_H, SB_H), BF16)

    def qk(qs, kb):
        start = pl.multiple_of(kb * SB_T, SB_T)
        kblk = k_ref[pl.ds(start, SB_T), :]
        return lax.dot_general(qs, kblk, (((1,), (1,)), ((), ())), preferred_element_type=F32)

    def softplus_parts(z):
        neg_abs = lax.bitcast_convert_type(
            lax.bitcast_convert_type(z, jnp.int32) | jnp.int32(-2 ** 31), F32)
        sp = jnp.maximum(z, 0.0) + jnp.log(1.0 + jnp.exp(neg_abs))
        return sp.astype(BF16), z - sp, sp[:, 0:1]

    def cut(z):
        return z + cut_add

    def col_max(c):
        return jnp.max(jnp.max(c.reshape(c.shape[0] // 8, 8, 1), axis=0))

    def quarters(a):
        left = a[:, :SB_H]
        right = jnp.concatenate([a[SB_H:SB_T, SB_H:], a[SB_T + SB_H:, SB_H:]], axis=0)
        return left, right

    def unquarter(left, right):
        right_full = jnp.concatenate([zeros_h, right[:SB_H], zeros_h, right[SB_H:]], axis=0)
        return jnp.concatenate([left, right_full], axis=1)

    def cumsum(hi):
        return jnp.dot(hi, tri, preferred_element_type=F32)

    def values(kb, w, v_valid=None):
        start = pl.multiple_of(kb * SB_T, SB_T)
        vblk = v_ref[pl.ds(start, SB_T), :]
        if v_valid is not None:
            vblk = jnp.where(v_valid, vblk, jnp.zeros_like(vblk))
        return jnp.dot(w, vblk, preferred_element_type=F32)

    def near_values(old, t):
        return (values(t, w_ref[old, 0])
                + values(jnp.maximum(t - 1, 0), w_ref[old, 1], v_valid=t > 0))

    def write_out(t, acc):
        rows = pl.ds(pl.multiple_of(t * SB_T, SB_T), SB_T)
        o_ref[rows, :] = jnp.where(lane < HEAD_DIM, acc[:SB_T], acc[SB_T:]).astype(BF16)

    def stages(cur, n, t):
        old = 1 - cur
        qs_n = stack_heads(n)
        z_d = qk(qs_n, n)
        z_p = qk(qs_n, jnp.maximum(n - 1, 0))
        cs_d = cumsum(hi_ref[old, 0])
        cs_p = cumsum(hi_ref[old, 1])
        write_out(t, near_values(old, t))
        go_ref[t] = (col_max(carry_w_ref[old]) >= SB_STOP).astype(jnp.int32)

        zl, zr = quarters(z_d)
        zl = jnp.concatenate([cut(zl[:SB_H]), zl[SB_H:SB_T],
                              cut(zl[SB_T:SB_T + SB_H]), zl[SB_T + SB_H:]], axis=0)
        zr = jnp.concatenate([cut(zr[:SB_H]), cut(zr[SB_H:])], axis=0)
        hi_l, lbl_ref[cur], sp0_ref[cur, 0] = softplus_parts(zl)
        hi_r, lbr_ref[cur], _ = softplus_parts(zr)
        hi_ref[cur, 0] = unquarter(hi_l, hi_r)
        hi_ref[cur, 1], lbp_ref[cur], sp0_ref[cur, 1] = softplus_parts(z_p)

        cs_l, cs_r = quarters(cs_d)
        w_l = jnp.exp(lbl_ref[old] + cs_l).astype(BF16)
        w_r = jnp.exp(lbr_ref[old] + cs_r).astype(BF16)
        carry_d = cs_d[:, 0:1] - sp0_ref[old, 0]
        w_ref[cur, 0] = unquarter(w_l, w_r)
        w_ref[cur, 1] = jnp.exp(lbp_ref[old] + cs_p + carry_d).astype(BF16)
        carry_w_ref[cur] = carry_d + (cs_p[:, 0:1] - sp0_ref[old, 1])

    def block(qs, kb, diagonal=False):
        z = qk(qs, kb)
        if diagonal:
            q_pos = lax.broadcasted_iota(jnp.int32, z.shape, 0) & (SB_T - 1)
            k_pos = lax.broadcasted_iota(jnp.int32, z.shape, 1)
            z = z + jnp.where(k_pos < q_pos, 0.0, NEG_BIG).astype(F32)
        hi, log_beta, sp0 = softplus_parts(z)
        cs = cumsum(hi)
        carry = carry_ref[...]
        w = jnp.exp(log_beta + cs + carry).astype(BF16)
        acc_ref[...] += values(kb, w)
        carry_ref[...] = carry + (cs[:, 0:1] - sp0)

    def cond(state):
        kb, go = state
        return jnp.logical_and(kb >= 0, go > 0)

    def two_steps(i, _):
        for half in range(2):
            s = 2 * i + half
            stages(half, jnp.minimum(s, last), jnp.maximum(s - 2, 0))
        return 0

    lax.fori_loop(0, (last + 1 + 2) // 2, two_steps, 0)
    flagged = lax.fori_loop(2, last + 1, lambda t, f: f | go_ref[t], jnp.int32(0))

    def redo(t, _):
        @pl.when(go_ref[t] > 0)
        def _():
            acc_ref[...] = jnp.zeros_like(acc_ref)
            carry_ref[...] = jnp.zeros_like(carry_ref)
            block(stack_heads(t), t, diagonal=True)

            def body(state):
                kb, _ = state
                block(stack_heads(t), kb)
                go = (jnp.max(carry_ref[...]) >= SB_STOP).astype(jnp.int32)
                return kb - 1, go

            lax.while_loop(cond, body, (t - 1, jnp.int32(1)))
            write_out(t, acc_ref[...])
        return 0

    @pl.when(flagged > 0)
    def _():
        lax.fori_loop(2, last + 1, redo, 0)


def _sb_tri():
    j = np.arange(SB_T)[:, None]
    s = np.arange(SB_T)[None, :]
    return jnp.asarray(np.where(j > s, -1.0, 0.0).astype(np.float32), BF16)


def _sb_attention(proj):
    pair = lambda off: pl.BlockSpec((SEQ, LANES), lambda p: (0, off // LANES + p))
    return pl.pallas_call(
        _sb_kernel,
        grid=(SB_WIDTH // LANES,),
        in_specs=[pair(OFF_Q_SB), pair(OFF_K_SB), pair(OFF_V_SB),
                  pl.BlockSpec((SB_T, SB_T), lambda p: (0, 0))],
        out_specs=pl.BlockSpec((SEQ, LANES), lambda p: (0, p)),
        out_shape=jax.ShapeDtypeStruct((SEQ, SB_WIDTH), BF16),
        scratch_shapes=[
            pltpu.VMEM((2, 2, 2 * SB_T, SB_T), BF16),
            pltpu.VMEM((2, 2 * SB_T, SB_H), F32),
            pltpu.VMEM((2, SB_T, SB_H), F32),
            pltpu.VMEM((2, 2 * SB_T, SB_T), F32),
            pltpu.VMEM((2, 2, 2 * SB_T, 1), F32),
            pltpu.VMEM((2, 2, 2 * SB_T, SB_T), BF16),
            pltpu.VMEM((2, 2 * SB_T, 1), F32),
            pltpu.SMEM((SEQ // SB_T,), jnp.int32),
            pltpu.VMEM((2 * SB_T, LANES), F32),
            pltpu.VMEM((2 * SB_T, 1), F32),
        ],
        compiler_params=pltpu.CompilerParams(
            dimension_semantics=("arbitrary",), vmem_limit_bytes=VMEM_LIMIT),
        name="sb_attention",
    )(proj, proj, proj, _sb_tri())


def _bucket_table():
    qi = np.arange(BLOCK)[:, None]
    ci = np.arange(2 * BLOCK)[None, :]
    dist = qi + BLOCK - ci
    n = np.maximum(dist, 0)
    max_exact = N_BUCKETS // 2
    nf = np.maximum(n, 1).astype(np.float64)
    val = np.log(nf / max_exact) / math.log(MAX_DISTANCE / max_exact) * (N_BUCKETS - max_exact)
    in_window = (dist >= 0) & (dist < WINDOW)
    frac = val - np.floor(val)
    chk = in_window & (n > max_exact)
    assert np.all((frac[chk] > 1e-3) & (frac[chk] < 1 - 1e-3))
    large = np.minimum(max_exact + val.astype(np.int64), N_BUCKETS - 1)
    bucket = np.where(n < max_exact, n, large)
    later = np.where(in_window, bucket, -1)
    assert all(np.array_equal(later[r], np.roll(later[0], r)) for r in range(BLOCK))
    return jnp.asarray(np.broadcast_to(later[0:1], (8, 2 * BLOCK)).astype(np.int32))


def _bias_kernel(rb_ref, bucket_ref, o_ref):
    b = bucket_ref[...]
    hits = [b == k for k in range(N_BUCKETS)]
    col = lax.broadcasted_iota(jnp.int32, (BLOCK, 2 * BLOCK), 1)
    for h in range(SW_HEADS):
        row0 = jnp.full(b.shape, NEG_BIG, F32)
        for k in range(N_BUCKETS):
            row0 = jnp.where(hits[k], rb_ref[k, h], row0)
        tile = pltpu.roll(jnp.concatenate([row0] * (BLOCK // 8), axis=0), 0, 1,
                          stride=1, stride_axis=0)
        o_ref[1, h] = tile
        o_ref[0, h] = jnp.where(col >= BLOCK, tile, NEG_BIG)


def _bias_table(rel_bias):
    return pl.pallas_call(
        _bias_kernel,
        in_specs=[
            pl.BlockSpec(memory_space=pltpu.SMEM),
            pl.BlockSpec(memory_space=pltpu.VMEM),
        ],
        out_specs=pl.BlockSpec(memory_space=pltpu.VMEM),
        out_shape=jax.ShapeDtypeStruct((2, SW_HEADS, BLOCK, 2 * BLOCK), F32),
        name="t5_bias",
    )(rel_bias.astype(F32), _bucket_table())


def _sw_kernel(sink_ref, qa_ref, qb_ref, k0_ref, k1_ref, k2_ref, v0_ref, v1_ref, v2_ref, bias_ref,
               o_ref, p_ref, st_ref):
    g = pl.program_id(0)

    @pl.when(g == 0)
    def _():
        p_ref[...] = jnp.zeros_like(p_ref)
        st_ref[...] = jnp.zeros_like(st_ref)

    lane = lax.broadcasted_iota(jnp.int32, (BLOCK, LANES), 1)
    ones = jnp.ones((2 * BLOCK, LANES), BF16)
    scale = jnp.asarray(SCALE, BF16)
    pairs_per_kv = SW_HEADS // SW_KV_HEADS // 2
    dims = (((1,), (1,)), ((), ()))

    def swap_halves(t):
        return jnp.concatenate([t[:, HEAD_DIM:], t[:, :HEAD_DIM]], axis=1)

    def on_side(kv):
        return (lane >= HEAD_DIM) if kv % 2 else (lane < HEAD_DIM)

    def kv_cols(kv):
        return slice((kv // 2) * LANES, (kv // 2 + 1) * LANES)

    def stages(cur, q_ref, kp_ref, kc_ref, vp_ref, vc_ref, out_rows):
        old = 1 - cur
        k = jnp.concatenate([kp_ref[...], kc_ref[...]], axis=0)
        v = jnp.concatenate([vp_ref[...], vc_ref[...]], axis=0)
        logits, heads = [], []
        for kv in range(SW_KV_HEADS):
            side = kv % 2
            same, other, hs, ho = [], [], [], []
            for pp in range(pairs_per_kv):
                p = kv * pairs_per_kv + pp
                q_pair = q_ref[:, p * LANES:(p + 1) * LANES] * scale
                zero = jnp.zeros_like(q_pair)
                same.append(jnp.where(on_side(kv), q_pair, zero))
                other.append(jnp.where(on_side(kv), swap_halves(q_pair), zero))
                hs.append(2 * p + side)
                ho.append(2 * p + 1 - side)
            lhs = jnp.concatenate(same + other, axis=0)
            logits.append(lax.dot_general(lhs, k[:, kv_cols(kv)], dims,
                                          preferred_element_type=F32))
            heads.append(hs + ho)
        results = []
        for kv in range(SW_KV_HEADS):
            v_pair = v[:, kv_cols(kv)]
            half = pairs_per_kv * BLOCK
            r_same = jnp.dot(p_ref[old, kv, :half, :], jnp.concatenate([v_pair, ones], axis=1),
                             preferred_element_type=F32)
            r_other = jnp.dot(p_ref[old, kv, half:, :],
                              jnp.concatenate([swap_halves(v_pair), ones], axis=1),
                              preferred_element_type=F32)
            results.append((r_same, r_other))

        for kv in range(SW_KV_HEADS):
            maxes = []
            for i, h in enumerate(heads[kv]):
                lg = logits[kv][i * BLOCK:(i + 1) * BLOCK, :] + bias_ref[0, h]
                m = jnp.maximum(jnp.max(lg, axis=-1, keepdims=True), sink_ref[h])
                p_ref[cur, kv, i * BLOCK:(i + 1) * BLOCK, :] = jnp.exp(lg - m).astype(BF16)
                maxes.append(m)
            for pp in range(pairs_per_kv):
                h_same, h_other = heads[kv][pp], heads[kv][pairs_per_kv + pp]
                m = jnp.where(on_side(kv), maxes[pp], maxes[pairs_per_kv + pp])
                sink = jnp.where(on_side(kv), sink_ref[h_same], sink_ref[h_other])
                st_ref[cur, kv * pairs_per_kv + pp] = jnp.exp(sink - m)

        for kv in range(SW_KV_HEADS):
            r_same, r_other = results[kv]
            for pp in range(pairs_per_kv):
                p = kv * pairs_per_kv + pp
                rows = slice(pp * BLOCK, (pp + 1) * BLOCK)
                num = jnp.where(on_side(kv), r_same[rows, :LANES], r_other[rows, :LANES])
                den = jnp.where(on_side(kv), r_same[rows, LANES:], r_other[rows, LANES:])
                o_ref[out_rows, p * LANES:(p + 1) * LANES] = (
                    num / (den + st_ref[old, p])).astype(BF16)

    @pl.when(g >= 0)
    def _():
        stages(1, qa_ref, k0_ref, k1_ref, v0_ref, v1_ref, slice(0, BLOCK))

    @pl.when(g >= 0)
    def _():
        stages(0, qb_ref, k1_ref, k2_ref, v1_ref, v2_ref, slice(BLOCK, 2 * BLOCK))


def _sw_attention(proj, sinks, bias):
    kq = OFF_Q_SW // SW_WIDTH
    kk = OFF_K_SW // SW_KV_WIDTH
    kv = OFF_V_SW // SW_KV_WIDTH
    last = SEQ // BLOCK - 1
    blk = lambda n: jnp.clip(n, 0, last)
    q_spec = lambda off: pl.BlockSpec((BLOCK, SW_WIDTH), lambda g: (blk(2 * g + off), kq))
    k_spec = lambda off: pl.BlockSpec((BLOCK, SW_KV_WIDTH), lambda g: (blk(2 * g + off), kk))
    v_spec = lambda off: pl.BlockSpec((BLOCK, SW_KV_WIDTH), lambda g: (blk(2 * g + off), kv))
    return pl.pallas_call(
        _sw_kernel,
        grid=((last + 1) // 2 + 1,),
        in_specs=[
            pl.BlockSpec(memory_space=pltpu.SMEM),
            q_spec(-1), q_spec(0),
            k_spec(-2), k_spec(-1), k_spec(0),
            v_spec(-3), v_spec(-2), v_spec(-1),
            pl.BlockSpec((1, SW_HEADS, BLOCK, 2 * BLOCK), lambda g: (jnp.minimum(g, 1), 0, 0, 0)),
        ],
        out_specs=pl.BlockSpec((2 * BLOCK, SW_WIDTH), lambda g: (jnp.maximum(g - 1, 0), 0)),
        out_shape=jax.ShapeDtypeStruct((SEQ, SW_WIDTH), BF16),
        scratch_shapes=[
            pltpu.VMEM((2, SW_KV_HEADS, SW_HEADS // SW_KV_HEADS * BLOCK, 2 * BLOCK), BF16),
            pltpu.VMEM((2, SW_HEADS // 2, BLOCK, LANES), F32),
        ],
        compiler_params=pltpu.CompilerParams(
            dimension_semantics=("arbitrary",), vmem_limit_bytes=VMEM_LIMIT),
        name="sw_attention",
    )(sinks.astype(F32), proj, proj, proj, proj, proj, proj, proj, proj, bias)


def _rms(x, g):
    ms = jnp.mean(x * x, axis=-1, keepdims=True)
    return (x * lax.rsqrt(ms + RMS_EPS)) * g


def _silu(z):
    return z * (1.0 / (1.0 + jnp.exp(-z)))


def _out_kernel(osb_ref, osw_ref, zsb_ref, zswa_ref, zswb_ref, gsb_ref, gsw_ref, gpost_ref,
                w_hbm, x_ref, o_ref, w_ref, stage_ref, sem):
    @pl.when(pl.program_id(0) == 0)
    def _():
        _cast_weights(w_hbm, w_ref, stage_ref, sem)

    chunk = OUT_TM // OUT_CHUNKS
    ys = []
    for c in range(OUT_CHUNKS):
        rows = pl.ds(c * chunk, chunk)
        a_sb = _rms(osb_ref[rows, :].astype(F32), gsb_ref[...]) * zsb_ref[rows, :].astype(F32)
        g_sw = jnp.concatenate([zswa_ref[rows, :], zswb_ref[rows, :]], axis=1).astype(F32)
        a_sw = _rms(osw_ref[rows, :].astype(F32), gsw_ref[...]) * g_sw
        y = jnp.dot(a_sb.astype(BF16), w_ref[:SB_WIDTH, :], preferred_element_type=F32)
        ys.append(y + jnp.dot(a_sw.astype(BF16), w_ref[SB_WIDTH:, :],
                              preferred_element_type=F32))
    for c in range(OUT_CHUNKS):
        rows = pl.ds(c * chunk, chunk)
        o_ref[rows, :] = x_ref[rows, :] + _rms(ys[c], gpost_ref[...])


def _outproj(o_sb, o_sw, proj, gn_sb, gn_sw, norm_post, w_out, x2):
    half = SW_WIDTH // 2
    return pl.pallas_call(
        _out_kernel,
        grid=(SEQ // OUT_TM,),
        in_specs=[
            pl.BlockSpec((OUT_TM, SB_WIDTH), lambda i: (i, 0)),
            pl.BlockSpec((OUT_TM, SW_WIDTH), lambda i: (i, 0)),
            pl.BlockSpec((OUT_TM, SB_WIDTH), lambda i: (i, OFF_Z_SB // SB_WIDTH)),
            pl.BlockSpec((OUT_TM, half), lambda i: (i, OFF_Z_SW // half)),
            pl.BlockSpec((OUT_TM, half), lambda i: (i, OFF_Z_SW // half + 1)),
            pl.BlockSpec((1, SB_WIDTH), lambda i: (0, 0)),
            pl.BlockSpec((1, SW_WIDTH), lambda i: (0, 0)),
            pl.BlockSpec((1, D_MODEL), lambda i: (0, 0)),
            pl.BlockSpec(memory_space=pl.ANY),
            pl.BlockSpec((OUT_TM, D_MODEL), lambda i: (i, 0)),
        ],
        out_specs=pl.BlockSpec((OUT_TM, D_MODEL), lambda i: (i, 0)),
        out_shape=jax.ShapeDtypeStruct((SEQ, D_MODEL), F32),
        scratch_shapes=[
            pltpu.VMEM((D_MIX, D_MODEL), BF16),
            pltpu.VMEM((CAST_SLOTS, OUT_CAST_ROWS, D_MODEL), F32),
            pltpu.SemaphoreType.DMA((CAST_SLOTS,)),
        ],
        compiler_params=pltpu.CompilerParams(
            dimension_semantics=("arbitrary",), vmem_limit_bytes=VMEM_LIMIT),
        name="outproj",
    )(o_sb, o_sw, proj, proj, proj, gn_sb, gn_sw, norm_post, w_out, x2)


def kernel(x, w_in, w_out, norm_pre, norm_post, gn_sb, gn_sw, sinks, rel_bias):
    assert x.shape == (1, SEQ, D_MODEL) and w_in.shape == (1, D_MODEL, D_IN_PROJ)
    assert OFF_Z_SW % (SW_WIDTH // 2) == 0 and OFF_Q_SW % SW_WIDTH == 0
    x2 = x.reshape(SEQ, D_MODEL)
    proj = _inproj(x2, norm_pre.astype(F32), w_in[0].astype(F32))
    o_sb = _sb_attention(proj)
    bias = _bias_table(rel_bias)
    o_sw = _sw_attention(proj, sinks[0], bias)
    out = _outproj(o_sb, o_sw, proj, gn_sb.astype(F32), gn_sw.astype(F32), norm_post.astype(F32),
                   w_out[0].astype(F32), x2)
    return out.reshape(1, SEQ, D_MODEL)
```

```python
import math

import numpy as np
import jax
import jax.numpy as jnp
from jax import lax
from jax.experimental import pallas as pl
from jax.experimental.pallas import tpu as pltpu

D_MODEL = 2048
SEQ = 16384
HEAD_DIM = 64
SB_HEADS = 16
SW_HEADS = 16
SW_KV_HEADS = 4
SB_WIDTH = SB_HEADS * HEAD_DIM
SW_WIDTH = SW_HEADS * HEAD_DIM
SW_KV_WIDTH = SW_KV_HEADS * HEAD_DIM
D_MIX = SB_WIDTH + SW_WIDTH
D_IN_PROJ = 4 * SB_WIDTH + 2 * SW_WIDTH + 2 * SW_KV_WIDTH
WINDOW = 128
BLOCK = 128
N_BUCKETS = 32
MAX_DISTANCE = 128
RMS_EPS = 1e-6
SCALE = HEAD_DIM ** -0.5

OFF_Q_SB = 0
OFF_K_SB = SB_WIDTH
OFF_V_SB = 2 * SB_WIDTH
OFF_Z_SB = 3 * SB_WIDTH
OFF_Q_SW = 4 * SB_WIDTH
OFF_K_SW = OFF_Q_SW + SW_WIDTH
OFF_V_SW = OFF_K_SW + SW_KV_WIDTH
OFF_Z_SW = OFF_V_SW + SW_KV_WIDTH

LANES = 128
VMEM_LIMIT = 56 * 1024 * 1024

PROJ_TM = 512
PROJ_TN = 3328
CAST_ROWS = 16
OUT_CAST_ROWS = 32
CAST_SLOTS = 8
SB_T = 256
SB_H = SB_T // 2
OUT_TM = 512
OUT_CHUNKS = 2
NEG_BIG = -1e30
SB_UNROLL = 6
SB_STOP = -110.0

F32 = jnp.float32
BF16 = jnp.bfloat16


def _cast_weights(w_hbm, w_ref, stage_ref, sem):
    slots, rows = stage_ref.shape[0], stage_ref.shape[1]
    n_chunks = w_hbm.shape[0] // rows
    ahead = slots - 1

    def chunk_copy(c):
        slot = c % slots
        src = w_hbm.at[pl.ds(pl.multiple_of(c * rows, rows), rows), :]
        return pltpu.make_async_copy(src, stage_ref.at[slot], sem.at[slot])

    for c in range(ahead):
        chunk_copy(c).start()

    def body(c, _):
        @pl.when(c + ahead < n_chunks)
        def _():
            chunk_copy(c + ahead).start()

        chunk_copy(c).wait()
        w_ref[pl.ds(pl.multiple_of(c * rows, rows), rows), :] = stage_ref[c % slots].astype(BF16)
        return 0

    lax.fori_loop(0, n_chunks, body, 0)


def _inproj_kernel(x_ref, g_ref, w_hbm, o_ref, w_ref, stage_ref, sem):
    @pl.when(pl.program_id(0) == 0)
    def _():
        _cast_weights(w_hbm, w_ref, stage_ref, sem)

    x = x_ref[...]
    ms = jnp.mean(x * x, axis=-1, keepdims=True)
    xn = ((x * lax.rsqrt(ms + RMS_EPS)) * g_ref[...]).astype(BF16)
    gates = ((OFF_Z_SB, OFF_Z_SB + SB_WIDTH), (OFF_Z_SW, OFF_Z_SW + SW_WIDTH))
    for j in reversed(range(D_IN_PROJ // PROJ_TN)):
        lo, hi = j * PROJ_TN, (j + 1) * PROJ_TN
        res = jnp.dot(xn, w_ref[:, lo:hi], preferred_element_type=F32)
        cuts = sorted({lo, hi} | {min(max(c, lo), hi) for g in gates for c in g})
        for a, b in zip(cuts[:-1], cuts[1:]):
            part = res[:, a - lo:b - lo]
            if any(g0 <= a and b <= g1 for g0, g1 in gates):
                part = _silu(part)
            o_ref[:, a:b] = part.astype(BF16)


def _inproj(x2, g, w):
    return pl.pallas_call(
        _inproj_kernel,
        grid=(SEQ // PROJ_TM,),
        in_specs=[
            pl.BlockSpec((PROJ_TM, D_MODEL), lambda i: (i, 0)),
            pl.BlockSpec((1, D_MODEL), lambda i: (0, 0)),
            pl.BlockSpec(memory_space=pl.ANY),
        ],
        out_specs=pl.BlockSpec((PROJ_TM, D_IN_PROJ), lambda i: (i, 0)),
        out_shape=jax.ShapeDtypeStruct((SEQ, D_IN_PROJ), BF16),
        scratch_shapes=[
            pltpu.VMEM((D_MODEL, D_IN_PROJ), BF16),
            pltpu.VMEM((CAST_SLOTS, CAST_ROWS, D_IN_PROJ), F32),
            pltpu.SemaphoreType.DMA((CAST_SLOTS,)),
        ],
        compiler_params=pltpu.CompilerParams(
            dimension_semantics=("arbitrary",), vmem_limit_bytes=VMEM_LIMIT),
        name="inproj",
    )(x2, g, w)


def _sb_kernel(q_ref, k_ref, v_ref, tri_ref, o_ref,
               hi_ref, lbl_ref, lbr_ref, lbp_ref, sp0_ref, w_ref, carry_w_ref, go_ref,
               acc_ref, carry_ref):
    last = SEQ // SB_T - 1
    for ref in (hi_ref, lbl_ref, lbr_ref, lbp_ref, sp0_ref, w_ref, carry_w_ref):
        ref[...] = jnp.zeros_like(ref)

    lane = lax.broadcasted_iota(jnp.int32, (SB_T, LANES), 1)

    def stack_heads(tile):
        start = pl.multiple_of(tile * SB_T, SB_T)
        q = q_ref[pl.ds(start, SB_T), :] * jnp.asarray(SCALE, BF16)
        zero = jnp.zeros_like(q)
        return jnp.concatenate([jnp.where(lane < HEAD_DIM, q, zero),
                                jnp.where(lane >= HEAD_DIM, q, zero)], axis=0)

    tri = tri_ref[...]
    r_h = lax.broadcasted_iota(jnp.int32, (SB_H, SB_H), 0)
    c_h = lax.broadcasted_iota(jnp.int32, (SB_H, SB_H), 1)
    cut_add = jnp.where(c_h < r_h, 0.0, NEG_BIG).astype(F32)
    zeros_h = jnp.zeros((SB_H, SB_H), BF16)

    def qk(qs, kb):
        start = pl.multiple_of(kb * SB_T, SB_T)
        kblk = k_ref[pl.ds(start, SB_T), :]
        return lax.dot_general(qs, kblk, (((1,), (1,)), ((), ())), preferred_element_type=F32)

    def softplus_parts(z):
        neg_abs = lax.bitcast_convert_type(
            lax.bitcast_convert_type(z, jnp.int32) | jnp.int32(-2 ** 31), F32)
        sp = jnp.maximum(z, 0.0) + jnp.log(1.0 + jnp.exp(neg_abs))
        return sp.astype(BF16), z - sp, sp[:, 0:1]

    def cut(z):
        return z + cut_add

    def col_max(c):
        return jnp.max(jnp.max(c.reshape(c.shape[0] // 8, 8, 1), axis=0))

    def quarters(a):
        left = a[:, :SB_H]
        right = jnp.concatenate([a[SB_H:SB_T, SB_H:], a[SB_T + SB_H:, SB_H:]], axis=0)
        return left, right

    def unquarter(left, right):
        right_full = jnp.concatenate([zeros_h, right[:SB_H], zeros_h, right[SB_H:]], axis=0)
        return jnp.concatenate([left, right_full], axis=1)

    def cumsum(hi):
        return jnp.dot(hi, tri, preferred_element_type=F32)

    def values(kb, w, v_valid=None):
        start = pl.multiple_of(kb * SB_T, SB_T)
        vblk = v_ref[pl.ds(start, SB_T), :]
        if v_valid is not None:
            vblk = jnp.where(v_valid, vblk, jnp.zeros_like(vblk))
        return jnp.dot(w, vblk, preferred_element_type=F32)

    def near_values(old, t):
        return (values(t, w_ref[old, 0])
                + values(jnp.maximum(t - 1, 0), w_ref[old, 1], v_valid=t > 0))

    def write_out(t, acc):
        rows = pl.ds(pl.multiple_of(t * SB_T, SB_T), SB_T)
        o_ref[rows, :] = jnp.where(lane < HEAD_DIM, acc[:SB_T], acc[SB_T:]).astype(BF16)

    def stages(cur, n, t):
        old = 1 - cur
        qs_n = stack_heads(n)
        z_d = qk(qs_n, n)
        z_p = qk(qs_n, jnp.maximum(n - 1, 0))
        cs_d = cumsum(hi_ref[old, 0])
        cs_p = cumsum(hi_ref[old, 1])
        write_out(t, near_values(old, t))
        go_ref[t] = (col_max(carry_w_ref[old]) >= SB_STOP).astype(jnp.int32)

        zl, zr = quarters(z_d)
        zl = jnp.concatenate([cut(zl[:SB_H]), zl[SB_H:SB_T],
                              cut(zl[SB_T:SB_T + SB_H]), zl[SB_T + SB_H:]], axis=0)
        zr = jnp.concatenate([cut(zr[:SB_H]), cut(zr[SB_H:])], axis=0)
        hi_l, lbl_ref[cur], sp0_ref[cur, 0] = softplus_parts(zl)
        hi_r, lbr_ref[cur], _ = softplus_parts(zr)
        hi_ref[cur, 0] = unquarter(hi_l, hi_r)
        hi_ref[cur, 1], lbp_ref[cur], sp0_ref[cur, 1] = softplus_parts(z_p)

        cs_l, cs_r = quarters(cs_d)
        w_l = jnp.exp(lbl_ref[old] + cs_l).astype(BF16)
        w_r = jnp.exp(lbr_ref[old] + cs_r).astype(BF16)
        carry_d = cs_d[:, 0:1] - sp0_ref[old, 0]
        w_ref[cur, 0] = unquarter(w_l, w_r)
        w_ref[cur, 1] = jnp.exp(lbp_ref[old] + cs_p + carry_d).astype(BF16)
        carry_w_ref[cur] = carry_d + (cs_p[:, 0:1] - sp0_ref[old, 1])

    def block(qs, kb, diagonal=False):
        z = qk(qs, kb)
        if diagonal:
            q_pos = lax.broadcasted_iota(jnp.int32, z.shape, 0) & (SB_T - 1)
            k_pos = lax.broadcasted_iota(jnp.int32, z.shape, 1)
            z = z + jnp.where(k_pos < q_pos, 0.0, NEG_BIG).astype(F32)
        hi, log_beta, sp0 = softplus_parts(z)
        cs = cumsum(hi)
        carry = carry_ref[...]
        w = jnp.exp(log_beta + cs + carry).astype(BF16)
        acc_ref[...] += values(kb, w)
        carry_ref[...] = carry + (cs[:, 0:1] - sp0)

    def cond(state):
        kb, go = state
        return jnp.logical_and(kb >= 0, go > 0)

    def some_steps(i, _):
        for j in range(SB_UNROLL):
            s = SB_UNROLL * i + j
            stages(j % 2, jnp.minimum(s, last), jnp.maximum(s - 2, 0))
        return 0

    lax.fori_loop(0, (last + 1 + 2) // SB_UNROLL, some_steps, 0)
    flagged = lax.fori_loop(2, last + 1, lambda t, f: f | go_ref[t], jnp.int32(0))

    def redo(t, _):
        @pl.when(go_ref[t] > 0)
        def _():
            acc_ref[...] = jnp.zeros_like(acc_ref)
            carry_ref[...] = jnp.zeros_like(carry_ref)
            block(stack_heads(t), t, diagonal=True)

            def body(state):
                kb, _ = state
                block(stack_heads(t), kb)
                go = (jnp.max(carry_ref[...]) >= SB_STOP).astype(jnp.int32)
                return kb - 1, go

            lax.while_loop(cond, body, (t - 1, jnp.int32(1)))
            write_out(t, acc_ref[...])
        return 0

    @pl.when(flagged > 0)
    def _():
        lax.fori_loop(2, last + 1, redo, 0)


def _sb_tri():
    j = np.arange(SB_T)[:, None]
    s = np.arange(SB_T)[None, :]
    return jnp.asarray(np.where(j > s, -1.0, 0.0).astype(np.float32), BF16)


def _sb_attention(proj):
    assert SB_UNROLL % 2 == 0 and (SEQ // SB_T + 2) % SB_UNROLL == 0
    pair = lambda off: pl.BlockSpec((SEQ, LANES), lambda p: (0, off // LANES + p))
    return pl.pallas_call(
        _sb_kernel,
        grid=(SB_WIDTH // LANES,),
        in_specs=[pair(OFF_Q_SB), pair(OFF_K_SB), pair(OFF_V_SB),
                  pl.BlockSpec((SB_T, SB_T), lambda p: (0, 0))],
        out_specs=pl.BlockSpec((SEQ, LANES), lambda p: (0, p)),
        out_shape=jax.ShapeDtypeStruct((SEQ, SB_WIDTH), BF16),
        scratch_shapes=[
            pltpu.VMEM((2, 2, 2 * SB_T, SB_T), BF16),
            pltpu.VMEM((2, 2 * SB_T, SB_H), F32),
            pltpu.VMEM((2, SB_T, SB_H), F32),
            pltpu.VMEM((2, 2 * SB_T, SB_T), F32),
            pltpu.VMEM((2, 2, 2 * SB_T, 1), F32),
            pltpu.VMEM((2, 2, 2 * SB_T, SB_T), BF16),
            pltpu.VMEM((2, 2 * SB_T, 1), F32),
            pltpu.SMEM((SEQ // SB_T,), jnp.int32),
            pltpu.VMEM((2 * SB_T, LANES), F32),
            pltpu.VMEM((2 * SB_T, 1), F32),
        ],
        compiler_params=pltpu.CompilerParams(
            dimension_semantics=("arbitrary",), vmem_limit_bytes=VMEM_LIMIT),
        name="sb_attention",
    )(proj, proj, proj, _sb_tri())


def _bucket_table():
    qi = np.arange(BLOCK)[:, None]
    ci = np.arange(2 * BLOCK)[None, :]
    dist = qi + BLOCK - ci
    n = np.maximum(dist, 0)
    max_exact = N_BUCKETS // 2
    nf = np.maximum(n, 1).astype(np.float64)
    val = np.log(nf / max_exact) / math.log(MAX_DISTANCE / max_exact) * (N_BUCKETS - max_exact)
    in_window = (dist >= 0) & (dist < WINDOW)
    frac = val - np.floor(val)
    chk = in_window & (n > max_exact)
    assert np.all((frac[chk] > 1e-3) & (frac[chk] < 1 - 1e-3))
    large = np.minimum(max_exact + val.astype(np.int64), N_BUCKETS - 1)
    bucket = np.where(n < max_exact, n, large)
    later = np.where(in_window, bucket, -1)
    assert all(np.array_equal(later[r], np.roll(later[0], r)) for r in range(BLOCK))
    return jnp.asarray(np.broadcast_to(later[0:1], (8, 2 * BLOCK)).astype(np.int32))


def _bias_kernel(rb_ref, bucket_ref, o_ref):
    b = bucket_ref[...]
    hits = [b == k for k in range(N_BUCKETS)]
    col = lax.broadcasted_iota(jnp.int32, (BLOCK, 2 * BLOCK), 1)
    for h in range(SW_HEADS):
        row0 = jnp.full(b.shape, NEG_BIG, F32)
        for k in range(N_BUCKETS):
            row0 = jnp.where(hits[k], rb_ref[k, h], row0)
        tile = pltpu.roll(jnp.concatenate([row0] * (BLOCK // 8), axis=0), 0, 1,
                          stride=1, stride_axis=0)
        o_ref[1, h] = tile
        o_ref[0, h] = jnp.where(col >= BLOCK, tile, NEG_BIG)


def _bias_table(rel_bias):
    return pl.pallas_call(
        _bias_kernel,
        in_specs=[
            pl.BlockSpec(memory_space=pltpu.SMEM),
            pl.BlockSpec(memory_space=pltpu.VMEM),
        ],
        out_specs=pl.BlockSpec(memory_space=pltpu.VMEM),
        out_shape=jax.ShapeDtypeStruct((2, SW_HEADS, BLOCK, 2 * BLOCK), F32),
        name="t5_bias",
    )(rel_bias.astype(F32), _bucket_table())


def _sw_kernel(sink_ref, qa_ref, qb_ref, k0_ref, k1_ref, k2_ref, v0_ref, v1_ref, v2_ref, bias_ref,
               o_ref, p_ref, st_ref):
    g = pl.program_id(0)

    @pl.when(g == 0)
    def _():
        p_ref[...] = jnp.zeros_like(p_ref)
        st_ref[...] = jnp.zeros_like(st_ref)

    lane = lax.broadcasted_iota(jnp.int32, (BLOCK, LANES), 1)
    ones = jnp.ones((2 * BLOCK, LANES), BF16)
    scale = jnp.asarray(SCALE, BF16)
    pairs_per_kv = SW_HEADS // SW_KV_HEADS // 2
    dims = (((1,), (1,)), ((), ()))

    def swap_halves(t):
        return jnp.concatenate([t[:, HEAD_DIM:], t[:, :HEAD_DIM]], axis=1)

    def on_side(kv):
        return (lane >= HEAD_DIM) if kv % 2 else (lane < HEAD_DIM)

    def kv_cols(kv):
        return slice((kv // 2) * LANES, (kv // 2 + 1) * LANES)

    def stages(cur, q_ref, kp_ref, kc_ref, vp_ref, vc_ref, out_rows):
        old = 1 - cur
        k = jnp.concatenate([kp_ref[...], kc_ref[...]], axis=0)
        v = jnp.concatenate([vp_ref[...], vc_ref[...]], axis=0)
        logits, heads = [], []
        for kv in range(SW_KV_HEADS):
            side = kv % 2
            same, other, hs, ho = [], [], [], []
            for pp in range(pairs_per_kv):
                p = kv * pairs_per_kv + pp
                q_pair = q_ref[:, p * LANES:(p + 1) * LANES] * scale
                zero = jnp.zeros_like(q_pair)
                same.append(jnp.where(on_side(kv), q_pair, zero))
                other.append(jnp.where(on_side(kv), swap_halves(q_pair), zero))
                hs.append(2 * p + side)
                ho.append(2 * p + 1 - side)
            lhs = jnp.concatenate(same + other, axis=0)
            logits.append(lax.dot_general(lhs, k[:, kv_cols(kv)], dims,
                                          preferred_element_type=F32))
            heads.append(hs + ho)
        results = []
        for kv in range(SW_KV_HEADS):
            v_pair = v[:, kv_cols(kv)]
            half = pairs_per_kv * BLOCK
            r_same = jnp.dot(p_ref[old, kv, :half, :], jnp.concatenate([v_pair, ones], axis=1),
                             preferred_element_type=F32)
            r_other = jnp.dot(p_ref[old, kv, half:, :],
                              jnp.concatenate([swap_halves(v_pair), ones], axis=1),
                              preferred_element_type=F32)
            results.append((r_same, r_other))

        for kv in range(SW_KV_HEADS):
            maxes = []
            for i, h in enumerate(heads[kv]):
                lg = logits[kv][i * BLOCK:(i + 1) * BLOCK, :] + bias_ref[0, h]
                m = jnp.maximum(jnp.max(lg, axis=-1, keepdims=True), sink_ref[h])
                p_ref[cur, kv, i * BLOCK:(i + 1) * BLOCK, :] = jnp.exp(lg - m).astype(BF16)
                maxes.append(m)
            for pp in range(pairs_per_kv):
                h_same, h_other = heads[kv][pp], heads[kv][pairs_per_kv + pp]
                m = jnp.where(on_side(kv), maxes[pp], maxes[pairs_per_kv + pp])
                sink = jnp.where(on_side(kv), sink_ref[h_same], sink_ref[h_other])
                st_ref[cur, kv * pairs_per_kv + pp] = jnp.exp(sink - m)

        for kv in range(SW_KV_HEADS):
            r_same, r_other = results[kv]
            for pp in range(pairs_per_kv):
                p = kv * pairs_per_kv + pp
                rows = slice(pp * BLOCK, (pp + 1) * BLOCK)
                num = jnp.where(on_side(kv), r_same[rows, :LANES], r_other[rows, :LANES])
                den = jnp.where(on_side(kv), r_same[rows, LANES:], r_other[rows, LANES:])
                o_ref[out_rows, p * LANES:(p + 1) * LANES] = (
                    num / (den + st_ref[old, p])).astype(BF16)

    @pl.when(g >= 0)
    def _():
        stages(1, qa_ref, k0_ref, k1_ref, v0_ref, v1_ref, slice(0, BLOCK))

    @pl.when(g >= 0)
    def _():
        stages(0, qb_ref, k1_ref, k2_ref, v1_ref, v2_ref, slice(BLOCK, 2 * BLOCK))


def _sw_attention(proj, sinks, bias):
    kq = OFF_Q_SW // SW_WIDTH
    kk = OFF_K_SW // SW_KV_WIDTH
    kv = OFF_V_SW // SW_KV_WIDTH
    last = SEQ // BLOCK - 1
    blk = lambda n: jnp.clip(n, 0, last)
    q_spec = lambda off: pl.BlockSpec((BLOCK, SW_WIDTH), lambda g: (blk(2 * g + off), kq))
    k_spec = lambda off: pl.BlockSpec((BLOCK, SW_KV_WIDTH), lambda g: (blk(2 * g + off), kk))
    v_spec = lambda off: pl.BlockSpec((BLOCK, SW_KV_WIDTH), lambda g: (blk(2 * g + off), kv))
    return pl.pallas_call(
        _sw_kernel,
        grid=((last + 1) // 2 + 1,),
        in_specs=[
            pl.BlockSpec(memory_space=pltpu.SMEM),
            q_spec(-1), q_spec(0),
            k_spec(-2), k_spec(-1), k_spec(0),
            v_spec(-3), v_spec(-2), v_spec(-1),
            pl.BlockSpec((1, SW_HEADS, BLOCK, 2 * BLOCK), lambda g: (jnp.minimum(g, 1), 0, 0, 0)),
        ],
        out_specs=pl.BlockSpec((2 * BLOCK, SW_WIDTH), lambda g: (jnp.maximum(g - 1, 0), 0)),
        out_shape=jax.ShapeDtypeStruct((SEQ, SW_WIDTH), BF16),
        scratch_shapes=[
            pltpu.VMEM((2, SW_KV_HEADS, SW_HEADS // SW_KV_HEADS * BLOCK, 2 * BLOCK), BF16),
            pltpu.VMEM((2, SW_HEADS // 2, BLOCK, LANES), F32),
        ],
        compiler_params=pltpu.CompilerParams(
            dimension_semantics=("arbitrary",), vmem_limit_bytes=VMEM_LIMIT),
        name="sw_attention",
    )(sinks.astype(F32), proj, proj, proj, proj, proj, proj, proj, proj, bias)


def _rms(x, g):
    ms = jnp.mean(x * x, axis=-1, keepdims=True)
    return (x * lax.rsqrt(ms + RMS_EPS)) * g


def _silu(z):
    return z * (1.0 / (1.0 + jnp.exp(-z)))


def _out_kernel(osb_ref, osw_ref, zsb_ref, zswa_ref, zswb_ref, gsb_ref, gsw_ref, gpost_ref,
                w_hbm, x_ref, o_ref, w_ref, stage_ref, sem):
    @pl.when(pl.program_id(0) == 0)
    def _():
        _cast_weights(w_hbm, w_ref, stage_ref, sem)

    chunk = OUT_TM // OUT_CHUNKS
    ys = []
    for c in range(OUT_CHUNKS):
        rows = pl.ds(c * chunk, chunk)
        a_sb = _rms(osb_ref[rows, :].astype(F32), gsb_ref[...]) * zsb_ref[rows, :].astype(F32)
        g_sw = jnp.concatenate([zswa_ref[rows, :], zswb_ref[rows, :]], axis=1).astype(F32)
        a_sw = _rms(osw_ref[rows, :].astype(F32), gsw_ref[...]) * g_sw
        y = jnp.dot(a_sb.astype(BF16), w_ref[:SB_WIDTH, :], preferred_element_type=F32)
        ys.append(y + jnp.dot(a_sw.astype(BF16), w_ref[SB_WIDTH:, :],
                              preferred_element_type=F32))
    for c in range(OUT_CHUNKS):
        rows = pl.ds(c * chunk, chunk)
        o_ref[rows, :] = x_ref[rows, :] + _rms(ys[c], gpost_ref[...])


def _outproj(o_sb, o_sw, proj, gn_sb, gn_sw, norm_post, w_out, x2):
    half = SW_WIDTH // 2
    return pl.pallas_call(
        _out_kernel,
        grid=(SEQ // OUT_TM,),
        in_specs=[
            pl.BlockSpec((OUT_TM, SB_WIDTH), lambda i: (i, 0)),
            pl.BlockSpec((OUT_TM, SW_WIDTH), lambda i: (i, 0)),
            pl.BlockSpec((OUT_TM, SB_WIDTH), lambda i: (i, OFF_Z_SB // SB_WIDTH)),
            pl.BlockSpec((OUT_TM, half), lambda i: (i, OFF_Z_SW // half)),
            pl.BlockSpec((OUT_TM, half), lambda i: (i, OFF_Z_SW // half + 1)),
            pl.BlockSpec((1, SB_WIDTH), lambda i: (0, 0)),
            pl.BlockSpec((1, SW_WIDTH), lambda i: (0, 0)),
            pl.BlockSpec((1, D_MODEL), lambda i: (0, 0)),
            pl.BlockSpec(memory_space=pl.ANY),
            pl.BlockSpec((OUT_TM, D_MODEL), lambda i: (i, 0)),
        ],
        out_specs=pl.BlockSpec((OUT_TM, D_MODEL), lambda i: (i, 0)),
        out_shape=jax.ShapeDtypeStruct((SEQ, D_MODEL), F32),
        scratch_shapes=[
            pltpu.VMEM((D_MIX, D_MODEL), BF16),
            pltpu.VMEM((CAST_SLOTS, OUT_CAST_ROWS, D_MODEL), F32),
            pltpu.SemaphoreType.DMA((CAST_SLOTS,)),
        ],
        compiler_params=pltpu.CompilerParams(
            dimension_semantics=("arbitrary",), vmem_limit_bytes=VMEM_LIMIT),
        name="outproj",
    )(o_sb, o_sw, proj, proj, proj, gn_sb, gn_sw, norm_post, w_out, x2)


def kernel(x, w_in, w_out, norm_pre, norm_post, gn_sb, gn_sw, sinks, rel_bias):
    assert x.shape == (1, SEQ, D_MODEL) and w_in.shape == (1, D_MODEL, D_IN_PROJ)
    assert OFF_Z_SW % (SW_WIDTH // 2) == 0 and OFF_Q_SW % SW_WIDTH == 0
    x2 = x.reshape(SEQ, D_MODEL)
    proj = _inproj(x2, norm_pre.astype(F32), w_in[0].astype(F32))
    o_sb = _sb_attention(proj)
    bias = _bias_table(rel_bias)
    o_sw = _sw_attention(proj, sinks[0], bias)
    out = _outproj(o_sb, o_sw, proj, gn_sb.astype(F32), gn_sw.astype(F32), norm_post.astype(F32),
                   w_out[0].astype(F32), x2)
    return out.reshape(1, SEQ, D_MODEL)
```

```python
import math

import numpy as np
import jax
import jax.numpy as jnp
from jax import lax
from jax.experimental import pallas as pl
from jax.experimental.pallas import tpu as pltpu

D_MODEL = 2048
SEQ = 16384
HEAD_DIM = 64
SB_HEADS = 16
SW_HEADS = 16
SW_KV_HEADS = 4
SB_WIDTH = SB_HEADS * HEAD_DIM
SW_WIDTH = SW_HEADS * HEAD_DIM
SW_KV_WIDTH = SW_KV_HEADS * HEAD_DIM
D_MIX = SB_WIDTH + SW_WIDTH
D_IN_PROJ = 4 * SB_WIDTH + 2 * SW_WIDTH + 2 * SW_KV_WIDTH
WINDOW = 128
BLOCK = 128
N_BUCKETS = 32
MAX_DISTANCE = 128
RMS_EPS = 1e-6
SCALE = HEAD_DIM ** -0.5

OFF_Q_SB = 0
OFF_K_SB = SB_WIDTH
OFF_V_SB = 2 * SB_WIDTH
OFF_Z_SB = 3 * SB_WIDTH
OFF_Q_SW = 4 * SB_WIDTH
OFF_K_SW = OFF_Q_SW + SW_WIDTH
OFF_V_SW = OFF_K_SW + SW_KV_WIDTH
OFF_Z_SW = OFF_V_SW + SW_KV_WIDTH

LANES = 128
VMEM_LIMIT = 56 * 1024 * 1024

PROJ_TM = 512
PROJ_TN = 3328
CAST_ROWS = 16
OUT_CAST_ROWS = 32
CAST_SLOTS = 8
SB_T = 256
SB_H = SB_T // 2
OUT_TM = 512
OUT_CHUNKS = 2
NEG_BIG = -1e30
SB_UNROLL = 22
SB_STOP = -110.0

F32 = jnp.float32
BF16 = jnp.bfloat16


def _cast_weights(w_hbm, w_ref, stage_ref, sem):
    slots, rows = stage_ref.shape[0], stage_ref.shape[1]
    n_chunks = w_hbm.shape[0] // rows
    ahead = slots - 1

    def chunk_copy(c):
        slot = c % slots
        src = w_hbm.at[pl.ds(pl.multiple_of(c * rows, rows), rows), :]
        return pltpu.make_async_copy(src, stage_ref.at[slot], sem.at[slot])

    for c in range(ahead):
        chunk_copy(c).start()

    def body(c, _):
        @pl.when(c + ahead < n_chunks)
        def _():
            chunk_copy(c + ahead).start()

        chunk_copy(c).wait()
        w_ref[pl.ds(pl.multiple_of(c * rows, rows), rows), :] = stage_ref[c % slots].astype(BF16)
        return 0

    lax.fori_loop(0, n_chunks, body, 0)


def _inproj_kernel(x_ref, g_ref, w_hbm, o_ref, w_ref, stage_ref, sem):
    @pl.when(pl.program_id(0) == 0)
    def _():
        _cast_weights(w_hbm, w_ref, stage_ref, sem)

    x = x_ref[...]
    ms = jnp.mean(x * x, axis=-1, keepdims=True)
    xn = ((x * lax.rsqrt(ms + RMS_EPS)) * g_ref[...]).astype(BF16)
    gates = ((OFF_Z_SB, OFF_Z_SB + SB_WIDTH), (OFF_Z_SW, OFF_Z_SW + SW_WIDTH))
    for j in reversed(range(D_IN_PROJ // PROJ_TN)):
        lo, hi = j * PROJ_TN, (j + 1) * PROJ_TN
        res = jnp.dot(xn, w_ref[:, lo:hi], preferred_element_type=F32)
        cuts = sorted({lo, hi} | {min(max(c, lo), hi) for g in gates for c in g})
        for a, b in zip(cuts[:-1], cuts[1:]):
            part = res[:, a - lo:b - lo]
            if any(g0 <= a and b <= g1 for g0, g1 in gates):
                part = _silu(part)
            o_ref[:, a:b] = part.astype(BF16)


def _inproj(x2, g, w):
    return pl.pallas_call(
        _inproj_kernel,
        grid=(SEQ // PROJ_TM,),
        in_specs=[
            pl.BlockSpec((PROJ_TM, D_MODEL), lambda i: (i, 0)),
            pl.BlockSpec((1, D_MODEL), lambda i: (0, 0)),
            pl.BlockSpec(memory_space=pl.ANY),
        ],
        out_specs=pl.BlockSpec((PROJ_TM, D_IN_PROJ), lambda i: (i, 0)),
        out_shape=jax.ShapeDtypeStruct((SEQ, D_IN_PROJ), BF16),
        scratch_shapes=[
            pltpu.VMEM((D_MODEL, D_IN_PROJ), BF16),
            pltpu.VMEM((CAST_SLOTS, CAST_ROWS, D_IN_PROJ), F32),
            pltpu.SemaphoreType.DMA((CAST_SLOTS,)),
        ],
        compiler_params=pltpu.CompilerParams(
            dimension_semantics=("arbitrary",), vmem_limit_bytes=VMEM_LIMIT),
        name="inproj",
    )(x2, g, w)


def _sb_kernel(q_ref, k_ref, v_ref, tri_ref, o_ref,
               hi_ref, lbl_ref, lbr_ref, lbp_ref, sp0_ref, w_ref, carry_w_ref, go_ref,
               acc_ref, carry_ref):
    last = SEQ // SB_T - 1
    for ref in (hi_ref, lbl_ref, lbr_ref, lbp_ref, sp0_ref, w_ref, carry_w_ref):
        ref[...] = jnp.zeros_like(ref)

    lane = lax.broadcasted_iota(jnp.int32, (SB_T, LANES), 1)

    def stack_heads(tile):
        start = pl.multiple_of(tile * SB_T, SB_T)
        q = q_ref[pl.ds(start, SB_T), :] * jnp.asarray(SCALE, BF16)
        zero = jnp.zeros_like(q)
        return jnp.concatenate([jnp.where(lane < HEAD_DIM, q, zero),
                                jnp.where(lane >= HEAD_DIM, q, zero)], axis=0)

    tri = tri_ref[...]
    r_h = lax.broadcasted_iota(jnp.int32, (SB_H, SB_H), 0)
    c_h = lax.broadcasted_iota(jnp.int32, (SB_H, SB_H), 1)
    cut_add = jnp.where(c_h < r_h, 0.0, NEG_BIG).astype(F32)
    zeros_h = jnp.zeros((SB_H, SB_H), BF16)

    def qk(qs, kb):
        start = pl.multiple_of(kb * SB_T, SB_T)
        kblk = k_ref[pl.ds(start, SB_T), :]
        return lax.dot_general(qs, kblk, (((1,), (1,)), ((), ())), preferred_element_type=F32)

    def softplus_parts(z):
        neg_abs = lax.bitcast_convert_type(
            lax.bitcast_convert_type(z, jnp.int32) | jnp.int32(-2 ** 31), F32)
        sp = jnp.maximum(z, 0.0) + jnp.log(1.0 + jnp.exp(neg_abs))
        return sp.astype(BF16), z - sp, sp[:, 0:1]

    def cut(z):
        return z + cut_add

    def col_max(c):
        return jnp.max(jnp.max(c.reshape(c.shape[0] // 8, 8, 1), axis=0))

    def quarters(a):
        left = a[:, :SB_H]
        right = jnp.concatenate([a[SB_H:SB_T, SB_H:], a[SB_T + SB_H:, SB_H:]], axis=0)
        return left, right

    def unquarter(left, right):
        right_full = jnp.concatenate([zeros_h, right[:SB_H], zeros_h, right[SB_H:]], axis=0)
        return jnp.concatenate([left, right_full], axis=1)

    def cumsum(hi):
        return jnp.dot(hi, tri, preferred_element_type=F32)

    def values(kb, w, v_valid=None):
        start = pl.multiple_of(kb * SB_T, SB_T)
        vblk = v_ref[pl.ds(start, SB_T), :]
        if v_valid is not None:
            vblk = jnp.where(v_valid, vblk, jnp.zeros_like(vblk))
        return jnp.dot(w, vblk, preferred_element_type=F32)

    def near_values(old, t):
        return (values(t, w_ref[old, 0])
                + values(jnp.maximum(t - 1, 0), w_ref[old, 1], v_valid=t > 0))

    def write_out(t, acc):
        rows = pl.ds(pl.multiple_of(t * SB_T, SB_T), SB_T)
        o_ref[rows, :] = jnp.where(lane < HEAD_DIM, acc[:SB_T], acc[SB_T:]).astype(BF16)

    def stages(cur, n, t):
        old = 1 - cur
        qs_n = stack_heads(n)
        z_d = qk(qs_n, n)
        z_p = qk(qs_n, jnp.maximum(n - 1, 0))
        cs_d = cumsum(hi_ref[old, 0])
        cs_p = cumsum(hi_ref[old, 1])
        write_out(t, near_values(old, t))
        go_ref[t] = (col_max(carry_w_ref[old]) >= SB_STOP).astype(jnp.int32)

        zl, zr = quarters(z_d)
        zl = jnp.concatenate([cut(zl[:SB_H]), zl[SB_H:SB_T],
                              cut(zl[SB_T:SB_T + SB_H]), zl[SB_T + SB_H:]], axis=0)
        zr = jnp.concatenate([cut(zr[:SB_H]), cut(zr[SB_H:])], axis=0)
        hi_l, lbl_ref[cur], sp0_ref[cur, 0] = softplus_parts(zl)
        hi_r, lbr_ref[cur], _ = softplus_parts(zr)
        hi_ref[cur, 0] = unquarter(hi_l, hi_r)
        hi_ref[cur, 1], lbp_ref[cur], sp0_ref[cur, 1] = softplus_parts(z_p)

        cs_l, cs_r = quarters(cs_d)
        w_l = jnp.exp(lbl_ref[old] + cs_l).astype(BF16)
        w_r = jnp.exp(lbr_ref[old] + cs_r).astype(BF16)
        carry_d = cs_d[:, 0:1] - sp0_ref[old, 0]
        w_ref[cur, 0] = unquarter(w_l, w_r)
        w_ref[cur, 1] = jnp.exp(lbp_ref[old] + cs_p + carry_d).astype(BF16)
        carry_w_ref[cur] = carry_d + (cs_p[:, 0:1] - sp0_ref[old, 1])

    def block(qs, kb, diagonal=False):
        z = qk(qs, kb)
        if diagonal:
            q_pos = lax.broadcasted_iota(jnp.int32, z.shape, 0) & (SB_T - 1)
            k_pos = lax.broadcasted_iota(jnp.int32, z.shape, 1)
            z = z + jnp.where(k_pos < q_pos, 0.0, NEG_BIG).astype(F32)
        hi, log_beta, sp0 = softplus_parts(z)
        cs = cumsum(hi)
        carry = carry_ref[...]
        w = jnp.exp(log_beta + cs + carry).astype(BF16)
        acc_ref[...] += values(kb, w)
        carry_ref[...] = carry + (cs[:, 0:1] - sp0)

    def cond(state):
        kb, go = state
        return jnp.logical_and(kb >= 0, go > 0)

    def some_steps(i, _):
        for j in range(SB_UNROLL):
            s = SB_UNROLL * i + j
            stages(j % 2, jnp.minimum(s, last), jnp.maximum(s - 2, 0))
        return 0

    lax.fori_loop(0, (last + 1 + 2) // SB_UNROLL, some_steps, 0)
    flagged = lax.fori_loop(2, last + 1, lambda t, f: f | go_ref[t], jnp.int32(0))

    def redo(t, _):
        @pl.when(go_ref[t] > 0)
        def _():
            acc_ref[...] = jnp.zeros_like(acc_ref)
            carry_ref[...] = jnp.zeros_like(carry_ref)
            block(stack_heads(t), t, diagonal=True)

            def body(state):
                kb, _ = state
                block(stack_heads(t), kb)
                go = (jnp.max(carry_ref[...]) >= SB_STOP).astype(jnp.int32)
                return kb - 1, go

            lax.while_loop(cond, body, (t - 1, jnp.int32(1)))
            write_out(t, acc_ref[...])
        return 0

    @pl.when(flagged > 0)
    def _():
        lax.fori_loop(2, last + 1, redo, 0)


def _sb_tri():
    j = np.arange(SB_T)[:, None]
    s = np.arange(SB_T)[None, :]
    return jnp.asarray(np.where(j > s, -1.0, 0.0).astype(np.float32), BF16)


def _sb_attention(proj):
    assert SB_UNROLL % 2 == 0 and (SEQ // SB_T + 2) % SB_UNROLL == 0
    pair = lambda off: pl.BlockSpec((SEQ, LANES), lambda p: (0, off // LANES + p))
    return pl.pallas_call(
        _sb_kernel,
        grid=(SB_WIDTH // LANES,),
        in_specs=[pair(OFF_Q_SB), pair(OFF_K_SB), pair(OFF_V_SB),
                  pl.BlockSpec((SB_T, SB_T), lambda p: (0, 0))],
        out_specs=pl.BlockSpec((SEQ, LANES), lambda p: (0, p)),
        out_shape=jax.ShapeDtypeStruct((SEQ, SB_WIDTH), BF16),
        scratch_shapes=[
            pltpu.VMEM((2, 2, 2 * SB_T, SB_T), BF16),
            pltpu.VMEM((2, 2 * SB_T, SB_H), F32),
            pltpu.VMEM((2, SB_T, SB_H), F32),
            pltpu.VMEM((2, 2 * SB_T, SB_T), F32),
            pltpu.VMEM((2, 2, 2 * SB_T, 1), F32),
            pltpu.VMEM((2, 2, 2 * SB_T, SB_T), BF16),
            pltpu.VMEM((2, 2 * SB_T, 1), F32),
            pltpu.SMEM((SEQ // SB_T,), jnp.int32),
            pltpu.VMEM((2 * SB_T, LANES), F32),
            pltpu.VMEM((2 * SB_T, 1), F32),
        ],
        compiler_params=pltpu.CompilerParams(
            dimension_semantics=("arbitrary",), vmem_limit_bytes=VMEM_LIMIT),
        name="sb_attention",
    )(proj, proj, proj, _sb_tri())


def _bucket_table():
    qi = np.arange(BLOCK)[:, None]
    ci = np.arange(2 * BLOCK)[None, :]
    dist = qi + BLOCK - ci
    n = np.maximum(dist, 0)
    max_exact = N_BUCKETS // 2
    nf = np.maximum(n, 1).astype(np.float64)
    val = np.log(nf / max_exact) / math.log(MAX_DISTANCE / max_exact) * (N_BUCKETS - max_exact)
    in_window = (dist >= 0) & (dist < WINDOW)
    frac = val - np.floor(val)
    chk = in_window & (n > max_exact)
    assert np.all((frac[chk] > 1e-3) & (frac[chk] < 1 - 1e-3))
    large = np.minimum(max_exact + val.astype(np.int64), N_BUCKETS - 1)
    bucket = np.where(n < max_exact, n, large)
    later = np.where(in_window, bucket, -1)
    assert all(np.array_equal(later[r], np.roll(later[0], r)) for r in range(BLOCK))
    return jnp.asarray(np.broadcast_to(later[0:1], (8, 2 * BLOCK)).astype(np.int32))


def _bias_kernel(rb_ref, bucket_ref, o_ref):
    b = bucket_ref[...]
    hits = [b == k for k in range(N_BUCKETS)]
    col = lax.broadcasted_iota(jnp.int32, (BLOCK, 2 * BLOCK), 1)
    for h in range(SW_HEADS):
        row0 = jnp.full(b.shape, NEG_BIG, F32)
        for k in range(N_BUCKETS):
            row0 = jnp.where(hits[k], rb_ref[k, h], row0)
        tile = pltpu.roll(jnp.concatenate([row0] * (BLOCK // 8), axis=0), 0, 1,
                          stride=1, stride_axis=0)
        o_ref[1, h] = tile
        o_ref[0, h] = jnp.where(col >= BLOCK, tile, NEG_BIG)


def _bias_table(rel_bias):
    return pl.pallas_call(
        _bias_kernel,
        in_specs=[
            pl.BlockSpec(memory_space=pltpu.SMEM),
            pl.BlockSpec(memory_space=pltpu.VMEM),
        ],
        out_specs=pl.BlockSpec(memory_space=pltpu.VMEM),
        out_shape=jax.ShapeDtypeStruct((2, SW_HEADS, BLOCK, 2 * BLOCK), F32),
        name="t5_bias",
    )(rel_bias.astype(F32), _bucket_table())


def _sw_kernel(sink_ref, qa_ref, qb_ref, k0_ref, k1_ref, k2_ref, v0_ref, v1_ref, v2_ref, bias_ref,
               o_ref, p_ref, st_ref):
    g = pl.program_id(0)

    @pl.when(g == 0)
    def _():
        p_ref[...] = jnp.zeros_like(p_ref)
        st_ref[...] = jnp.zeros_like(st_ref)

    lane = lax.broadcasted_iota(jnp.int32, (BLOCK, LANES), 1)
    ones = jnp.ones((2 * BLOCK, LANES), BF16)
    scale = jnp.asarray(SCALE, BF16)
    pairs_per_kv = SW_HEADS // SW_KV_HEADS // 2
    dims = (((1,), (1,)), ((), ()))

    def swap_halves(t):
        return jnp.concatenate([t[:, HEAD_DIM:], t[:, :HEAD_DIM]], axis=1)

    def on_side(kv):
        return (lane >= HEAD_DIM) if kv % 2 else (lane < HEAD_DIM)

    def kv_cols(kv):
        return slice((kv // 2) * LANES, (kv // 2 + 1) * LANES)

    def stages(cur, q_ref, kp_ref, kc_ref, vp_ref, vc_ref, out_rows):
        old = 1 - cur
        k = jnp.concatenate([kp_ref[...], kc_ref[...]], axis=0)
        v = jnp.concatenate([vp_ref[...], vc_ref[...]], axis=0)
        logits, heads = [], []
        for kv in range(SW_KV_HEADS):
            side = kv % 2
            same, other, hs, ho = [], [], [], []
            for pp in range(pairs_per_kv):
                p = kv * pairs_per_kv + pp
                q_pair = q_ref[:, p * LANES:(p + 1) * LANES] * scale
                zero = jnp.zeros_like(q_pair)
                same.append(jnp.where(on_side(kv), q_pair, zero))
                other.append(jnp.where(on_side(kv), swap_halves(q_pair), zero))
                hs.append(2 * p + side)
                ho.append(2 * p + 1 - side)
            lhs = jnp.concatenate(same + other, axis=0)
            logits.append(lax.dot_general(lhs, k[:, kv_cols(kv)], dims,
                                          preferred_element_type=F32))
            heads.append(hs + ho)
        results = []
        for kv in range(SW_KV_HEADS):
            v_pair = v[:, kv_cols(kv)]
            half = pairs_per_kv * BLOCK
            r_same = jnp.dot(p_ref[old, kv, :half, :], jnp.concatenate([v_pair, ones], axis=1),
                             preferred_element_type=F32)
            r_other = jnp.dot(p_ref[old, kv, half:, :],
                              jnp.concatenate([swap_halves(v_pair), ones], axis=1),
                              preferred_element_type=F32)
            results.append((r_same, r_other))

        for kv in range(SW_KV_HEADS):
            maxes = []
            for i, h in enumerate(heads[kv]):
                lg = logits[kv][i * BLOCK:(i + 1) * BLOCK, :] + bias_ref[0, h]
                m = jnp.maximum(jnp.max(lg, axis=-1, keepdims=True), sink_ref[h])
                p_ref[cur, kv, i * BLOCK:(i + 1) * BLOCK, :] = jnp.exp(lg - m).astype(BF16)
                maxes.append(m)
            for pp in range(pairs_per_kv):
                h_same, h_other = heads[kv][pp], heads[kv][pairs_per_kv + pp]
                m = jnp.where(on_side(kv), maxes[pp], maxes[pairs_per_kv + pp])
                sink = jnp.where(on_side(kv), sink_ref[h_same], sink_ref[h_other])
                st_ref[cur, kv * pairs_per_kv + pp] = jnp.exp(sink - m)

        for kv in range(SW_KV_HEADS):
            r_same, r_other = results[kv]
            for pp in range(pairs_per_kv):
                p = kv * pairs_per_kv + pp
                rows = slice(pp * BLOCK, (pp + 1) * BLOCK)
                num = jnp.where(on_side(kv), r_same[rows, :LANES], r_other[rows, :LANES])
                den = jnp.where(on_side(kv), r_same[rows, LANES:], r_other[rows, LANES:])
                o_ref[out_rows, p * LANES:(p + 1) * LANES] = (
                    num / (den + st_ref[old, p])).astype(BF16)

    @pl.when(g >= 0)
    def _():
        stages(1, qa_ref, k0_ref, k1_ref, v0_ref, v1_ref, slice(0, BLOCK))

    @pl.when(g >= 0)
    def _():
        stages(0, qb_ref, k1_ref, k2_ref, v1_ref, v2_ref, slice(BLOCK, 2 * BLOCK))


def _sw_attention(proj, sinks, bias):
    kq = OFF_Q_SW // SW_WIDTH
    kk = OFF_K_SW // SW_KV_WIDTH
    kv = OFF_V_SW // SW_KV_WIDTH
    last = SEQ // BLOCK - 1
    blk = lambda n: jnp.clip(n, 0, last)
    q_spec = lambda off: pl.BlockSpec((BLOCK, SW_WIDTH), lambda g: (blk(2 * g + off), kq))
    k_spec = lambda off: pl.BlockSpec((BLOCK, SW_KV_WIDTH), lambda g: (blk(2 * g + off), kk))
    v_spec = lambda off: pl.BlockSpec((BLOCK, SW_KV_WIDTH), lambda g: (blk(2 * g + off), kv))
    return pl.pallas_call(
        _sw_kernel,
        grid=((last + 1) // 2 + 1,),
        in_specs=[
            pl.BlockSpec(memory_space=pltpu.SMEM),
            q_spec(-1), q_spec(0),
            k_spec(-2), k_spec(-1), k_spec(0),
            v_spec(-3), v_spec(-2), v_spec(-1),
            pl.BlockSpec((1, SW_HEADS, BLOCK, 2 * BLOCK), lambda g: (jnp.minimum(g, 1), 0, 0, 0)),
        ],
        out_specs=pl.BlockSpec((2 * BLOCK, SW_WIDTH), lambda g: (jnp.maximum(g - 1, 0), 0)),
        out_shape=jax.ShapeDtypeStruct((SEQ, SW_WIDTH), BF16),
        scratch_shapes=[
            pltpu.VMEM((2, SW_KV_HEADS, SW_HEADS // SW_KV_HEADS * BLOCK, 2 * BLOCK), BF16),
            pltpu.VMEM((2, SW_HEADS // 2, BLOCK, LANES), F32),
        ],
        compiler_params=pltpu.CompilerParams(
            dimension_semantics=("arbitrary",), vmem_limit_bytes=VMEM_LIMIT),
        name="sw_attention",
    )(sinks.astype(F32), proj, proj, proj, proj, proj, proj, proj, proj, bias)


def _rms(x, g):
    ms = jnp.mean(x * x, axis=-1, keepdims=True)
    return (x * lax.rsqrt(ms + RMS_EPS)) * g


def _silu(z):
    return z * (1.0 / (1.0 + jnp.exp(-z)))


def _out_kernel(osb_ref, osw_ref, zsb_ref, zswa_ref, zswb_ref, gsb_ref, gsw_ref, gpost_ref,
                w_hbm, x_ref, o_ref, w_ref, stage_ref, sem):
    @pl.when(pl.program_id(0) == 0)
    def _():
        _cast_weights(w_hbm, w_ref, stage_ref, sem)

    chunk = OUT_TM // OUT_CHUNKS
    ys = []
    for c in range(OUT_CHUNKS):
        rows = pl.ds(c * chunk, chunk)
        a_sb = _rms(osb_ref[rows, :].astype(F32), gsb_ref[...]) * zsb_ref[rows, :].astype(F32)
        g_sw = jnp.concatenate([zswa_ref[rows, :], zswb_ref[rows, :]], axis=1).astype(F32)
        a_sw = _rms(osw_ref[rows, :].astype(F32), gsw_ref[...]) * g_sw
        y = jnp.dot(a_sb.astype(BF16), w_ref[:SB_WIDTH, :], preferred_element_type=F32)
        ys.append(y + jnp.dot(a_sw.astype(BF16), w_ref[SB_WIDTH:, :],
                              preferred_element_type=F32))
    for c in range(OUT_CHUNKS):
        rows = pl.ds(c * chunk, chunk)
        o_ref[rows, :] = x_ref[rows, :] + _rms(ys[c], gpost_ref[...])


def _outproj(o_sb, o_sw, proj, gn_sb, gn_sw, norm_post, w_out, x2):
    half = SW_WIDTH // 2
    return pl.pallas_call(
        _out_kernel,
        grid=(SEQ // OUT_TM,),
        in_specs=[
            pl.BlockSpec((OUT_TM, SB_WIDTH), lambda i: (i, 0)),
            pl.BlockSpec((OUT_TM, SW_WIDTH), lambda i: (i, 0)),
            pl.BlockSpec((OUT_TM, SB_WIDTH), lambda i: (i, OFF_Z_SB // SB_WIDTH)),
            pl.BlockSpec((OUT_TM, half), lambda i: (i, OFF_Z_SW // half)),
            pl.BlockSpec((OUT_TM, half), lambda i: (i, OFF_Z_SW // half + 1)),
            pl.BlockSpec((1, SB_WIDTH), lambda i: (0, 0)),
            pl.BlockSpec((1, SW_WIDTH), lambda i: (0, 0)),
            pl.BlockSpec((1, D_MODEL), lambda i: (0, 0)),
            pl.BlockSpec(memory_space=pl.ANY),
            pl.BlockSpec((OUT_TM, D_MODEL), lambda i: (i, 0)),
        ],
        out_specs=pl.BlockSpec((OUT_TM, D_MODEL), lambda i: (i, 0)),
        out_shape=jax.ShapeDtypeStruct((SEQ, D_MODEL), F32),
        scratch_shapes=[
            pltpu.VMEM((D_MIX, D_MODEL), BF16),
            pltpu.VMEM((CAST_SLOTS, OUT_CAST_ROWS, D_MODEL), F32),
            pltpu.SemaphoreType.DMA((CAST_SLOTS,)),
        ],
        compiler_params=pltpu.CompilerParams(
            dimension_semantics=("arbitrary",), vmem_limit_bytes=VMEM_LIMIT),
        name="outproj",
    )(o_sb, o_sw, proj, proj, proj, gn_sb, gn_sw, norm_post, w_out, x2)


def kernel(x, w_in, w_out, norm_pre, norm_post, gn_sb, gn_sw, sinks, rel_bias):
    assert x.shape == (1, SEQ, D_MODEL) and w_in.shape == (1, D_MODEL, D_IN_PROJ)
    assert OFF_Z_SW % (SW_WIDTH // 2) == 0 and OFF_Q_SW % SW_WIDTH == 0
    x2 = x.reshape(SEQ, D_MODEL)
    proj = _inproj(x2, norm_pre.astype(F32), w_in[0].astype(F32))
    o_sb = _sb_attention(proj)
    bias = _bias_table(rel_bias)
    o_sw = _sw_attention(proj, sinks[0], bias)
    out = _outproj(o_sb, o_sw, proj, gn_sb.astype(F32), gn_sw.astype(F32), norm_post.astype(F32),
                   w_out[0].astype(F32), x2)
    return out.reshape(1, SEQ, D_MODEL)
```

```python
import math

import numpy as np
import jax
import jax.numpy as jnp
from jax import lax
from jax.experimental import pallas as pl
from jax.experimental.pallas import tpu as pltpu

D_MODEL = 2048
SEQ = 16384
HEAD_DIM = 64
SB_HEADS = 16
SW_HEADS = 16
SW_KV_HEADS = 4
SB_WIDTH = SB_HEADS * HEAD_DIM
SW_WIDTH = SW_HEADS * HEAD_DIM
SW_KV_WIDTH = SW_KV_HEADS * HEAD_DIM
D_MIX = SB_WIDTH + SW_WIDTH
D_IN_PROJ = 4 * SB_WIDTH + 2 * SW_WIDTH + 2 * SW_KV_WIDTH
WINDOW = 128
BLOCK = 128
N_BUCKETS = 32
MAX_DISTANCE = 128
RMS_EPS = 1e-6
SCALE = HEAD_DIM ** -0.5

OFF_Q_SB = 0
OFF_K_SB = SB_WIDTH
OFF_V_SB = 2 * SB_WIDTH
OFF_Z_SB = 3 * SB_WIDTH
OFF_Q_SW = 4 * SB_WIDTH
OFF_K_SW = OFF_Q_SW + SW_WIDTH
OFF_V_SW = OFF_K_SW + SW_KV_WIDTH
OFF_Z_SW = OFF_V_SW + SW_KV_WIDTH

LANES = 128
VMEM_LIMIT = 56 * 1024 * 1024

PROJ_TM = 512
PROJ_TN = 3328
CAST_ROWS = 16
OUT_CAST_ROWS = 32
CAST_SLOTS = 8
SB_T = 256
SB_H = SB_T // 2
OUT_TM = 512
OUT_CHUNKS = 2
NEG_BIG = -1e30
SB_UNROLL = 22
SB_STOP = -110.0

F32 = jnp.float32
BF16 = jnp.bfloat16


def _cast_weights(w_hbm, w_ref, stage_ref, sem):
    slots, rows = stage_ref.shape[0], stage_ref.shape[1]
    n_chunks = w_hbm.shape[0] // rows
    ahead = slots - 1

    def chunk_copy(c):
        slot = c % slots
        src = w_hbm.at[pl.ds(pl.multiple_of(c * rows, rows), rows), :]
        return pltpu.make_async_copy(src, stage_ref.at[slot], sem.at[slot])

    for c in range(ahead):
        chunk_copy(c).start()

    def body(c, _):
        @pl.when(c + ahead < n_chunks)
        def _():
            chunk_copy(c + ahead).start()

        chunk_copy(c).wait()
        w_ref[pl.ds(pl.multiple_of(c * rows, rows), rows), :] = stage_ref[c % slots].astype(BF16)
        return 0

    lax.fori_loop(0, n_chunks, body, 0)


def _inproj_kernel(x_ref, g_ref, w_hbm, o_ref, w_ref, stage_ref, sem):
    @pl.when(pl.program_id(0) == 0)
    def _():
        _cast_weights(w_hbm, w_ref, stage_ref, sem)

    x = x_ref[...]
    ms = jnp.mean(x * x, axis=-1, keepdims=True)
    xn = ((x * lax.rsqrt(ms + RMS_EPS)) * g_ref[...]).astype(BF16)
    gates = ((OFF_Z_SB, OFF_Z_SB + SB_WIDTH), (OFF_Z_SW, OFF_Z_SW + SW_WIDTH))
    for j in reversed(range(D_IN_PROJ // PROJ_TN)):
        lo, hi = j * PROJ_TN, (j + 1) * PROJ_TN
        res = jnp.dot(xn, w_ref[:, lo:hi], preferred_element_type=F32)
        cuts = sorted({lo, hi} | {min(max(c, lo), hi) for g in gates for c in g})
        for a, b in zip(cuts[:-1], cuts[1:]):
            part = res[:, a - lo:b - lo]
            if any(g0 <= a and b <= g1 for g0, g1 in gates):
                part = _silu(part)
            o_ref[:, a:b] = part.astype(BF16)


def _inproj(x2, g, w):
    return pl.pallas_call(
        _inproj_kernel,
        grid=(SEQ // PROJ_TM,),
        in_specs=[
            pl.BlockSpec((PROJ_TM, D_MODEL), lambda i: (i, 0)),
            pl.BlockSpec((1, D_MODEL), lambda i: (0, 0)),
            pl.BlockSpec(memory_space=pl.ANY),
        ],
        out_specs=pl.BlockSpec((PROJ_TM, D_IN_PROJ), lambda i: (i, 0)),
        out_shape=jax.ShapeDtypeStruct((SEQ, D_IN_PROJ), BF16),
        scratch_shapes=[
            pltpu.VMEM((D_MODEL, D_IN_PROJ), BF16),
            pltpu.VMEM((CAST_SLOTS, CAST_ROWS, D_IN_PROJ), F32),
            pltpu.SemaphoreType.DMA((CAST_SLOTS,)),
        ],
        compiler_params=pltpu.CompilerParams(
            dimension_semantics=("arbitrary",), vmem_limit_bytes=VMEM_LIMIT),
        name="inproj",
    )(x2, g, w)


def _sb_kernel(q_ref, k_ref, v_ref, tri_ref, o_ref,
               hi_ref, lbl_ref, lbr_ref, lbp_ref, sp0_ref, w_ref, carry_w_ref, go_ref,
               acc_ref, carry_ref):
    last = SEQ // SB_T - 1
    for ref in (hi_ref, lbl_ref, lbr_ref, lbp_ref, sp0_ref, w_ref, carry_w_ref):
        ref[...] = jnp.zeros_like(ref)

    lane = lax.broadcasted_iota(jnp.int32, (SB_T, LANES), 1)

    def stack_heads(tile):
        start = pl.multiple_of(tile * SB_T, SB_T)
        q = q_ref[pl.ds(start, SB_T), :] * jnp.asarray(SCALE, BF16)
        zero = jnp.zeros_like(q)
        return jnp.concatenate([jnp.where(lane < HEAD_DIM, q, zero),
                                jnp.where(lane >= HEAD_DIM, q, zero)], axis=0)

    tri = tri_ref[...]
    r_h = lax.broadcasted_iota(jnp.int32, (SB_H, SB_H), 0)
    c_h = lax.broadcasted_iota(jnp.int32, (SB_H, SB_H), 1)
    cut_add = jnp.where(c_h < r_h, 0.0, NEG_BIG).astype(F32)
    zeros_h = jnp.zeros((SB_H, SB_H), BF16)

    def qk(qs, kb):
        start = pl.multiple_of(kb * SB_T, SB_T)
        kblk = k_ref[pl.ds(start, SB_T), :]
        return lax.dot_general(qs, kblk, (((1,), (1,)), ((), ())), preferred_element_type=F32)

    def softplus_parts(z):
        neg_abs = lax.bitcast_convert_type(
            lax.bitcast_convert_type(z, jnp.int32) | jnp.int32(-2 ** 31), F32)
        sp = jnp.maximum(z, 0.0) + jnp.log(1.0 + jnp.exp(neg_abs))
        return sp.astype(BF16), z - sp, sp[:, 0:1]

    def cut(z):
        return z + cut_add

    def col_max(c):
        return jnp.max(jnp.max(c.reshape(c.shape[0] // 8, 8, 1), axis=0))

    def quarters(a):
        left = a[:, :SB_H]
        right = jnp.concatenate([a[SB_H:SB_T, SB_H:], a[SB_T + SB_H:, SB_H:]], axis=0)
        return left, right

    def unquarter(left, right):
        right_full = jnp.concatenate([zeros_h, right[:SB_H], zeros_h, right[SB_H:]], axis=0)
        return jnp.concatenate([left, right_full], axis=1)

    def cumsum(hi):
        return jnp.dot(hi, tri, preferred_element_type=F32)

    def values(kb, w, v_valid=None):
        start = pl.multiple_of(kb * SB_T, SB_T)
        vblk = v_ref[pl.ds(start, SB_T), :]
        if v_valid is not None:
            vblk = jnp.where(v_valid, vblk, jnp.zeros_like(vblk))
        return jnp.dot(w, vblk, preferred_element_type=F32)

    def near_values(old, t):
        return (values(t, w_ref[old, 0])
                + values(jnp.maximum(t - 1, 0), w_ref[old, 1], v_valid=t > 0))

    def write_out(t, acc):
        rows = pl.ds(pl.multiple_of(t * SB_T, SB_T), SB_T)
        o_ref[rows, :] = jnp.where(lane < HEAD_DIM, acc[:SB_T], acc[SB_T:]).astype(BF16)

    def stages(cur, n, t):
        old = 1 - cur
        qs_n = stack_heads(n)
        z_d = qk(qs_n, n)
        z_p = qk(qs_n, jnp.maximum(n - 1, 0))
        cs_d = cumsum(hi_ref[old, 0])
        cs_p = cumsum(hi_ref[old, 1])
        write_out(t, near_values(old, t))
        go_ref[t] = (col_max(carry_w_ref[old]) >= SB_STOP).astype(jnp.int32)

        zl, zr = quarters(z_d)
        zl = jnp.concatenate([cut(zl[:SB_H]), zl[SB_H:SB_T],
                              cut(zl[SB_T:SB_T + SB_H]), zl[SB_T + SB_H:]], axis=0)
        zr = jnp.concatenate([cut(zr[:SB_H]), cut(zr[SB_H:])], axis=0)
        hi_l, lbl_ref[cur], sp0_ref[cur, 0] = softplus_parts(zl)
        hi_r, lbr_ref[cur], _ = softplus_parts(zr)
        hi_ref[cur, 0] = unquarter(hi_l, hi_r)
        hi_ref[cur, 1], lbp_ref[cur], sp0_ref[cur, 1] = softplus_parts(z_p)

        cs_l, cs_r = quarters(cs_d)
        w_l = jnp.exp(lbl_ref[old] + cs_l).astype(BF16)
        w_r = jnp.exp(lbr_ref[old] + cs_r).astype(BF16)
        carry_d = cs_d[:, 0:1] - sp0_ref[old, 0]
        w_ref[cur, 0] = unquarter(w_l, w_r)
        w_ref[cur, 1] = jnp.exp(lbp_ref[old] + cs_p + carry_d).astype(BF16)
        carry_w_ref[cur] = carry_d + (cs_p[:, 0:1] - sp0_ref[old, 1])

    def block(qs, kb, diagonal=False):
        z = qk(qs, kb)
        if diagonal:
            q_pos = lax.broadcasted_iota(jnp.int32, z.shape, 0) & (SB_T - 1)
            k_pos = lax.broadcasted_iota(jnp.int32, z.shape, 1)
            z = z + jnp.where(k_pos < q_pos, 0.0, NEG_BIG).astype(F32)
        hi, log_beta, sp0 = softplus_parts(z)
        cs = cumsum(hi)
        carry = carry_ref[...]
        w = jnp.exp(log_beta + cs + carry).astype(BF16)
        acc_ref[...] += values(kb, w)
        carry_ref[...] = carry + (cs[:, 0:1] - sp0)

    def cond(state):
        kb, go = state
        return jnp.logical_and(kb >= 0, go > 0)

    def some_steps(i, _):
        for j in range(SB_UNROLL):
            s = SB_UNROLL * i + j
            stages(j % 2, jnp.minimum(s, last), jnp.maximum(s - 2, 0))
        return 0

    lax.fori_loop(0, (last + 1 + 2) // SB_UNROLL, some_steps, 0)
    flagged = lax.fori_loop(2, last + 1, lambda t, f: f | go_ref[t], jnp.int32(0))

    def redo(t, _):
        @pl.when(go_ref[t] > 0)
        def _():
            acc_ref[...] = jnp.zeros_like(acc_ref)
            carry_ref[...] = jnp.zeros_like(carry_ref)
            block(stack_heads(t), t, diagonal=True)

            def body(state):
                kb, _ = state
                block(stack_heads(t), kb)
                go = (jnp.max(carry_ref[...]) >= SB_STOP).astype(jnp.int32)
                return kb - 1, go

            lax.while_loop(cond, body, (t - 1, jnp.int32(1)))
            write_out(t, acc_ref[...])
        return 0

    @pl.when(flagged > 0)
    def _():
        lax.fori_loop(2, last + 1, redo, 0)


def _sb_tri():
    j = np.arange(SB_T)[:, None]
    s = np.arange(SB_T)[None, :]
    return jnp.asarray(np.where(j > s, -1.0, 0.0).astype(np.float32), BF16)


def _sb_attention(proj):
    assert SB_UNROLL % 2 == 0 and (SEQ // SB_T + 2) % SB_UNROLL == 0
    pair = lambda off: pl.BlockSpec((SEQ, LANES), lambda p: (0, off // LANES + p))
    return pl.pallas_call(
        _sb_kernel,
        grid=(SB_WIDTH // LANES,),
        in_specs=[pair(OFF_Q_SB), pair(OFF_K_SB), pair(OFF_V_SB),
                  pl.BlockSpec((SB_T, SB_T), lambda p: (0, 0))],
        out_specs=pl.BlockSpec((SEQ, LANES), lambda p: (0, p)),
        out_shape=jax.ShapeDtypeStruct((SEQ, SB_WIDTH), BF16),
        scratch_shapes=[
            pltpu.VMEM((2, 2, 2 * SB_T, SB_T), BF16),
            pltpu.VMEM((2, 2 * SB_T, SB_H), F32),
            pltpu.VMEM((2, SB_T, SB_H), F32),
            pltpu.VMEM((2, 2 * SB_T, SB_T), F32),
            pltpu.VMEM((2, 2, 2 * SB_T, 1), F32),
            pltpu.VMEM((2, 2, 2 * SB_T, SB_T), BF16),
            pltpu.VMEM((2, 2 * SB_T, 1), F32),
            pltpu.SMEM((SEQ // SB_T,), jnp.int32),
            pltpu.VMEM((2 * SB_T, LANES), F32),
            pltpu.VMEM((2 * SB_T, 1), F32),
        ],
        compiler_params=pltpu.CompilerParams(
            dimension_semantics=("arbitrary",), vmem_limit_bytes=VMEM_LIMIT),
        name="sb_attention",
    )(proj, proj, proj, _sb_tri())


def _bucket_table():
    qi = np.arange(BLOCK)[:, None]
    ci = np.arange(2 * BLOCK)[None, :]
    dist = qi + BLOCK - ci
    n = np.maximum(dist, 0)
    max_exact = N_BUCKETS // 2
    nf = np.maximum(n, 1).astype(np.float64)
    val = np.log(nf / max_exact) / math.log(MAX_DISTANCE / max_exact) * (N_BUCKETS - max_exact)
    in_window = (dist >= 0) & (dist < WINDOW)
    frac = val - np.floor(val)
    chk = in_window & (n > max_exact)
    assert np.all((frac[chk] > 1e-3) & (frac[chk] < 1 - 1e-3))
    large = np.minimum(max_exact + val.astype(np.int64), N_BUCKETS - 1)
    bucket = np.where(n < max_exact, n, large)
    later = np.where(in_window, bucket, -1)
    assert all(np.array_equal(later[r], np.roll(later[0], r)) for r in range(BLOCK))
    return jnp.asarray(np.broadcast_to(later[0:1], (8, 2 * BLOCK)).astype(np.int32))


def _bias_kernel(rb_ref, bucket_ref, o_ref):
    b = bucket_ref[...]
    hits = [b == k for k in range(N_BUCKETS)]
    col = lax.broadcasted_iota(jnp.int32, (BLOCK, 2 * BLOCK), 1)
    for h in range(SW_HEADS):
        row0 = jnp.full(b.shape, NEG_BIG, F32)
        for k in range(N_BUCKETS):
            row0 = jnp.where(hits[k], rb_ref[k, h], row0)
        tile = pltpu.roll(jnp.concatenate([row0] * (BLOCK // 8), axis=0), 0, 1,
                          stride=1, stride_axis=0)
        o_ref[1, h] = tile
        o_ref[0, h] = jnp.where(col >= BLOCK, tile, NEG_BIG)


def _bias_table(rel_bias):
    return pl.pallas_call(
        _bias_kernel,
        in_specs=[
            pl.BlockSpec(memory_space=pltpu.SMEM),
            pl.BlockSpec(memory_space=pltpu.VMEM),
        ],
        out_specs=pl.BlockSpec(memory_space=pltpu.VMEM),
        out_shape=jax.ShapeDtypeStruct((2, SW_HEADS, BLOCK, 2 * BLOCK), F32),
        name="t5_bias",
    )(rel_bias.astype(F32), _bucket_table())


def _sw_kernel(sink_ref, qa_ref, qb_ref, k0_ref, k1_ref, k2_ref, v0_ref, v1_ref, v2_ref, bias_ref,
               wout_ref, o_ref, wout_bf_ref, p_ref, st_ref):
    g = pl.program_id(0)
    wout_bf_ref[...] = wout_ref[...].astype(BF16)

    @pl.when(g == 0)
    def _():
        p_ref[...] = jnp.zeros_like(p_ref)
        st_ref[...] = jnp.zeros_like(st_ref)

    lane = lax.broadcasted_iota(jnp.int32, (BLOCK, LANES), 1)
    ones = jnp.ones((2 * BLOCK, LANES), BF16)
    scale = jnp.asarray(SCALE, BF16)
    pairs_per_kv = SW_HEADS // SW_KV_HEADS // 2
    dims = (((1,), (1,)), ((), ()))

    def swap_halves(t):
        return jnp.concatenate([t[:, HEAD_DIM:], t[:, :HEAD_DIM]], axis=1)

    def on_side(kv):
        return (lane >= HEAD_DIM) if kv % 2 else (lane < HEAD_DIM)

    def kv_cols(kv):
        return slice((kv // 2) * LANES, (kv // 2 + 1) * LANES)

    def stages(cur, q_ref, kp_ref, kc_ref, vp_ref, vc_ref, out_rows):
        old = 1 - cur
        k = jnp.concatenate([kp_ref[...], kc_ref[...]], axis=0)
        v = jnp.concatenate([vp_ref[...], vc_ref[...]], axis=0)
        logits, heads = [], []
        for kv in range(SW_KV_HEADS):
            side = kv % 2
            same, other, hs, ho = [], [], [], []
            for pp in range(pairs_per_kv):
                p = kv * pairs_per_kv + pp
                q_pair = q_ref[:, p * LANES:(p + 1) * LANES] * scale
                zero = jnp.zeros_like(q_pair)
                same.append(jnp.where(on_side(kv), q_pair, zero))
                other.append(jnp.where(on_side(kv), swap_halves(q_pair), zero))
                hs.append(2 * p + side)
                ho.append(2 * p + 1 - side)
            lhs = jnp.concatenate(same + other, axis=0)
            logits.append(lax.dot_general(lhs, k[:, kv_cols(kv)], dims,
                                          preferred_element_type=F32))
            heads.append(hs + ho)
        results = []
        for kv in range(SW_KV_HEADS):
            v_pair = v[:, kv_cols(kv)]
            half = pairs_per_kv * BLOCK
            r_same = jnp.dot(p_ref[old, kv, :half, :], jnp.concatenate([v_pair, ones], axis=1),
                             preferred_element_type=F32)
            r_other = jnp.dot(p_ref[old, kv, half:, :],
                              jnp.concatenate([swap_halves(v_pair), ones], axis=1),
                              preferred_element_type=F32)
            results.append((r_same, r_other))

        for kv in range(SW_KV_HEADS):
            maxes = []
            for i, h in enumerate(heads[kv]):
                lg = logits[kv][i * BLOCK:(i + 1) * BLOCK, :] + bias_ref[0, h]
                m = jnp.maximum(jnp.max(lg, axis=-1, keepdims=True), sink_ref[h])
                p_ref[cur, kv, i * BLOCK:(i + 1) * BLOCK, :] = jnp.exp(lg - m).astype(BF16)
                maxes.append(m)
            for pp in range(pairs_per_kv):
                h_same, h_other = heads[kv][pp], heads[kv][pairs_per_kv + pp]
                m = jnp.where(on_side(kv), maxes[pp], maxes[pairs_per_kv + pp])
                sink = jnp.where(on_side(kv), sink_ref[h_same], sink_ref[h_other])
                st_ref[cur, kv * pairs_per_kv + pp] = jnp.exp(sink - m)

        for kv in range(SW_KV_HEADS):
            r_same, r_other = results[kv]
            for pp in range(pairs_per_kv):
                p = kv * pairs_per_kv + pp
                rows = slice(pp * BLOCK, (pp + 1) * BLOCK)
                num = jnp.where(on_side(kv), r_same[rows, :LANES], r_other[rows, :LANES])
                den = jnp.where(on_side(kv), r_same[rows, LANES:], r_other[rows, LANES:])
                o_ref[out_rows, p * LANES:(p + 1) * LANES] = (
                    num / (den + st_ref[old, p])).astype(BF16)

    @pl.when(g >= 0)
    def _():
        stages(1, qa_ref, k0_ref, k1_ref, v0_ref, v1_ref, slice(0, BLOCK))

    @pl.when(g >= 0)
    def _():
        stages(0, qb_ref, k1_ref, k2_ref, v1_ref, v2_ref, slice(BLOCK, 2 * BLOCK))


def _sw_attention(proj, sinks, bias, w_out):
    steps = (SEQ // BLOCK) // 2 + 1
    w_rows = 32
    assert D_MIX % w_rows == 0 and D_MIX // w_rows <= steps
    w_spec = pl.BlockSpec((w_rows, D_MODEL), lambda g: (jnp.minimum(g, D_MIX // w_rows - 1), 0))
    kq = OFF_Q_SW // SW_WIDTH
    kk = OFF_K_SW // SW_KV_WIDTH
    kv = OFF_V_SW // SW_KV_WIDTH
    last = SEQ // BLOCK - 1
    blk = lambda n: jnp.clip(n, 0, last)
    q_spec = lambda off: pl.BlockSpec((BLOCK, SW_WIDTH), lambda g: (blk(2 * g + off), kq))
    k_spec = lambda off: pl.BlockSpec((BLOCK, SW_KV_WIDTH), lambda g: (blk(2 * g + off), kk))
    v_spec = lambda off: pl.BlockSpec((BLOCK, SW_KV_WIDTH), lambda g: (blk(2 * g + off), kv))
    return pl.pallas_call(
        _sw_kernel,
        grid=((last + 1) // 2 + 1,),
        in_specs=[
            pl.BlockSpec(memory_space=pltpu.SMEM),
            q_spec(-1), q_spec(0),
            k_spec(-2), k_spec(-1), k_spec(0),
            v_spec(-3), v_spec(-2), v_spec(-1),
            pl.BlockSpec((1, SW_HEADS, BLOCK, 2 * BLOCK), lambda g: (jnp.minimum(g, 1), 0, 0, 0)),
            w_spec,
        ],
        out_specs=[pl.BlockSpec((2 * BLOCK, SW_WIDTH), lambda g: (jnp.maximum(g - 1, 0), 0)),
                   w_spec],
        out_shape=[jax.ShapeDtypeStruct((SEQ, SW_WIDTH), BF16),
                   jax.ShapeDtypeStruct((D_MIX, D_MODEL), BF16)],
        scratch_shapes=[
            pltpu.VMEM((2, SW_KV_HEADS, SW_HEADS // SW_KV_HEADS * BLOCK, 2 * BLOCK), BF16),
            pltpu.VMEM((2, SW_HEADS // 2, BLOCK, LANES), F32),
        ],
        compiler_params=pltpu.CompilerParams(
            dimension_semantics=("arbitrary",), vmem_limit_bytes=VMEM_LIMIT),
        name="sw_attention",
    )(sinks.astype(F32), proj, proj, proj, proj, proj, proj, proj, proj, bias, w_out)


def _rms(x, g):
    ms = jnp.mean(x * x, axis=-1, keepdims=True)
    return (x * lax.rsqrt(ms + RMS_EPS)) * g


def _silu(z):
    return z * (1.0 / (1.0 + jnp.exp(-z)))


def _out_kernel(osb_ref, osw_ref, zsb_ref, zswa_ref, zswb_ref, gsb_ref, gsw_ref, gpost_ref,
                w_ref, x_ref, o_ref):
    chunk = OUT_TM // OUT_CHUNKS
    ys = []
    for c in range(OUT_CHUNKS):
        rows = pl.ds(c * chunk, chunk)
        a_sb = _rms(osb_ref[rows, :].astype(F32), gsb_ref[...]) * zsb_ref[rows, :].astype(F32)
        g_sw = jnp.concatenate([zswa_ref[rows, :], zswb_ref[rows, :]], axis=1).astype(F32)
        a_sw = _rms(osw_ref[rows, :].astype(F32), gsw_ref[...]) * g_sw
        y = jnp.dot(a_sb.astype(BF16), w_ref[:SB_WIDTH, :], preferred_element_type=F32)
        ys.append(y + jnp.dot(a_sw.astype(BF16), w_ref[SB_WIDTH:, :],
                              preferred_element_type=F32))
    for c in range(OUT_CHUNKS):
        rows = pl.ds(c * chunk, chunk)
        o_ref[rows, :] = x_ref[rows, :] + _rms(ys[c], gpost_ref[...])


def _outproj(o_sb, o_sw, proj, gn_sb, gn_sw, norm_post, w_out, x2):
    half = SW_WIDTH // 2
    return pl.pallas_call(
        _out_kernel,
        grid=(SEQ // OUT_TM,),
        in_specs=[
            pl.BlockSpec((OUT_TM, SB_WIDTH), lambda i: (i, 0)),
            pl.BlockSpec((OUT_TM, SW_WIDTH), lambda i: (i, 0)),
            pl.BlockSpec((OUT_TM, SB_WIDTH), lambda i: (i, OFF_Z_SB // SB_WIDTH)),
            pl.BlockSpec((OUT_TM, half), lambda i: (i, OFF_Z_SW // half)),
            pl.BlockSpec((OUT_TM, half), lambda i: (i, OFF_Z_SW // half + 1)),
            pl.BlockSpec((1, SB_WIDTH), lambda i: (0, 0)),
            pl.BlockSpec((1, SW_WIDTH), lambda i: (0, 0)),
            pl.BlockSpec((1, D_MODEL), lambda i: (0, 0)),
            pl.BlockSpec((D_MIX, D_MODEL), lambda i: (0, 0), pipeline_mode=pl.Buffered(1)),
            pl.BlockSpec((OUT_TM, D_MODEL), lambda i: (i, 0)),
        ],
        out_specs=pl.BlockSpec((OUT_TM, D_MODEL), lambda i: (i, 0)),
        out_shape=jax.ShapeDtypeStruct((SEQ, D_MODEL), F32),
        compiler_params=pltpu.CompilerParams(
            dimension_semantics=("arbitrary",), vmem_limit_bytes=VMEM_LIMIT),
        name="outproj",
    )(o_sb, o_sw, proj, proj, proj, gn_sb, gn_sw, norm_post, w_out, x2)


def kernel(x, w_in, w_out, norm_pre, norm_post, gn_sb, gn_sw, sinks, rel_bias):
    assert x.shape == (1, SEQ, D_MODEL) and w_in.shape == (1, D_MODEL, D_IN_PROJ)
    assert OFF_Z_SW % (SW_WIDTH // 2) == 0 and OFF_Q_SW % SW_WIDTH == 0
    x2 = x.reshape(SEQ, D_MODEL)
    proj = _inproj(x2, norm_pre.astype(F32), w_in[0].astype(F32))
    o_sb = _sb_attention(proj)
    bias = _bias_table(rel_bias)
    o_sw, w_out_bf = _sw_attention(proj, sinks[0], bias, w_out[0].astype(F32))
    out = _outproj(o_sb, o_sw, proj, gn_sb.astype(F32), gn_sw.astype(F32), norm_post.astype(F32),
                   w_out_bf, x2)
    return out.reshape(1, SEQ, D_MODEL)
```
